```python
import math
import jax
import jax.numpy as jnp
from jax import lax
import numpy as np

D_MODEL = 2048
BATCH = 1
SEQ = 8192
DEPTH = 4

N_MIXERS = 4
EPS = 1e-6
NEG_INF = -1e30

MLA_HEADS = 16
MLA_Q_RANK = 512
MLA_KV_RANK = 512
MLA_NOPE = 128
MLA_ROPE = 64
MLA_V = 128
MLA_IN_DIM = MLA_Q_RANK + MLA_KV_RANK + MLA_ROPE
ROPE_THETA = 10000.0
Q_BLOCK = 128

GDN_QK_HEADS = 16
GDN_V_HEADS = 32
GDN_DK = 128
GDN_DV = 128
GDN_CONV = 4
GDN_CHUNK = 64
GDN_QK_DIM = GDN_QK_HEADS * GDN_DK
GDN_V_DIM = GDN_V_HEADS * GDN_DV
GDN_QKV_DIM = 2 * GDN_QK_DIM + GDN_V_DIM
GDN_PROJ_DIM = GDN_QKV_DIM + GDN_V_DIM + 2 * GDN_V_HEADS

S5_WIDTH = D_MODEL // 2
S5_GROUP = 16
S5_GROUPS = S5_WIDTH // S5_GROUP
S5_STATE = 64
S5_DT_MIN = 1e-3
S5_DT_MAX = 1e-1

CONV_CH = D_MODEL
CONV_K = 31

FFN_HIDDEN = -(-8 * D_MODEL // (3 * 256)) * 256

kernel_name = 'hybrid_interleaved_mla_gdn_s5_convmod'


def rmsnorm(x, g):
    xf = x.astype(jnp.float32)
    y = xf * lax.rsqrt(jnp.mean(xf * xf, axis=-1, keepdims=True) + EPS)
    return (y * g.astype(jnp.float32)).astype(x.dtype)


def layernorm(x, g, b):
    xf = x.astype(jnp.float32)
    mu = jnp.mean(xf, axis=-1, keepdims=True)
    var = jnp.mean(jnp.square(xf - mu), axis=-1, keepdims=True)
    y = (xf - mu) * lax.rsqrt(var + EPS)
    return (y * g.astype(jnp.float32) + b.astype(jnp.float32)).astype(x.dtype)


def l2norm(x):
    return x * lax.rsqrt(jnp.sum(x * x, axis=-1, keepdims=True) + EPS)


def apply_rope(x, positions):
    half = x.shape[-1] // 2
    inv_freq = ROPE_THETA ** (-jnp.arange(half, dtype=jnp.float32) / half)
    ang = positions.astype(jnp.float32)[..., None] * inv_freq
    ang = ang.reshape(ang.shape[:2] + (1,) * (x.ndim - 3) + (half,))
    cos, sin = jnp.cos(ang), jnp.sin(ang)
    xf = x.astype(jnp.float32)
    x1, x2 = xf[..., :half], xf[..., half:]
    return jnp.concatenate([x1 * cos - x2 * sin, x2 * cos + x1 * sin], axis=-1).astype(x.dtype)


def causal_depthwise_conv(x, w):
    k = w.shape[0]
    xp = jnp.pad(x, ((0, 0), (k - 1, 0), (0, 0)))
    return lax.conv_general_dilated(xp, w[:, None, :].astype(x.dtype), (1,), 'VALID',
                                    dimension_numbers=('NWC', 'WIO', 'NWC'),
                                    feature_group_count=x.shape[-1])


def causal_block_attention(q, k, v, scale):
    b, s, h, dq = q.shape
    nb = s // Q_BLOCK
    qb = q.reshape(b, nb, Q_BLOCK, h, dq).transpose(1, 0, 2, 3, 4)
    starts = jnp.arange(nb, dtype=jnp.int32) * Q_BLOCK
    kpos = jnp.arange(s, dtype=jnp.int32)

    def one_block(args):
        qi, start = args
        sc = jnp.einsum('bqhd,bkhd->bhqk', qi, k, preferred_element_type=jnp.float32) * scale
        qpos = start + jnp.arange(Q_BLOCK, dtype=jnp.int32)
        sc = jnp.where(kpos[None, :] <= qpos[:, None], sc, NEG_INF)
        p = jax.nn.softmax(sc, axis=-1)
        return jnp.einsum('bhqk,bkhd->bqhd', p.astype(v.dtype), v)

    o = lax.map(one_block, (qb, starts))
    return o.transpose(1, 0, 2, 3, 4).reshape(b, s, h, v.shape[-1])


def mla_mixer(h, positions, w_in, q_norm, w_uq, kv_norm, w_ukv, w_o):
    b, s, _ = h.shape
    proj = h @ w_in
    q_lat = proj[..., :MLA_Q_RANK]
    kv_lat = proj[..., MLA_Q_RANK:MLA_Q_RANK + MLA_KV_RANK]
    k_rope = apply_rope(proj[..., MLA_Q_RANK + MLA_KV_RANK:], positions)
    q = (rmsnorm(q_lat, q_norm) @ w_uq).reshape(b, s, MLA_HEADS, MLA_NOPE + MLA_ROPE)
    q = jnp.concatenate([q[..., :MLA_NOPE], apply_rope(q[..., MLA_NOPE:], positions)], axis=-1)
    kv = (rmsnorm(kv_lat, kv_norm) @ w_ukv).reshape(b, s, MLA_HEADS, MLA_NOPE + MLA_V)
    k_rope_h = jnp.broadcast_to(k_rope[:, :, None, :], (b, s, MLA_HEADS, MLA_ROPE))
    k = jnp.concatenate([kv[..., :MLA_NOPE], k_rope_h], axis=-1)
    v = kv[..., MLA_NOPE:]
    o = causal_block_attention(q, k, v, (MLA_NOPE + MLA_ROPE) ** -0.5)
    return o.reshape(b, s, MLA_HEADS * MLA_V) @ w_o


def chunk_gated_delta_rule(q, k, v, g, beta):
    b, s, h, dk = q.shape
    dv = v.shape[-1]
    c = GDN_CHUNK
    n = s // c

    def to_chunks(t):
        return t.reshape(b, n, c, h, t.shape[-1]).transpose(0, 3, 1, 2, 4)

    q, k, v = to_chunks(q), to_chunks(k), to_chunks(v)
    g = g.reshape(b, n, c, h).transpose(0, 3, 1, 2)
    beta = beta.reshape(b, n, c, h).transpose(0, 3, 1, 2)
    gc = jnp.cumsum(g, axis=-1)
    tri_incl = jnp.tril(jnp.ones((c, c), dtype=bool))
    tri_strict = jnp.tril(jnp.ones((c, c), dtype=bool), k=-1)
    decay = jnp.exp(jnp.where(tri_incl, gc[..., :, None] - gc[..., None, :], NEG_INF))
    kb = k * beta[..., None]
    a_strict = jnp.where(tri_strict, jnp.einsum('bhncd,bhnmd->bhncm', kb, k) * decay, 0.0)
    lhs = a_strict + jnp.eye(c, dtype=jnp.float32)
    u = lax.linalg.triangular_solve(lhs, v * beta[..., None], left_side=True, lower=True, unit_diagonal=True)
    w = lax.linalg.triangular_solve(lhs, kb * jnp.exp(gc)[..., None], left_side=True, lower=True, unit_diagonal=True)
    attn_qk = jnp.einsum('bhncd,bhnmd->bhncm', q, k) * decay

    def step(state, xs):
        q_c, k_c, u_c, w_c, gc_c, attn_c = xs
        v_new = u_c - jnp.einsum('bhcd,bhde->bhce', w_c, state)
        o_c = (jnp.einsum('bhcd,bhde->bhce', q_c * jnp.exp(gc_c)[..., None], state)
               + jnp.einsum('bhcm,bhme->bhce', attn_c, v_new))
        g_last = gc_c[..., -1]
        k_dec = k_c * jnp.exp(g_last[..., None] - gc_c)[..., None]
        state = state * jnp.exp(g_last)[..., None, None] + jnp.einsum('bhcd,bhce->bhde', k_dec, v_new)
        return state, o_c

    xs = tuple(jnp.moveaxis(t, 2, 0) for t in (q, k, u, w, gc, attn_qk))
    _, o = lax.scan(step, jnp.zeros((b, h, dk, dv), jnp.float32), xs)
    return o.transpose(1, 0, 3, 2, 4).reshape(b, s, h, dv)


def gdn_mixer(h, w_in, conv_w, a_log, dt_bias, o_norm, w_o):
    b, s, _ = h.shape
    f32 = jnp.float32
    proj = h @ w_in
    qkv = jax.nn.silu(causal_depthwise_conv(proj[..., :GDN_QKV_DIM], conv_w))
    q = qkv[..., :GDN_QK_DIM].reshape(b, s, GDN_QK_HEADS, GDN_DK)
    k = qkv[..., GDN_QK_DIM:2 * GDN_QK_DIM].reshape(b, s, GDN_QK_HEADS, GDN_DK)
    v = qkv[..., 2 * GDN_QK_DIM:].reshape(b, s, GDN_V_HEADS, GDN_DV)
    off = GDN_QKV_DIM
    z = proj[..., off:off + GDN_V_DIM].reshape(b, s, GDN_V_HEADS, GDN_DV)
    beta_logit = proj[..., off + GDN_V_DIM:off + GDN_V_DIM + GDN_V_HEADS]
    alpha = proj[..., off + GDN_V_DIM + GDN_V_HEADS:]
    rep = GDN_V_HEADS // GDN_QK_HEADS
    q = jnp.repeat(l2norm(q.astype(f32)), rep, axis=2) * (GDN_DK ** -0.5)
    k = jnp.repeat(l2norm(k.astype(f32)), rep, axis=2)
    beta = jax.nn.sigmoid(beta_logit.astype(f32))
    g = -jnp.exp(a_log.astype(f32)) * jax.nn.softplus(alpha.astype(f32) + dt_bias.astype(f32))
    o = chunk_gated_delta_rule(q, k, v.astype(f32), g, beta)
    o = rmsnorm(o, o_norm) * jax.nn.silu(z.astype(f32))
    return o.reshape(b, s, GDN_V_DIM).astype(h.dtype) @ w_o


def s5_mixer(h, w_in, lam_re, lam_im, log_dt, b_re, b_im, c_re, c_im, d_skip, w_out):
    bsz, s, _ = h.shape
    f32 = jnp.float32
    lam_re, lam_im = lam_re.astype(f32), lam_im.astype(f32)
    b_re, b_im = b_re.astype(f32), b_im.astype(f32)
    c_re, c_im = c_re.astype(f32), c_im.astype(f32)
    u = (h @ w_in).astype(f32)
    ug = u.reshape(bsz, s, S5_GROUPS, S5_GROUP)
    dt = jnp.exp(log_dt.astype(f32))[:, None]
    mag = jnp.exp(lam_re * dt)
    ang = lam_im * dt
    lb_re, lb_im = mag * jnp.cos(ang), mag * jnp.sin(ang)
    den = lam_re * lam_re + lam_im * lam_im
    nr = lb_re - 1.0
    f_re = (nr * lam_re + lb_im * lam_im) / den
    f_im = (lb_im * lam_re - nr * lam_im) / den
    bb_re = f_re[..., None] * b_re - f_im[..., None] * b_im
    bb_im = f_re[..., None] * b_im + f_im[..., None] * b_re
    bu_re = jnp.einsum('gpc,bsgc->bsgp', bb_re, ug)
    bu_im = jnp.einsum('gpc,bsgc->bsgp', bb_im, ug)
    a_re = jnp.broadcast_to(lb_re, bu_re.shape)
    a_im = jnp.broadcast_to(lb_im, bu_im.shape)

    def combine(e1, e2):
        a1r, a1i, x1r, x1i = e1
        a2r, a2i, x2r, x2i = e2
        return (a2r * a1r - a2i * a1i, a2r * a1i + a2i * a1r,
                a2r * x1r - a2i * x1i + x2r, a2r * x1i + a2i * x1r + x2i)

    _, _, xr, xi = lax.associative_scan(combine, (a_re, a_im, bu_re, bu_im), axis=1)
    y = jnp.einsum('gcp,bsgp->bsgc', c_re, xr) - jnp.einsum('gcp,bsgp->bsgc', c_im, xi)
    y = y.reshape(bsz, s, S5_WIDTH) + d_skip.astype(f32) * u
    gl = jax.nn.gelu(y).astype(h.dtype) @ w_out
    val, gate = gl[..., :D_MODEL], gl[..., D_MODEL:]
    return val * jax.nn.sigmoid(gate)


def conv_module_mixer(h, w_in, dw_w, dw_b, ln_g, ln_b, w_out):
    p = h @ w_in
    xg = p[..., :CONV_CH] * jax.nn.sigmoid(p[..., CONV_CH:])
    xg = causal_depthwise_conv(xg, dw_w) + dw_b.astype(xg.dtype)
    xg = jax.nn.silu(layernorm(xg, ln_g, ln_b))
    return xg @ w_out


def swiglu_ffn(h, w_gate, w_up, w_down):
    return (jax.nn.silu(h @ w_gate) * (h @ w_up)) @ w_down


def _dense(key, shape, fan_in):
    return jax.random.normal(key, shape, jnp.float32) * (fan_in ** -0.5)


def _gain(key, shape):
    return 1.0 + 0.02 * jax.random.normal(key, shape, jnp.float32)


def setup_inputs(seed: int = 0) -> dict:
    key = jax.random.key(seed)
    ks = iter(jax.random.split(key, 64))
    na, nb, nc, nd = [len(range(m, DEPTH, N_MIXERS)) for m in range(N_MIXERS)]
    f32 = jnp.float32
    d = D_MODEL
    x = jax.random.normal(next(ks), (BATCH, SEQ, d), f32)
    positions = jnp.broadcast_to(jnp.arange(SEQ, dtype=jnp.int32), (BATCH, SEQ))
    norm_mix = _gain(next(ks), (DEPTH, d))
    norm_ffn = _gain(next(ks), (DEPTH, d))
    final_norm = _gain(next(ks), (d,))
    mla_w_in = _dense(next(ks), (na, d, MLA_IN_DIM), d)
    mla_q_norm = _gain(next(ks), (na, MLA_Q_RANK))
    mla_w_uq = _dense(next(ks), (na, MLA_Q_RANK, MLA_HEADS * (MLA_NOPE + MLA_ROPE)), MLA_Q_RANK)
    mla_kv_norm = _gain(next(ks), (na, MLA_KV_RANK))
    mla_w_ukv = _dense(next(ks), (na, MLA_KV_RANK, MLA_HEADS * (MLA_NOPE + MLA_V)), MLA_KV_RANK)
    mla_w_o = _dense(next(ks), (na, MLA_HEADS * MLA_V, d), MLA_HEADS * MLA_V)
    gdn_w_in = _dense(next(ks), (nb, d, GDN_PROJ_DIM), d)
    gdn_conv_w = _dense(next(ks), (nb, GDN_CONV, GDN_QKV_DIM), GDN_CONV)
    gdn_a_log = jnp.log(jax.random.uniform(next(ks), (nb, GDN_V_HEADS), f32, 1.0, 16.0))
    dt0 = jnp.exp(jax.random.uniform(next(ks), (nb, GDN_V_HEADS), f32, math.log(1e-3), math.log(1e-1)))
    gdn_dt_bias = dt0 + jnp.log(-jnp.expm1(-dt0))
    gdn_o_norm = _gain(next(ks), (nb, GDN_DV))
    gdn_w_o = _dense(next(ks), (nb, GDN_V_DIM, d), GDN_V_DIM)
    s5_w_in = _dense(next(ks), (nc, d, S5_WIDTH), d)
    s5_lam_re = -0.5 + 0.01 * jax.random.normal(next(ks), (nc, S5_GROUPS, S5_STATE), f32)
    s5_lam_im = jnp.broadcast_to(math.pi * jnp.arange(S5_STATE, dtype=f32), (nc, S5_GROUPS, S5_STATE))
    s5_log_dt = jax.random.uniform(next(ks), (nc, S5_GROUPS), f32, math.log(S5_DT_MIN), math.log(S5_DT_MAX))
    s5_b_re = _dense(next(ks), (nc, S5_GROUPS, S5_STATE, S5_GROUP), 2 * S5_GROUP)
    s5_b_im = _dense(next(ks), (nc, S5_GROUPS, S5_STATE, S5_GROUP), 2 * S5_GROUP)
    s5_c_re = _dense(next(ks), (nc, S5_GROUPS, S5_GROUP, S5_STATE), 2 * S5_STATE)
    s5_c_im = _dense(next(ks), (nc, S5_GROUPS, S5_GROUP, S5_STATE), 2 * S5_STATE)
    s5_d = jax.random.normal(next(ks), (nc, S5_WIDTH), f32)
    s5_w_out = _dense(next(ks), (nc, S5_WIDTH, 2 * d), S5_WIDTH)
    cv_w_in = _dense(next(ks), (nd, d, 2 * CONV_CH), d)
    cv_dw_w = _dense(next(ks), (nd, CONV_K, CONV_CH), CONV_K)
    cv_dw_b = 0.02 * jax.random.normal(next(ks), (nd, CONV_CH), f32)
    cv_ln_g = _gain(next(ks), (nd, CONV_CH))
    cv_ln_b = 0.02 * jax.random.normal(next(ks), (nd, CONV_CH), f32)
    cv_w_out = _dense(next(ks), (nd, CONV_CH, d), CONV_CH)
    ffn_w_gate = _dense(next(ks), (DEPTH, d, FFN_HIDDEN), d)
    ffn_w_up = _dense(next(ks), (DEPTH, d, FFN_HIDDEN), d)
    ffn_w_down = _dense(next(ks), (DEPTH, FFN_HIDDEN, d), FFN_HIDDEN)
    return {'x': x, 'positions': positions, 'norm_mix': norm_mix, 'norm_ffn': norm_ffn,
            'final_norm': final_norm,
            'mla_w_in': mla_w_in, 'mla_q_norm': mla_q_norm, 'mla_w_uq': mla_w_uq,
            'mla_kv_norm': mla_kv_norm, 'mla_w_ukv': mla_w_ukv, 'mla_w_o': mla_w_o,
            'gdn_w_in': gdn_w_in, 'gdn_conv_w': gdn_conv_w, 'gdn_a_log': gdn_a_log,
            'gdn_dt_bias': gdn_dt_bias, 'gdn_o_norm': gdn_o_norm, 'gdn_w_o': gdn_w_o,
            's5_w_in': s5_w_in, 's5_lam_re': s5_lam_re, 's5_lam_im': s5_lam_im, 's5_log_dt': s5_log_dt,
            's5_b_re': s5_b_re, 's5_b_im': s5_b_im, 's5_c_re': s5_c_re, 's5_c_im': s5_c_im,
            's5_d': s5_d, 's5_w_out': s5_w_out,
            'cv_w_in': cv_w_in, 'cv_dw_w': cv_dw_w, 'cv_dw_b': cv_dw_b, 'cv_ln_g': cv_ln_g,
            'cv_ln_b': cv_ln_b, 'cv_w_out': cv_w_out,
            'ffn_w_gate': ffn_w_gate, 'ffn_w_up': ffn_w_up, 'ffn_w_down': ffn_w_down}


def reference(x, positions, norm_mix, norm_ffn, final_norm,
              mla_w_in, mla_q_norm, mla_w_uq, mla_kv_norm, mla_w_ukv, mla_w_o,
              gdn_w_in, gdn_conv_w, gdn_a_log, gdn_dt_bias, gdn_o_norm, gdn_w_o,
              s5_w_in, s5_lam_re, s5_lam_im, s5_log_dt, s5_b_re, s5_b_im, s5_c_re, s5_c_im,
              s5_d, s5_w_out,
              cv_w_in, cv_dw_w, cv_dw_b, cv_ln_g, cv_ln_b, cv_w_out,
              ffn_w_gate, ffn_w_up, ffn_w_down):
    for i in range(DEPTH):
        m = i % N_MIXERS
        j = i // N_MIXERS
        hn = rmsnorm(x, norm_mix[i])
        if m == 0:
            y = mla_mixer(hn, positions, mla_w_in[j], mla_q_norm[j], mla_w_uq[j],
                          mla_kv_norm[j], mla_w_ukv[j], mla_w_o[j])
        elif m == 1:
            y = gdn_mixer(hn, gdn_w_in[j], gdn_conv_w[j], gdn_a_log[j], gdn_dt_bias[j],
                          gdn_o_norm[j], gdn_w_o[j])
        elif m == 2:
            y = s5_mixer(hn, s5_w_in[j], s5_lam_re[j], s5_lam_im[j], s5_log_dt[j], s5_b_re[j],
                         s5_b_im[j], s5_c_re[j], s5_c_im[j], s5_d[j], s5_w_out[j])
        else:
            y = conv_module_mixer(hn, cv_w_in[j], cv_dw_w[j], cv_dw_b[j], cv_ln_g[j],
                                  cv_ln_b[j], cv_w_out[j])
        x = x + y.astype(x.dtype)
        hn = rmsnorm(x, norm_ffn[i])
        x = x + swiglu_ffn(hn, ffn_w_gate[i], ffn_w_up[i], ffn_w_down[i]).astype(x.dtype)
    return rmsnorm(x, final_norm)
```

```python
import functools
import math

import jax
import jax.numpy as jnp
from jax import lax
from jax.experimental import pallas as pl
from jax.experimental.pallas import tpu as pltpu

F32 = jnp.float32
BF16 = jnp.bfloat16
EPS = 1e-6
NEG_INF = -1e30

VMEM_LIMIT_BYTES = 56 * 1024 * 1024
LANES = 128

MLA_HEADS = 16
MLA_Q_RANK = 512
MLA_KV_RANK = 512
MLA_NOPE = 128
MLA_ROPE = 64
MLA_V = 128
MLA_HEAD_PAD = 256
ROPE_THETA = 10000.0

GDN_QK_HEADS = 16
GDN_V_HEADS = 32
GDN_DK = 128
GDN_DV = 128
GDN_CONV = 4
GDN_CHUNK = 64
GDN_SUPER = 256

S5_GROUP = 16
S5_STATE = 64
S5_SEGMENTS = 8

CONV_K = 31


def _params(*sem):
    return pltpu.CompilerParams(dimension_semantics=sem, vmem_limit_bytes=VMEM_LIMIT_BYTES)


def _rms(x, g):
    return x * lax.rsqrt(jnp.mean(x * x, axis=-1, keepdims=True) + EPS) * g


def _silu(x):
    return x * jax.nn.sigmoid(x)


def _mm_kernel(*refs, has_norm, n_w, act, has_res):
    it = iter(refs)
    a_ref = next(it)
    g_ref = next(it) if has_norm else None
    w_refs = [next(it) for _ in range(n_w)]
    res_ref = next(it) if has_res else None
    o_ref = next(it)
    xn_ref = next(it) if has_norm else None

    if has_norm:
        @pl.when(pl.program_id(1) == 0)
        def _():
            xn_ref[...] = _rms(a_ref[...], g_ref[...]).astype(BF16)

        a = xn_ref[...]
    else:
        a = a_ref[...]
    accs = [jnp.dot(a, w[...], preferred_element_type=F32) for w in w_refs]
    if act == "glu":
        y = accs[0] * jax.nn.sigmoid(accs[1])
    else:
        y = accs[0]
    if has_res:
        y = res_ref[...] + y
    o_ref[...] = y.astype(o_ref.dtype)


def fused_matmul(a, ws, *, gain=None, act="none", res=None, out_dtype=F32, tm=512, tn=512,
                 w_col_offsets=None, out_map=None, out_shape=None, a_block=None, a_map=None, name="mm"):
    k = ws[0][0].shape[0]
    m = a.size // k
    n = ws[0][1]
    n_w = len(ws)
    tm = min(tm, m)
    tn = min(tn, n)
    assert m % tm == 0 and n % tn == 0
    has_norm = gain is not None
    has_res = res is not None
    if w_col_offsets is None:
        w_col_offsets = [0] * n_w
    if a_block is None:
        a_block, a_map = (tm, k), (lambda i, j: (i, 0))
    in_specs = [pl.BlockSpec(a_block, a_map)]
    args = [a]
    if has_norm:
        in_specs.append(pl.BlockSpec((1, k), lambda i, j: (0, 0)))
        args.append(gain.reshape(1, k))
    for (w, _), off in zip(ws, w_col_offsets):
        assert off % tn == 0
        ob = off // tn
        in_specs.append(pl.BlockSpec((k, tn), lambda i, j, ob=ob: (0, j + ob)))
        args.append(w)
    if has_res:
        in_specs.append(pl.BlockSpec((tm, tn), lambda i, j: (i, j)))
        args.append(res)
    if out_map is None:
        out_map = lambda i, j: (i, j)
    if out_shape is None:
        out_shape = (m, n)
    scratch = [pltpu.VMEM((tm, k), BF16)] if has_norm else []
    return pl.pallas_call(
        functools.partial(_mm_kernel, has_norm=has_norm, n_w=n_w, act=act, has_res=has_res),
        out_shape=jax.ShapeDtypeStruct(out_shape, out_dtype),
        grid=(m // tm, n // tn),
        in_specs=in_specs,
        out_specs=pl.BlockSpec((tm, tn), out_map),
        scratch_shapes=scratch,
        compiler_params=_params("arbitrary", "arbitrary"),
        name=name,
    )(*args)


def _ffn_kernel(*refs, final_norm):
    if final_norm:
        x_ref, g_ref, wg_ref, wu_ref, wd_ref, fg_ref, o_ref, xn_ref = refs
    else:
        x_ref, g_ref, wg_ref, wu_ref, wd_ref, o_ref, xn_ref = refs
        fg_ref = None
    j = pl.program_id(1)

    @pl.when(j == 0)
    def _():
        x = x_ref[...]
        xn_ref[...] = _rms(x, g_ref[...]).astype(BF16)
        o_ref[...] = x

    xn = xn_ref[...]
    hg = jnp.dot(xn, wg_ref[...], preferred_element_type=F32)
    hu = jnp.dot(xn, wu_ref[...], preferred_element_type=F32)
    h = (_silu(hg) * hu).astype(BF16)
    o_ref[...] += jnp.dot(h, wd_ref[...], preferred_element_type=F32)

    if final_norm:
        @pl.when(j == pl.num_programs(1) - 1)
        def _():
            o_ref[...] = _rms(o_ref[...], fg_ref[...])


def ffn(x, gain, wg, wu, wd, final_gain=None, *, tm=512, th=512):
    m, d = x.shape
    hdim = wg.shape[1]
    assert m % tm == 0 and hdim % th == 0
    final_norm = final_gain is not None
    in_specs = [
        pl.BlockSpec((tm, d), lambda i, j: (i, 0)),
        pl.BlockSpec((1, d), lambda i, j: (0, 0)),
        pl.BlockSpec((d, th), lambda i, j: (0, j)),
        pl.BlockSpec((d, th), lambda i, j: (0, j)),
        pl.BlockSpec((th, d), lambda i, j: (j, 0)),
    ]
    args = [x, gain.reshape(1, d), wg, wu, wd]
    if final_norm:
        in_specs.append(pl.BlockSpec((1, d), lambda i, j: (0, 0)))
        args.append(final_gain.reshape(1, d))
    return pl.pallas_call(
        functools.partial(_ffn_kernel, final_norm=final_norm),
        out_shape=jax.ShapeDtypeStruct((m, d), F32),
        grid=(m // tm, hdim // th),
        in_specs=in_specs,
        out_specs=pl.BlockSpec((tm, d), lambda i, j: (i, 0)),
        scratch_shapes=[pltpu.VMEM((tm, d), BF16)],
        compiler_params=_params("arbitrary", "arbitrary"),
        name="ffn",
    )(*args)


def _rope_block(xb, c, s1, s2):
    return xb * c + pltpu.roll(xb, 96, 1) * s1 + pltpu.roll(xb, 32, 1) * s2


def _mla_proj_kernel(x_ref, g_ref, pos_ref, invf_ref, win_ref, qn_ref, wuq_ref, kvn_ref, wukv_ref,
                     q_ref, k_ref, v_ref):
    xn = _rms(x_ref[...], g_ref[...]).astype(BF16)
    proj = jnp.dot(xn, win_ref[...], preferred_element_type=F32)
    qn = _rms(proj[:, :MLA_Q_RANK], qn_ref[...]).astype(BF16)
    kvn = _rms(proj[:, MLA_Q_RANK:MLA_Q_RANK + MLA_KV_RANK], kvn_ref[...]).astype(BF16)
    kr = proj[:, MLA_Q_RANK + MLA_KV_RANK:]

    ang = pos_ref[...].astype(F32) * invf_ref[...]
    cs = jnp.cos(ang)
    sn = jnp.sin(ang)
    lane = lax.broadcasted_iota(jnp.int32, ang.shape, 1)
    half = MLA_ROPE // 2
    c = jnp.where(lane < MLA_ROPE, cs, 0.0)
    s1 = jnp.where(lane < half, -sn, 0.0)
    s2 = jnp.where((lane >= half) & (lane < MLA_ROPE), sn, 0.0)

    k_rope = _rope_block(kr, c, s1, s2).astype(BF16)
    hp = MLA_HEAD_PAD
    for h in range(MLA_HEADS):
        qh = jnp.dot(qn, wuq_ref[:, h * hp:(h + 1) * hp], preferred_element_type=F32)
        q_ref[:, h * hp:h * hp + LANES] = qh[:, :LANES].astype(BF16)
        q_ref[:, h * hp + LANES:(h + 1) * hp] = _rope_block(qh[:, LANES:], c, s1, s2).astype(BF16)
        kvh = jnp.dot(kvn, wukv_ref[:, h * hp:(h + 1) * hp], preferred_element_type=F32)
        k_ref[:, h * hp:h * hp + LANES] = kvh[:, :LANES].astype(BF16)
        k_ref[:, h * hp + LANES:(h + 1) * hp] = k_rope
        v_ref[:, h * MLA_V:(h + 1) * MLA_V] = kvh[:, LANES:].astype(BF16)


def mla_proj(x, gain, pos, invf, w_in_p, q_norm, w_uq_p, kv_norm, w_ukv, *, tm=256):
    s, d = x.shape
    nq = MLA_HEADS * MLA_HEAD_PAD
    const = lambda i: (0, 0)
    return pl.pallas_call(
        _mla_proj_kernel,
        out_shape=(jax.ShapeDtypeStruct((s, nq), BF16), jax.ShapeDtypeStruct((s, nq), BF16),
                   jax.ShapeDtypeStruct((s, MLA_HEADS * MLA_V), BF16)),
        grid=(s // tm,),
        in_specs=[
            pl.BlockSpec((tm, d), lambda i: (i, 0)),
            pl.BlockSpec((1, d), const),
            pl.BlockSpec((tm, 1), lambda i: (i, 0)),
            pl.BlockSpec((1, LANES), const),
            pl.BlockSpec(w_in_p.shape, const),
            pl.BlockSpec((1, MLA_Q_RANK), const),
            pl.BlockSpec(w_uq_p.shape, const),
            pl.BlockSpec((1, MLA_KV_RANK), const),
            pl.BlockSpec(w_ukv.shape, const),
        ],
        out_specs=(pl.BlockSpec((tm, nq), lambda i: (i, 0)), pl.BlockSpec((tm, nq), lambda i: (i, 0)),
                   pl.BlockSpec((tm, MLA_HEADS * MLA_V), lambda i: (i, 0))),
        compiler_params=_params("arbitrary"),
        name="mla_proj",
    )(x, gain.reshape(1, d), pos, invf, w_in_p, q_norm.reshape(1, -1), w_uq_p, kv_norm.reshape(1, -1), w_ukv)


def _flash_kernel(q_ref, k_ref, v_ref, o_ref, m_ref, l_ref, acc_ref, *, tq, scale):
    qi = pl.program_id(1)
    q = q_ref[...]
    m_ref[...] = jnp.full(m_ref.shape, NEG_INF, F32)
    l_ref[...] = jnp.zeros(l_ref.shape, F32)
    acc_ref[...] = jnp.zeros(acc_ref.shape, F32)

    def step(start, masked):
        k = k_ref[pl.ds(start, tq), :]
        v = v_ref[pl.ds(start, tq), :]
        s = lax.dot_general(q, k, (((1,), (1,)), ((), ())), preferred_element_type=F32) * scale
        if masked:
            row = lax.broadcasted_iota(jnp.int32, s.shape, 0)
            col = lax.broadcasted_iota(jnp.int32, s.shape, 1)
            s = jnp.where(col <= row, s, NEG_INF)
        m_prev = m_ref[...]
        m_new = jnp.maximum(m_prev, jnp.max(s, axis=-1, keepdims=True))
        alpha = jnp.exp(m_prev - m_new)
        p = jnp.exp(s - m_new)
        l_ref[...] = alpha * l_ref[...] + jnp.sum(p, axis=-1, keepdims=True)
        acc_ref[...] = alpha * acc_ref[...] + jnp.dot(p.astype(BF16), v, preferred_element_type=F32)
        m_ref[...] = m_new

    def body(ki, carry):
        step(pl.multiple_of(ki * tq, tq), False)
        return carry

    lax.fori_loop(0, qi, body, 0)
    step(pl.multiple_of(qi * tq, tq), True)
    o_ref[...] = (acc_ref[...] / l_ref[...]).astype(o_ref.dtype)


def flash_attention(q, k, v, *, tq=512):
    s = q.shape[0]
    hp, dv = MLA_HEAD_PAD, MLA_V
    scale = (MLA_NOPE + MLA_ROPE) ** -0.5
    return pl.pallas_call(
        functools.partial(_flash_kernel, tq=tq, scale=scale),
        out_shape=jax.ShapeDtypeStruct((s, MLA_HEADS * dv), BF16),
        grid=(MLA_HEADS, s // tq),
        in_specs=[
            pl.BlockSpec((tq, hp), lambda h, i: (i, h)),
            pl.BlockSpec((s, hp), lambda h, i: (0, h)),
            pl.BlockSpec((s, dv), lambda h, i: (0, h)),
        ],
        out_specs=pl.BlockSpec((tq, dv), lambda h, i: (i, h)),
        scratch_shapes=[pltpu.VMEM((tq, 1), F32), pltpu.VMEM((tq, 1), F32), pltpu.VMEM((tq, dv), F32)],
        compiler_params=_params("arbitrary", "arbitrary"),
        name="mla_flash",
    )(q, k, v)


def mla_layer(x, positions, gain, w_in, q_norm, w_uq, kv_norm, w_ukv, w_o):
    s, d = x.shape
    half = MLA_ROPE // 2
    inv = ROPE_THETA ** (-jnp.arange(half, dtype=F32) / half)
    invf = jnp.concatenate([inv, inv, jnp.zeros((LANES - MLA_ROPE,), F32)]).reshape(1, LANES)
    w_in_p = jnp.pad(w_in, ((0, 0), (0, LANES - MLA_ROPE))).astype(BF16)
    dq = MLA_NOPE + MLA_ROPE
    w_uq_p = jnp.pad(w_uq.reshape(MLA_Q_RANK, MLA_HEADS, dq), ((0, 0), (0, 0), (0, MLA_HEAD_PAD - dq)))
    w_uq_p = w_uq_p.reshape(MLA_Q_RANK, MLA_HEADS * MLA_HEAD_PAD).astype(BF16)
    q, k, v = mla_proj(x, gain, positions.reshape(s, 1), invf, w_in_p, q_norm, w_uq_p, kv_norm,
                       w_ukv.astype(BF16))
    o = flash_attention(q, k, v)
    return fused_matmul(o, [(w_o.astype(BF16), d)], res=x, name="mla_out")


def _softplus(x):
    return jnp.maximum(x, 0.0) + jnp.log(1.0 + jnp.exp(-jnp.abs(x)))


def _gdn_proj_kernel(x_ref, g_ref, w_ref, cw_ref, wba_ref, wat_ref, alr_ref, dtr_ref, alc_ref, dtc_ref,
                     o_ref, bg_ref, gt_ref, xn_ref, stage_ref, carry_ref, *, tm, tn, n_qk, n_conv, qscale):
    i = pl.program_id(0)
    j = pl.program_id(1)
    nh = GDN_V_HEADS

    @pl.when(j == 0)
    def _():
        xn = _rms(x_ref[...], g_ref[...]).astype(BF16)
        xn_ref[...] = xn
        ba = jnp.dot(xn, wba_ref[...], preferred_element_type=F32)
        lane = lax.broadcasted_iota(jnp.int32, ba.shape, 1)
        gate = -jnp.exp(alr_ref[...]) * _softplus(ba + dtr_ref[...])
        bg_ref[...] = jnp.where(lane < nh, jax.nn.sigmoid(ba), gate)
        at = lax.dot_general(wat_ref[...], xn, (((1,), (1,)), ((), ())), preferred_element_type=F32)
        gt_ref[...] = -jnp.exp(alc_ref[...]) * _softplus(at + dtc_ref[...])

    acc = jnp.dot(xn_ref[...], w_ref[...], preferred_element_type=F32)

    @pl.when(j < n_conv)
    def _():
        prev = carry_ref[j]
        stage_ref[0:8, :] = jnp.where(i == 0, 0.0, prev)
        stage_ref[8:, :] = acc
        carry_ref[j] = acc[tm - 8:, :]
        cw = cw_ref[...]
        y = stage_ref[pl.ds(8 - (GDN_CONV - 1), tm), :] * cw[0:1, :]
        for t in range(1, GDN_CONV):
            y = y + stage_ref[pl.ds(8 - (GDN_CONV - 1) + t, tm), :] * cw[t:t + 1, :]
        stage_ref[8:, :] = _silu(y)

    @pl.when(j < n_qk)
    def _():
        sc = jnp.where(j < n_qk // 2, qscale, 1.0)
        for c in range(tn // LANES):
            yb = stage_ref[8:, c * LANES:(c + 1) * LANES]
            nrm = lax.rsqrt(jnp.sum(yb * yb, axis=-1, keepdims=True) + EPS) * sc
            o_ref[:, c * LANES:(c + 1) * LANES] = (yb * nrm).astype(o_ref.dtype)

    @pl.when((j >= n_qk) & (j < n_conv))
    def _():
        o_ref[...] = stage_ref[8:, :].astype(o_ref.dtype)

    @pl.when(j >= n_conv)
    def _():
        o_ref[...] = acc.astype(o_ref.dtype)


def gdn_proj(x, gain, w_qkvz, conv_w_p, w_ba, w_at, a_log, dt_bias, *, tm=512, tn=512):
    s, d = x.shape
    n = w_qkvz.shape[1]
    qk_dim = GDN_QK_HEADS * GDN_DK
    v_dim = GDN_V_HEADS * GDN_DV
    n_qk = 2 * qk_dim // tn
    n_conv = (2 * qk_dim + v_dim) // tn
    nh = GDN_V_HEADS
    pad = jnp.zeros((nh,), F32)
    alr = jnp.concatenate([pad, a_log, pad, pad]).reshape(1, LANES)
    dtr = jnp.concatenate([pad, dt_bias, pad, pad]).reshape(1, LANES)
    const = lambda i, j: (0, 0)
    kern = functools.partial(_gdn_proj_kernel, tm=tm, tn=tn, n_qk=n_qk, n_conv=n_conv, qscale=GDN_DK ** -0.5)
    return pl.pallas_call(
        kern,
        out_shape=(jax.ShapeDtypeStruct((s, n), BF16), jax.ShapeDtypeStruct((s, LANES), F32),
                   jax.ShapeDtypeStruct((nh, s), F32)),
        grid=(s // tm, n // tn),
        in_specs=[
            pl.BlockSpec((tm, d), lambda i, j: (i, 0)),
            pl.BlockSpec((1, d), const),
            pl.BlockSpec((d, tn), lambda i, j: (0, j)),
            pl.BlockSpec((GDN_CONV, tn), lambda i, j: (0, jnp.minimum(j, n_conv - 1))),
            pl.BlockSpec((d, LANES), const),
            pl.BlockSpec((nh, d), const),
            pl.BlockSpec((1, LANES), const),
            pl.BlockSpec((1, LANES), const),
            pl.BlockSpec((nh, 1), const),
            pl.BlockSpec((nh, 1), const),
        ],
        out_specs=(pl.BlockSpec((tm, tn), lambda i, j: (i, j)),
                   pl.BlockSpec((tm, LANES), lambda i, j: (i, 0)),
                   pl.BlockSpec((nh, tm), lambda i, j: (0, i))),
        scratch_shapes=[pltpu.VMEM((tm, d), BF16), pltpu.VMEM((tm + 8, tn), F32),
                        pltpu.VMEM((n_conv, 8, tn), F32)],
        compiler_params=_params("arbitrary", "arbitrary"),
        name="gdn_proj",
    )(x, gain.reshape(1, d), w_qkvz, conv_w_p, w_ba, w_at, alr, dtr, a_log.reshape(nh, 1), dt_bias.reshape(nh, 1))


def _gdn_delta_kernel(q_ref, k_ref, v_ref, z_ref, bg_ref, gt_ref, on_ref, o_ref, state_ref, mask_ref, tri_ref,
                      *, hq, c):
    hb = pl.program_id(0)
    t = pl.program_id(1)
    nlev = c.bit_length() - 1
    row = lax.broadcasted_iota(jnp.int32, (c, c), 0)
    col = lax.broadcasted_iota(jnp.int32, (c, c), 1)

    @pl.when((hb == 0) & (t == 0))
    def _():
        for l in range(nlev):
            m = ((row >> (l + 1)) == (col >> (l + 1))) & (((row >> l) & 1) == 1) & (((col >> l) & 1) == 0)
            mask_ref[l] = m.astype(F32)
        tri_ref[...] = (col <= row).astype(F32)

    @pl.when(t == 0)
    def _():
        state_ref[...] = jnp.zeros(state_ref.shape, F32)

    tri = tri_ref[...]
    incl = col <= row
    strict = col < row
    bg = bg_ref[...]
    lane = lax.broadcasted_iota(jnp.int32, bg.shape, 1)
    hi = lax.Precision.HIGHEST
    nt = (((1,), (1,)), ((), ()))

    for a in range(hq):
        qh = hb * hq + a
        q = q_ref[:, a * GDN_DK:(a + 1) * GDN_DK]
        k = k_ref[:, a * GDN_DK:(a + 1) * GDN_DK]
        kf = k.astype(F32)
        kk = lax.dot_general(k, k, nt, preferred_element_type=F32)
        qk = lax.dot_general(q, k, nt, preferred_element_type=F32)
        for b in range(2):
            vh = qh * 2 + b
            sl = a * 2 + b
            beta = jnp.sum(jnp.where(lane == vh, bg, 0.0), axis=-1, keepdims=True)
            gcol = jnp.sum(jnp.where(lane == vh + GDN_V_HEADS, bg, 0.0), axis=-1, keepdims=True)
            grow = gt_ref[pl.ds(vh, 1), :]
            gc_col = jnp.dot(tri, gcol, precision=hi, preferred_element_type=F32)
            gc_row = lax.dot_general(grow, tri, nt, precision=hi, preferred_element_type=F32)
            glast = gc_col[c - 1:c, :]
            diff = gc_col - gc_row
            dec = jnp.exp(jnp.where(incl, diff, NEG_INF))
            amat = jnp.where(strict, kk * dec, 0.0) * beta
            attn = (qk * dec).astype(BF16)
            r = -(amat * mask_ref[0])
            for l in range(1, nlev):
                e = amat * mask_ref[l]
                rb = r.astype(BF16)
                x = e + jnp.dot(e.astype(BF16), rb, preferred_element_type=F32)
                r = r - x - jnp.dot(rb, x.astype(BF16), preferred_element_type=F32)
            eg = jnp.exp(gc_col)
            v = v_ref[:, sl * GDN_DV:(sl + 1) * GDN_DV].astype(F32)
            rhs = jnp.concatenate([v * beta, kf * (beta * eg)], axis=1)
            uw = rhs + jnp.dot(r.astype(BF16), rhs.astype(BF16), preferred_element_type=F32)
            u = uw[:, :GDN_DV]
            w = uw[:, GDN_DV:]
            st = state_ref[sl]
            stb = st.astype(BF16)
            qd = (q.astype(F32) * eg).astype(BF16)
            ws = jnp.dot(jnp.concatenate([w.astype(BF16), qd], axis=0), stb, preferred_element_type=F32)
            v_new = u - ws[:c]
            vnb = v_new.astype(BF16)
            o = ws[c:] + jnp.dot(attn, vnb, preferred_element_type=F32)
            kdec = (kf * jnp.exp(glast - gc_col)).astype(BF16)
            state_ref[sl] = st * jnp.exp(glast) + lax.dot_general(
                kdec, vnb, (((0,), (0,)), ((), ())), preferred_element_type=F32)
            z = z_ref[:, sl * GDN_DV:(sl + 1) * GDN_DV].astype(F32)
            o_ref[:, sl * GDN_DV:(sl + 1) * GDN_DV] = (_rms(o, on_ref[...]) * _silu(z)).astype(o_ref.dtype)


def gdn_delta(qkvz, bg, gt, o_norm, *, hq=2, c=GDN_SUPER):
    s = qkvz.shape[0]
    qk_dim = GDN_QK_HEADS * GDN_DK
    v_dim = GDN_V_HEADS * GDN_DV
    wq = hq * GDN_DK
    wv = 2 * hq * GDN_DV
    nlev = c.bit_length() - 1
    return pl.pallas_call(
        functools.partial(_gdn_delta_kernel, hq=hq, c=c),
        out_shape=jax.ShapeDtypeStruct((s, v_dim), BF16),
        grid=(GDN_QK_HEADS // hq, s // c),
        in_specs=[
            pl.BlockSpec((c, wq), lambda h, t: (t, h)),
            pl.BlockSpec((c, wq), lambda h, t: (t, qk_dim // wq + h)),
            pl.BlockSpec((c, wv), lambda h, t: (t, 2 * qk_dim // wv + h)),
            pl.BlockSpec((c, wv), lambda h, t: (t, (2 * qk_dim + v_dim) // wv + h)),
            pl.BlockSpec((c, LANES), lambda h, t: (t, 0)),
            pl.BlockSpec((GDN_V_HEADS, c), lambda h, t: (0, t)),
            pl.BlockSpec((1, GDN_DV), lambda h, t: (0, 0)),
        ],
        out_specs=pl.BlockSpec((c, wv), lambda h, t: (t, h)),
        scratch_shapes=[pltpu.VMEM((2 * hq, GDN_DK, GDN_DV), F32), pltpu.VMEM((nlev, c, c), F32),
                        pltpu.VMEM((c, c), F32)],
        compiler_params=_params("arbitrary", "arbitrary"),
        name="gdn_delta",
    )(qkvz, qkvz, qkvz, qkvz, bg, gt, o_norm.reshape(1, GDN_DV))


def gdn_layer(x, gain, w_in, conv_w, a_log, dt_bias, o_norm, w_o):
    s, d = x.shape
    qk_dim = GDN_QK_HEADS * GDN_DK
    v_dim = GDN_V_HEADS * GDN_DV
    n_main = 2 * qk_dim + 2 * v_dim
    nh = GDN_V_HEADS
    w_qkvz = w_in[:, :n_main].astype(BF16)
    w_ba = jnp.pad(w_in[:, n_main:], ((0, 0), (0, LANES - 2 * nh))).astype(BF16)
    w_at = w_in[:, n_main + nh:].T.astype(BF16)
    qkvz, bg, gt = gdn_proj(x, gain, w_qkvz, conv_w, w_ba, w_at, a_log, dt_bias)
    o = gdn_delta(qkvz, bg, gt, o_norm)
    return fused_matmul(o, [(w_o.astype(BF16), d)], res=x, name="gdn_out")


def _s5_disc_kernel(lre_ref, lim_ref, ldt_ref, lre_e_ref, lim_e_ref, bre_ref, bim_ref,
                    lbre_ref, lbim_ref, bbre_ref, bbim_ref):
    dt = jnp.exp(ldt_ref[...])

    def zoh(lre, lim):
        mag = jnp.exp(lre * dt)
        ang = lim * dt
        lb_re = mag * jnp.cos(ang)
        lb_im = mag * jnp.sin(ang)
        den = lre * lre + lim * lim
        nr = lb_re - 1.0
        f_re = (nr * lre + lb_im * lim) / den
        f_im = (lb_im * lre - nr * lim) / den
        return lb_re, lb_im, f_re, f_im

    lb_re, lb_im, _, _ = zoh(lre_ref[...], lim_ref[...])
    lbre_ref[...] = lb_re
    lbim_ref[...] = lb_im
    _, _, f_re, f_im = zoh(lre_e_ref[...], lim_e_ref[...])
    bbre_ref[...] = f_re * bre_ref[...] - f_im * bim_ref[...]
    bbim_ref[...] = f_re * bim_ref[...] + f_im * bre_ref[...]


def _s5_scan_kernel(u_ref, bblk_ref, cblk_ref, are_ref, aim_ref, d_ref, o_ref, st_ref, bu_ref, x_ref,
                    *, lt, n_sq):
    p = pl.program_id(1)
    tb = pl.program_id(2)
    last = pl.num_programs(2) - 1
    nseg = S5_SEGMENTS
    half = bu_ref.shape[1] // 2

    @pl.when((p == 0) & (tb == 0))
    def _():
        st_ref[...] = jnp.zeros(st_ref.shape, F32)

    u = u_ref[...]
    bu_ref[...] = jnp.dot(u.astype(BF16), bblk_ref[0], preferred_element_type=F32)
    ar = jnp.broadcast_to(are_ref[0], (nseg, half))
    ai = jnp.broadcast_to(aim_ref[0], (nseg, half))

    def body(tau, carry):
        xr, xi = carry
        r0 = pl.multiple_of(tau * nseg, nseg)
        b = bu_ref[pl.ds(r0, nseg), :]
        nxr = ar * xr - ai * xi + b[:, :half]
        nxi = ar * xi + ai * xr + b[:, half:]
        x_ref[pl.ds(r0, nseg), :half] = nxr
        x_ref[pl.ds(r0, nseg), half:] = nxi
        return nxr, nxi

    st = st_ref[...]
    xr, xi = lax.fori_loop(0, lt, body, (st[:, :half], st[:, half:]), unroll=8)
    st_ref[:, :half] = xr
    st_ref[:, half:] = xi

    @pl.when((p == 0) & (tb == last))
    def _():
        pr, pi = are_ref[0], aim_ref[0]
        for _ in range(n_sq):
            pr, pi = pr * pr - pi * pi, 2.0 * pr * pi
        sr = jnp.zeros((1, half), F32)
        si = jnp.zeros((1, half), F32)
        rows_r, rows_i = [sr], [si]
        for r in range(nseg - 1):
            sr, si = pr * sr - pi * si + xr[r:r + 1, :], pr * si + pi * sr + xi[r:r + 1, :]
            rows_r.append(sr)
            rows_i.append(si)
        st_ref[:, :half] = jnp.concatenate(rows_r, axis=0)
        st_ref[:, half:] = jnp.concatenate(rows_i, axis=0)

    @pl.when(p == 1)
    def _():
        y = jnp.dot(x_ref[...].astype(BF16), cblk_ref[0], preferred_element_type=F32) + d_ref[...] * u
        o_ref[...] = jax.nn.gelu(y).astype(o_ref.dtype)


def s5_layer(x, gain, w_in, lam_re, lam_im, log_dt, b_re, b_im, c_re, c_im, d_skip, w_out):
    s, d = x.shape
    width = w_in.shape[1]
    g, p_st, gc = b_re.shape
    nseg = S5_SEGMENTS
    sseg = s // nseg
    assert sseg & (sseg - 1) == 0
    tm = min(512, sseg)
    nb = sseg // tm
    ncb = width // LANES
    gpb = LANES // gc
    half = gpb * p_st

    u = fused_matmul(x, [(w_in.astype(BF16), width)], gain=gain, tm=tm, tn=width,
                     out_map=lambda i, j: (i % nb, i // nb), out_shape=(sseg, nseg * width), name="s5_in")
    u2 = u.reshape(s, width)

    rep = lambda a: jnp.repeat(a, gc, axis=1)
    vm = pl.BlockSpec(memory_space=pltpu.VMEM)
    lb_re, lb_im, bb_re, bb_im = pl.pallas_call(
        _s5_disc_kernel,
        out_shape=(jax.ShapeDtypeStruct((g, p_st), F32), jax.ShapeDtypeStruct((g, p_st), F32),
                   jax.ShapeDtypeStruct((g, p_st * gc), F32), jax.ShapeDtypeStruct((g, p_st * gc), F32)),
        in_specs=[vm] * 7, out_specs=(vm, vm, vm, vm), name="s5_disc",
    )(lam_re, lam_im, log_dt.reshape(g, 1), rep(lam_re), rep(lam_im),
      b_re.reshape(g, p_st * gc), b_im.reshape(g, p_st * gc))

    eye = jnp.eye(gpb, dtype=F32)

    def in_blocks(bb):
        t = bb.reshape(ncb, gpb, p_st, gc)
        return jnp.einsum("bgpc,gh->bgchp", t, eye).reshape(ncb, gpb * gc, gpb * p_st)

    def out_blocks(cc):
        t = cc.reshape(ncb, gpb, gc, p_st)
        return jnp.einsum("bgcp,gh->bhpgc", t, eye).reshape(ncb, gpb * p_st, gpb * gc)

    bblk = jnp.concatenate([in_blocks(bb_re), in_blocks(bb_im)], axis=2).astype(BF16)
    cblk = jnp.concatenate([out_blocks(c_re), -out_blocks(c_im)], axis=1).astype(BF16)
    a_re = lb_re.reshape(ncb, 1, half)
    a_im = lb_im.reshape(ncb, 1, half)

    lt = min(64, sseg)
    rows = lt * nseg
    yg = pl.pallas_call(
        functools.partial(_s5_scan_kernel, lt=lt, n_sq=sseg.bit_length() - 1),
        out_shape=jax.ShapeDtypeStruct((s, width), BF16),
        grid=(ncb, 2, sseg // lt),
        in_specs=[
            pl.BlockSpec((rows, LANES), lambda cb, p, t: (t, cb)),
            pl.BlockSpec((1, LANES, 2 * half), lambda cb, p, t: (cb, 0, 0)),
            pl.BlockSpec((1, 2 * half, LANES), lambda cb, p, t: (cb, 0, 0)),
            pl.BlockSpec((1, 1, half), lambda cb, p, t: (cb, 0, 0)),
            pl.BlockSpec((1, 1, half), lambda cb, p, t: (cb, 0, 0)),
            pl.BlockSpec((1, LANES), lambda cb, p, t: (0, cb)),
        ],
        out_specs=pl.BlockSpec((rows, LANES), lambda cb, p, t: (t * p, cb)),
        scratch_shapes=[pltpu.VMEM((nseg, 2 * half), F32), pltpu.VMEM((rows, 2 * half), F32),
                        pltpu.VMEM((rows, 2 * half), F32)],
        compiler_params=_params("arbitrary", "arbitrary", "arbitrary"),
        name="s5_scan",
    )(u2, bblk, cblk, a_re, a_im, d_skip.reshape(1, width))

    w_out_b = w_out.astype(BF16)
    return fused_matmul(yg.reshape(sseg, nseg * width), [(w_out_b, d), (w_out_b, d)], act="glu", res=x,
                        w_col_offsets=[0, d], tm=tm, a_block=(tm, width), a_map=lambda i, j: (i % nb, i // nb),
                        name="s5_out")


def _conv_kernel(xg_ref, w_ref, b_ref, lg_ref, lb_ref, wo_ref, res_ref, o_ref, stage_ref, y_ref, a_ref,
                 *, tm, halo, tn):
    i = pl.program_id(0)
    ns = stage_ref.shape[0]
    d = ns * LANES
    for c in range(ns):
        prev = stage_ref[c, tm:tm + halo, :]
        stage_ref[c, 0:halo, :] = jnp.where(i == 0, 0.0, prev)
        stage_ref[c, halo:, :] = xg_ref[:, c * LANES:(c + 1) * LANES]

    def strip(c, carry):
        w = w_ref[c]
        acc = stage_ref[c, pl.ds(halo - (CONV_K - 1), tm), :] * w[0:1, :]
        for k in range(1, CONV_K):
            acc = acc + stage_ref[c, pl.ds(halo - (CONV_K - 1) + k, tm), :] * w[k:k + 1, :]
        y_ref[c] = acc + b_ref[c]
        return carry

    lax.fori_loop(0, ns, strip, 0)

    tot = y_ref[0]
    for c in range(1, ns):
        tot = tot + y_ref[c]
    mu = jnp.sum(tot, axis=-1, keepdims=True) * (1.0 / d)
    sq = jnp.square(y_ref[0] - mu)
    for c in range(1, ns):
        sq = sq + jnp.square(y_ref[c] - mu)
    rstd = lax.rsqrt(jnp.sum(sq, axis=-1, keepdims=True) * (1.0 / d) + EPS)
    for c in range(ns):
        yn = (y_ref[c] - mu) * rstd * lg_ref[c] + lb_ref[c]
        a_ref[:, c * LANES:(c + 1) * LANES] = _silu(yn).astype(BF16)
    a = a_ref[...]
    for jn in range(d // tn):
        sl = slice(jn * tn, (jn + 1) * tn)
        o_ref[:, sl] = res_ref[:, sl] + jnp.dot(a, wo_ref[:, sl], preferred_element_type=F32)


def conv_layer(x, gain, w_in, dw_w, dw_b, ln_g, ln_b, w_out, *, tm=256):
    s, d = x.shape
    ch = dw_w.shape[1]
    ns = ch // LANES
    halo = 32
    w_in_b = w_in.astype(BF16)
    xg = fused_matmul(x, [(w_in_b, ch), (w_in_b, ch)], gain=gain, act="glu", w_col_offsets=[0, ch],
                      name="conv_in")
    strips = lambda a: a.reshape(-1, ns, LANES).transpose(1, 0, 2)
    w_s = strips(jnp.pad(dw_w, ((0, halo - CONV_K), (0, 0))))
    const2 = lambda i: (0, 0)
    const3 = lambda i: (0, 0, 0)
    return pl.pallas_call(
        functools.partial(_conv_kernel, tm=tm, halo=halo, tn=512),
        out_shape=jax.ShapeDtypeStruct((s, d), F32),
        grid=(s // tm,),
        in_specs=[
            pl.BlockSpec((tm, ch), lambda i: (i, 0)),
            pl.BlockSpec((ns, halo, LANES), const3),
            pl.BlockSpec((ns, 1, LANES), const3),
            pl.BlockSpec((ns, 1, LANES), const3),
            pl.BlockSpec((ns, 1, LANES), const3),
            pl.BlockSpec((ch, d), const2),
            pl.BlockSpec((tm, d), lambda i: (i, 0)),
        ],
        out_specs=pl.BlockSpec((tm, d), lambda i: (i, 0)),
        scratch_shapes=[pltpu.VMEM((ns, tm + halo, LANES), F32), pltpu.VMEM((ns, tm, LANES), F32),
                        pltpu.VMEM((tm, ch), BF16)],
        compiler_params=_params("arbitrary"),
        name="conv_mod",
    )(xg, w_s, strips(dw_b.reshape(1, ch)), strips(ln_g.reshape(1, ch)), strips(ln_b.reshape(1, ch)),
      w_out.astype(BF16), x)


def kernel(x, positions, norm_mix, norm_ffn, final_norm, mla_w_in, mla_q_norm, mla_w_uq, mla_kv_norm,
           mla_w_ukv, mla_w_o, gdn_w_in, gdn_conv_w, gdn_a_log, gdn_dt_bias, gdn_o_norm, gdn_w_o, s5_w_in,
           s5_lam_re, s5_lam_im, s5_log_dt, s5_b_re, s5_b_im, s5_c_re, s5_c_im, s5_d, s5_w_out, cv_w_in,
           cv_dw_w, cv_dw_b, cv_ln_g, cv_ln_b, cv_w_out, ffn_w_gate, ffn_w_up, ffn_w_down):
    bsz, s, d = x.shape
    depth = norm_mix.shape[0]
    outs = []
    for b in range(bsz):
        h = x[b]
        pos = positions[b]
        for i in range(depth):
            m, j = i % 4, i // 4
            if m == 0:
                h = mla_layer(h, pos, norm_mix[i], mla_w_in[j], mla_q_norm[j], mla_w_uq[j], mla_kv_norm[j],
                              mla_w_ukv[j], mla_w_o[j])
            elif m == 1:
                h = gdn_layer(h, norm_mix[i], gdn_w_in[j], gdn_conv_w[j], gdn_a_log[j], gdn_dt_bias[j],
                              gdn_o_norm[j], gdn_w_o[j])
            elif m == 2:
                h = s5_layer(h, norm_mix[i], s5_w_in[j], s5_lam_re[j], s5_lam_im[j], s5_log_dt[j], s5_b_re[j],
                             s5_b_im[j], s5_c_re[j], s5_c_im[j], s5_d[j], s5_w_out[j])
            else:
                h = conv_layer(h, norm_mix[i], cv_w_in[j], cv_dw_w[j], cv_dw_b[j], cv_ln_g[j], cv_ln_b[j],
                               cv_w_out[j])
            fg = final_norm if i == depth - 1 else None
            h = ffn(h, norm_ffn[i], ffn_w_gate[i].astype(BF16), ffn_w_up[i].astype(BF16),
                    ffn_w_down[i].astype(BF16), fg)
        outs.append(h)
    return jnp.stack(outs)
```

```python
import functools
import math

import jax
import jax.numpy as jnp
from jax import lax
from jax.experimental import pallas as pl
from jax.experimental.pallas import tpu as pltpu

F32 = jnp.float32
BF16 = jnp.bfloat16
EPS = 1e-6
NEG_INF = -1e30

VMEM_LIMIT_BYTES = 56 * 1024 * 1024
LANES = 128

MLA_HEADS = 16
MLA_Q_RANK = 512
MLA_KV_RANK = 512
MLA_NOPE = 128
MLA_ROPE = 64
MLA_V = 128
MLA_HEAD_PAD = 256
ROPE_THETA = 10000.0

GDN_QK_HEADS = 16
GDN_V_HEADS = 32
GDN_DK = 128
GDN_DV = 128
GDN_CONV = 4
GDN_CHUNK = 64
GDN_SUPER = 256

S5_GROUP = 16
S5_STATE = 64
S5_SEGMENTS = 8

CONV_K = 31


def _params(*sem):
    return pltpu.CompilerParams(dimension_semantics=sem, vmem_limit_bytes=VMEM_LIMIT_BYTES)


def _rms(x, g):
    return x * lax.rsqrt(jnp.mean(x * x, axis=-1, keepdims=True) + EPS) * g


def _silu(x):
    return x * jax.nn.sigmoid(x)


def _mm_kernel(*refs, has_norm, n_w, act, has_res):
    it = iter(refs)
    a_ref = next(it)
    g_ref = next(it) if has_norm else None
    w_refs = [next(it) for _ in range(n_w)]
    res_ref = next(it) if has_res else None
    o_ref = next(it)
    xn_ref = next(it) if has_norm else None

    if has_norm:
        @pl.when(pl.program_id(1) == 0)
        def _():
            xn_ref[...] = _rms(a_ref[...], g_ref[...]).astype(BF16)

        a = xn_ref[...]
    else:
        a = a_ref[...]
    accs = [jnp.dot(a, w[...], preferred_element_type=F32) for w in w_refs]
    if act == "glu":
        y = accs[0] * jax.nn.sigmoid(accs[1])
    else:
        y = accs[0]
    if has_res:
        y = res_ref[...] + y
    o_ref[...] = y.astype(o_ref.dtype)


def fused_matmul(a, ws, *, gain=None, act="none", res=None, out_dtype=F32, tm=512, tn=512,
                 w_col_offsets=None, out_map=None, out_shape=None, a_block=None, a_map=None, name="mm"):
    k = ws[0][0].shape[0]
    m = a.size // k
    n = ws[0][1]
    n_w = len(ws)
    tm = min(tm, m)
    tn = min(tn, n)
    assert m % tm == 0 and n % tn == 0
    has_norm = gain is not None
    has_res = res is not None
    if w_col_offsets is None:
        w_col_offsets = [0] * n_w
    if a_block is None:
        a_block, a_map = (tm, k), (lambda i, j: (i, 0))
    in_specs = [pl.BlockSpec(a_block, a_map)]
    args = [a]
    if has_norm:
        in_specs.append(pl.BlockSpec((1, k), lambda i, j: (0, 0)))
        args.append(gain.reshape(1, k))
    for (w, _), off in zip(ws, w_col_offsets):
        assert off % tn == 0
        ob = off // tn
        in_specs.append(pl.BlockSpec((k, tn), lambda i, j, ob=ob: (0, j + ob)))
        args.append(w)
    if has_res:
        in_specs.append(pl.BlockSpec((tm, tn), lambda i, j: (i, j)))
        args.append(res)
    if out_map is None:
        out_map = lambda i, j: (i, j)
    if out_shape is None:
        out_shape = (m, n)
    scratch = [pltpu.VMEM((tm, k), BF16)] if has_norm else []
    return pl.pallas_call(
        functools.partial(_mm_kernel, has_norm=has_norm, n_w=n_w, act=act, has_res=has_res),
        out_shape=jax.ShapeDtypeStruct(out_shape, out_dtype),
        grid=(m // tm, n // tn),
        in_specs=in_specs,
        out_specs=pl.BlockSpec((tm, tn), out_map),
        scratch_shapes=scratch,
        compiler_params=_params("arbitrary", "arbitrary"),
        name=name,
    )(*args)


def _ffn_kernel(*refs, final_norm):
    if final_norm:
        x_ref, g_ref, wg_ref, wu_ref, wd_ref, fg_ref, o_ref, xn_ref = refs
    else:
        x_ref, g_ref, wg_ref, wu_ref, wd_ref, o_ref, xn_ref = refs
        fg_ref = None
    j = pl.program_id(1)

    @pl.when(j == 0)
    def _():
        x = x_ref[...]
        xn_ref[...] = _rms(x, g_ref[...]).astype(BF16)
        o_ref[...] = x

    xn = xn_ref[...]
    hg = jnp.dot(xn, wg_ref[...], preferred_element_type=F32)
    hu = jnp.dot(xn, wu_ref[...], preferred_element_type=F32)
    h = (_silu(hg) * hu).astype(BF16)
    o_ref[...] += jnp.dot(h, wd_ref[...], preferred_element_type=F32)

    if final_norm:
        @pl.when(j == pl.num_programs(1) - 1)
        def _():
            o_ref[...] = _rms(o_ref[...], fg_ref[...])


def ffn(x, gain, wg, wu, wd, final_gain=None, *, tm=512, th=512):
    m, d = x.shape
    hdim = wg.shape[1]
    assert m % tm == 0 and hdim % th == 0
    final_norm = final_gain is not None
    in_specs = [
        pl.BlockSpec((tm, d), lambda i, j: (i, 0)),
        pl.BlockSpec((1, d), lambda i, j: (0, 0)),
        pl.BlockSpec((d, th), lambda i, j: (0, j)),
        pl.BlockSpec((d, th), lambda i, j: (0, j)),
        pl.BlockSpec((th, d), lambda i, j: (j, 0)),
    ]
    args = [x, gain.reshape(1, d), wg, wu, wd]
    if final_norm:
        in_specs.append(pl.BlockSpec((1, d), lambda i, j: (0, 0)))
        args.append(final_gain.reshape(1, d))
    return pl.pallas_call(
        functools.partial(_ffn_kernel, final_norm=final_norm),
        out_shape=jax.ShapeDtypeStruct((m, d), F32),
        grid=(m // tm, hdim // th),
        in_specs=in_specs,
        out_specs=pl.BlockSpec((tm, d), lambda i, j: (i, 0)),
        scratch_shapes=[pltpu.VMEM((tm, d), BF16)],
        compiler_params=_params("arbitrary", "arbitrary"),
        name="ffn",
    )(*args)


def _rope_block(xb, c, s1, s2):
    return xb * c + pltpu.roll(xb, 96, 1) * s1 + pltpu.roll(xb, 32, 1) * s2


def _mla_proj_kernel(x_ref, g_ref, pos_ref, invf_ref, win_ref, qn_ref, wuq_ref, kvn_ref, wukv_ref,
                     q_ref, k_ref, v_ref):
    xn = _rms(x_ref[...], g_ref[...]).astype(BF16)
    proj = jnp.dot(xn, win_ref[...], preferred_element_type=F32)
    qn = _rms(proj[:, :MLA_Q_RANK], qn_ref[...]).astype(BF16)
    kvn = _rms(proj[:, MLA_Q_RANK:MLA_Q_RANK + MLA_KV_RANK], kvn_ref[...]).astype(BF16)
    kr = proj[:, MLA_Q_RANK + MLA_KV_RANK:]

    ang = pos_ref[...].astype(F32) * invf_ref[...]
    cs = jnp.cos(ang)
    sn = jnp.sin(ang)
    lane = lax.broadcasted_iota(jnp.int32, ang.shape, 1)
    half = MLA_ROPE // 2
    c = jnp.where(lane < MLA_ROPE, cs, 0.0)
    s1 = jnp.where(lane < half, -sn, 0.0)
    s2 = jnp.where((lane >= half) & (lane < MLA_ROPE), sn, 0.0)

    k_rope = _rope_block(kr, c, s1, s2).astype(BF16)
    hp = MLA_HEAD_PAD
    for h in range(MLA_HEADS):
        qh = jnp.dot(qn, wuq_ref[:, h * hp:(h + 1) * hp], preferred_element_type=F32)
        q_ref[:, h * hp:h * hp + LANES] = qh[:, :LANES].astype(BF16)
        q_ref[:, h * hp + LANES:(h + 1) * hp] = _rope_block(qh[:, LANES:], c, s1, s2).astype(BF16)
        kvh = jnp.dot(kvn, wukv_ref[:, h * hp:(h + 1) * hp], preferred_element_type=F32)
        k_ref[:, h * hp:h * hp + LANES] = kvh[:, :LANES].astype(BF16)
        k_ref[:, h * hp + LANES:(h + 1) * hp] = k_rope
        v_ref[:, h * MLA_V:(h + 1) * MLA_V] = kvh[:, LANES:].astype(BF16)


def mla_proj(x, gain, pos, invf, w_in_p, q_norm, w_uq_p, kv_norm, w_ukv, *, tm=256):
    s, d = x.shape
    nq = MLA_HEADS * MLA_HEAD_PAD
    const = lambda i: (0, 0)
    return pl.pallas_call(
        _mla_proj_kernel,
        out_shape=(jax.ShapeDtypeStruct((s, nq), BF16), jax.ShapeDtypeStruct((s, nq), BF16),
                   jax.ShapeDtypeStruct((s, MLA_HEADS * MLA_V), BF16)),
        grid=(s // tm,),
        in_specs=[
            pl.BlockSpec((tm, d), lambda i: (i, 0)),
            pl.BlockSpec((1, d), const),
            pl.BlockSpec((tm, 1), lambda i: (i, 0)),
            pl.BlockSpec((1, LANES), const),
            pl.BlockSpec(w_in_p.shape, const),
            pl.BlockSpec((1, MLA_Q_RANK), const),
            pl.BlockSpec(w_uq_p.shape, const),
            pl.BlockSpec((1, MLA_KV_RANK), const),
            pl.BlockSpec(w_ukv.shape, const),
        ],
        out_specs=(pl.BlockSpec((tm, nq), lambda i: (i, 0)), pl.BlockSpec((tm, nq), lambda i: (i, 0)),
                   pl.BlockSpec((tm, MLA_HEADS * MLA_V), lambda i: (i, 0))),
        compiler_params=_params("arbitrary"),
        name="mla_proj",
    )(x, gain.reshape(1, d), pos, invf, w_in_p, q_norm.reshape(1, -1), w_uq_p, kv_norm.reshape(1, -1), w_ukv)


def _flash_kernel(q_ref, k_ref, v_ref, o_ref, m_ref, l_ref, acc_ref, *, tq, tk, c2):
    qi = pl.program_id(1)
    q = q_ref[...]
    m_ref[...] = jnp.full(m_ref.shape, NEG_INF, F32)
    l_ref[...] = jnp.zeros(l_ref.shape, F32)
    acc_ref[...] = jnp.zeros(acc_ref.shape, F32)

    def step(start, masked):
        k = k_ref[pl.ds(start, tk), :]
        v = v_ref[pl.ds(start, tk), :]
        s = lax.dot_general(q, k, (((1,), (1,)), ((), ())), preferred_element_type=F32)
        if masked:
            row = qi * tq + lax.broadcasted_iota(jnp.int32, s.shape, 0)
            col = start + lax.broadcasted_iota(jnp.int32, s.shape, 1)
            s = jnp.where(col <= row, s, NEG_INF)
        m_prev = m_ref[...]
        m_new = jnp.maximum(m_prev, jnp.max(s, axis=-1, keepdims=True))
        alpha = jnp.exp2((m_prev - m_new) * c2)
        l_new = alpha * l_ref[...]
        ps = []
        for c in range(tk // LANES):
            pc = jnp.exp2((s[:, c * LANES:(c + 1) * LANES] - m_new) * c2)
            l_new = l_new + pc
            ps.append(pc.astype(BF16))
        p = jnp.concatenate(ps, axis=1)
        acc_ref[...] = alpha * acc_ref[...] + jnp.dot(p, v, preferred_element_type=F32)
        l_ref[...] = l_new
        m_ref[...] = m_new

    def body(ki, carry):
        step(pl.multiple_of(ki * tk, tk), False)
        return carry

    nd = tq // tk
    lax.fori_loop(0, qi * nd, body, 0)
    for dblk in range(nd):
        step(pl.multiple_of((qi * nd + dblk) * tk, tk), True)
    o_ref[...] = (acc_ref[...] / jnp.sum(l_ref[...], axis=-1, keepdims=True)).astype(o_ref.dtype)


def flash_attention(q, k, v, *, tq=1024, tk=512):
    s = q.shape[0]
    tq = min(tq, s)
    tk = min(tk, tq)
    hp, dv = MLA_HEAD_PAD, MLA_V
    c2 = (MLA_NOPE + MLA_ROPE) ** -0.5 * math.log2(math.e)
    return pl.pallas_call(
        functools.partial(_flash_kernel, tq=tq, tk=tk, c2=c2),
        out_shape=jax.ShapeDtypeStruct((s, MLA_HEADS * dv), BF16),
        grid=(MLA_HEADS, s // tq),
        in_specs=[
            pl.BlockSpec((tq, hp), lambda h, i: (i, h)),
            pl.BlockSpec((s, hp), lambda h, i: (0, h)),
            pl.BlockSpec((s, dv), lambda h, i: (0, h)),
        ],
        out_specs=pl.BlockSpec((tq, dv), lambda h, i: (i, h)),
        scratch_shapes=[pltpu.VMEM((tq, LANES), F32), pltpu.VMEM((tq, LANES), F32), pltpu.VMEM((tq, dv), F32)],
        compiler_params=_params("arbitrary", "arbitrary"),
        name="mla_flash",
    )(q, k, v)


def mla_layer(x, positions, gain, w_in, q_norm, w_uq, kv_norm, w_ukv, w_o):
    s, d = x.shape
    half = MLA_ROPE // 2
    inv = ROPE_THETA ** (-jnp.arange(half, dtype=F32) / half)
    invf = jnp.concatenate([inv, inv, jnp.zeros((LANES - MLA_ROPE,), F32)]).reshape(1, LANES)
    w_in_p = jnp.pad(w_in, ((0, 0), (0, LANES - MLA_ROPE))).astype(BF16)
    dq = MLA_NOPE + MLA_ROPE
    w_uq_p = jnp.pad(w_uq.reshape(MLA_Q_RANK, MLA_HEADS, dq), ((0, 0), (0, 0), (0, MLA_HEAD_PAD - dq)))
    w_uq_p = w_uq_p.reshape(MLA_Q_RANK, MLA_HEADS * MLA_HEAD_PAD).astype(BF16)
    q, k, v = mla_proj(x, gain, positions.reshape(s, 1), invf, w_in_p, q_norm, w_uq_p, kv_norm,
                       w_ukv.astype(BF16))
    o = flash_attention(q, k, v)
    return fused_matmul(o, [(w_o.astype(BF16), d)], res=x, name="mla_out")


def _softplus(x):
    return jnp.maximum(x, 0.0) + jnp.log(1.0 + jnp.exp(-jnp.abs(x)))


def _gdn_proj_kernel(x_ref, g_ref, w_ref, cw_ref, wba_ref, wat_ref, alr_ref, dtr_ref, alc_ref, dtc_ref,
                     o_ref, bg_ref, gt_ref, xn_ref, stage_ref, carry_ref, *, tm, tn, n_qk, n_conv, qscale):
    i = pl.program_id(0)
    j = pl.program_id(1)
    nh = GDN_V_HEADS

    @pl.when(j == 0)
    def _():
        xn = _rms(x_ref[...], g_ref[...]).astype(BF16)
        xn_ref[...] = xn
        ba = jnp.dot(xn, wba_ref[...], preferred_element_type=F32)
        lane = lax.broadcasted_iota(jnp.int32, ba.shape, 1)
        gate = -jnp.exp(alr_ref[...]) * _softplus(ba + dtr_ref[...])
        bg_ref[...] = jnp.where(lane < nh, jax.nn.sigmoid(ba), gate)
        at = lax.dot_general(wat_ref[...], xn, (((1,), (1,)), ((), ())), preferred_element_type=F32)
        gt_ref[...] = -jnp.exp(alc_ref[...]) * _softplus(at + dtc_ref[...])

    acc = jnp.dot(xn_ref[...], w_ref[...], preferred_element_type=F32)

    @pl.when(j < n_conv)
    def _():
        prev = carry_ref[j]
        stage_ref[0:8, :] = jnp.where(i == 0, 0.0, prev)
        stage_ref[8:, :] = acc
        carry_ref[j] = acc[tm - 8:, :]
        cw = cw_ref[...]
        y = stage_ref[pl.ds(8 - (GDN_CONV - 1), tm), :] * cw[0:1, :]
        for t in range(1, GDN_CONV):
            y = y + stage_ref[pl.ds(8 - (GDN_CONV - 1) + t, tm), :] * cw[t:t + 1, :]
        stage_ref[8:, :] = _silu(y)

    @pl.when(j < n_qk)
    def _():
        sc = jnp.where(j < n_qk // 2, qscale, 1.0)
        for c in range(tn // LANES):
            yb = stage_ref[8:, c * LANES:(c + 1) * LANES]
            nrm = lax.rsqrt(jnp.sum(yb * yb, axis=-1, keepdims=True) + EPS) * sc
            o_ref[:, c * LANES:(c + 1) * LANES] = (yb * nrm).astype(o_ref.dtype)

    @pl.when((j >= n_qk) & (j < n_conv))
    def _():
        o_ref[...] = stage_ref[8:, :].astype(o_ref.dtype)

    @pl.when(j >= n_conv)
    def _():
        o_ref[...] = acc.astype(o_ref.dtype)


def gdn_proj(x, gain, w_qkvz, conv_w_p, w_ba, w_at, a_log, dt_bias, *, tm=512, tn=512):
    s, d = x.shape
    n = w_qkvz.shape[1]
    qk_dim = GDN_QK_HEADS * GDN_DK
    v_dim = GDN_V_HEADS * GDN_DV
    n_qk = 2 * qk_dim // tn
    n_conv = (2 * qk_dim + v_dim) // tn
    nh = GDN_V_HEADS
    pad = jnp.zeros((nh,), F32)
    alr = jnp.concatenate([pad, a_log, pad, pad]).reshape(1, LANES)
    dtr = jnp.concatenate([pad, dt_bias, pad, pad]).reshape(1, LANES)
    const = lambda i, j: (0, 0)
    kern = functools.partial(_gdn_proj_kernel, tm=tm, tn=tn, n_qk=n_qk, n_conv=n_conv, qscale=GDN_DK ** -0.5)
    return pl.pallas_call(
        kern,
        out_shape=(jax.ShapeDtypeStruct((s, n), BF16), jax.ShapeDtypeStruct((s, LANES), F32),
                   jax.ShapeDtypeStruct((nh, s), F32)),
        grid=(s // tm, n // tn),
        in_specs=[
            pl.BlockSpec((tm, d), lambda i, j: (i, 0)),
            pl.BlockSpec((1, d), const),
            pl.BlockSpec((d, tn), lambda i, j: (0, j)),
            pl.BlockSpec((GDN_CONV, tn), lambda i, j: (0, jnp.minimum(j, n_conv - 1))),
            pl.BlockSpec((d, LANES), const),
            pl.BlockSpec((nh, d), const),
            pl.BlockSpec((1, LANES), const),
            pl.BlockSpec((1, LANES), const),
            pl.BlockSpec((nh, 1), const),
            pl.BlockSpec((nh, 1), const),
        ],
        out_specs=(pl.BlockSpec((tm, tn), lambda i, j: (i, j)),
                   pl.BlockSpec((tm, LANES), lambda i, j: (i, 0)),
                   pl.BlockSpec((nh, tm), lambda i, j: (0, i))),
        scratch_shapes=[pltpu.VMEM((tm, d), BF16), pltpu.VMEM((tm + 8, tn), F32),
                        pltpu.VMEM((n_conv, 8, tn), F32)],
        compiler_params=_params("arbitrary", "arbitrary"),
        name="gdn_proj",
    )(x, gain.reshape(1, d), w_qkvz, conv_w_p, w_ba, w_at, alr, dtr, a_log.reshape(nh, 1), dt_bias.reshape(nh, 1))


def _gdn_delta_kernel(q_ref, k_ref, v_ref, z_ref, bg_ref, gt_ref, on_ref, o_ref, state_ref, mask_ref, tri_ref,
                      gtc_ref, *, hq, c):
    hb = pl.program_id(0)
    t = pl.program_id(1)
    nlev = c.bit_length() - 1
    row = lax.broadcasted_iota(jnp.int32, (c, c), 0)
    col = lax.broadcasted_iota(jnp.int32, (c, c), 1)

    @pl.when((hb == 0) & (t == 0))
    def _():
        for l in range(nlev):
            m = ((row >> (l + 1)) == (col >> (l + 1))) & (((row >> l) & 1) == 1) & (((col >> l) & 1) == 0)
            mask_ref[l] = m.astype(F32)
        tri_ref[...] = (col <= row).astype(F32)

    @pl.when(t == 0)
    def _():
        state_ref[...] = jnp.zeros(state_ref.shape, F32)

    tri = tri_ref[...]
    incl = col <= row
    strict = col < row
    bg = bg_ref[...]
    lane = lax.broadcasted_iota(jnp.int32, bg.shape, 1)
    hi = lax.Precision.HIGHEST
    nt = (((1,), (1,)), ((), ()))

    gall = jnp.dot(tri, bg, precision=hi, preferred_element_type=F32)
    gtc_ref[...] = lax.dot_general(gt_ref[...], tri, nt, precision=hi, preferred_element_type=F32)

    nb = 2 * hq
    qs, ks, kfs, amats, attns, betas, gcs, glasts = [], [], [], [], [], [], [], []
    for a in range(hq):
        q = q_ref[:, a * GDN_DK:(a + 1) * GDN_DK]
        k = k_ref[:, a * GDN_DK:(a + 1) * GDN_DK]
        kk = lax.dot_general(k, k, nt, preferred_element_type=F32)
        qk = lax.dot_general(q, k, nt, preferred_element_type=F32)
        for b in range(2):
            vh = (hb * hq + a) * 2 + b
            beta = jnp.sum(jnp.where(lane == vh, bg, 0.0), axis=-1, keepdims=True)
            gc_col = jnp.sum(jnp.where(lane == vh + GDN_V_HEADS, gall, 0.0), axis=-1, keepdims=True)
            gc_row = gtc_ref[pl.ds(vh, 1), :]
            dec = jnp.exp(jnp.where(incl, gc_col - gc_row, NEG_INF))
            amats.append(jnp.where(strict, kk * dec, 0.0) * beta)
            attns.append((qk * dec).astype(BF16))
            qs.append(q)
            ks.append(k)
            betas.append(beta)
            gcs.append(gc_col)
            glasts.append(gc_col[c - 1:c, :])
    rs = [-(amats[i] * mask_ref[0]) for i in range(nb)]
    for l in range(1, nlev):
        es = [amats[i] * mask_ref[l] for i in range(nb)]
        rbs = [rs[i].astype(BF16) for i in range(nb)]
        xs = [es[i] + jnp.dot(es[i].astype(BF16), rbs[i], preferred_element_type=F32) for i in range(nb)]
        rs = [rs[i] - xs[i] - jnp.dot(rbs[i], xs[i].astype(BF16), preferred_element_type=F32) for i in range(nb)]
    egs = [jnp.exp(gcs[i]) for i in range(nb)]
    kfs = [ks[i].astype(F32) for i in range(nb)]
    rhss = [jnp.concatenate([v_ref[:, i * GDN_DV:(i + 1) * GDN_DV].astype(F32) * betas[i],
                             kfs[i] * (betas[i] * egs[i])], axis=1) for i in range(nb)]
    uws = [rhss[i] + jnp.dot(rs[i].astype(BF16), rhss[i].astype(BF16), preferred_element_type=F32)
           for i in range(nb)]
    sts = [state_ref[i] for i in range(nb)]
    wss = [jnp.dot(jnp.concatenate([uws[i][:, GDN_DV:].astype(BF16),
                                    (qs[i].astype(F32) * egs[i]).astype(BF16)], axis=0),
                   sts[i].astype(BF16), preferred_element_type=F32) for i in range(nb)]
    vnbs = [(uws[i][:, :GDN_DV] - wss[i][:c]).astype(BF16) for i in range(nb)]
    os_ = [wss[i][c:] + jnp.dot(attns[i], vnbs[i], preferred_element_type=F32) for i in range(nb)]
    for i in range(nb):
        kdec = (kfs[i] * jnp.exp(glasts[i] - gcs[i])).astype(BF16)
        state_ref[i] = sts[i] * jnp.exp(glasts[i]) + lax.dot_general(
            kdec, vnbs[i], (((0,), (0,)), ((), ())), preferred_element_type=F32)
        z = z_ref[:, i * GDN_DV:(i + 1) * GDN_DV].astype(F32)
        o_ref[:, i * GDN_DV:(i + 1) * GDN_DV] = (_rms(os_[i], on_ref[...]) * _silu(z)).astype(o_ref.dtype)


def gdn_delta(qkvz, bg, gt, o_norm, *, hq=2, c=GDN_SUPER):
    s = qkvz.shape[0]
    qk_dim = GDN_QK_HEADS * GDN_DK
    v_dim = GDN_V_HEADS * GDN_DV
    wq = hq * GDN_DK
    wv = 2 * hq * GDN_DV
    nlev = c.bit_length() - 1
    return pl.pallas_call(
        functools.partial(_gdn_delta_kernel, hq=hq, c=c),
        out_shape=jax.ShapeDtypeStruct((s, v_dim), BF16),
        grid=(GDN_QK_HEADS // hq, s // c),
        in_specs=[
            pl.BlockSpec((c, wq), lambda h, t: (t, h)),
            pl.BlockSpec((c, wq), lambda h, t: (t, qk_dim // wq + h)),
            pl.BlockSpec((c, wv), lambda h, t: (t, 2 * qk_dim // wv + h)),
            pl.BlockSpec((c, wv), lambda h, t: (t, (2 * qk_dim + v_dim) // wv + h)),
            pl.BlockSpec((c, LANES), lambda h, t: (t, 0)),
            pl.BlockSpec((GDN_V_HEADS, c), lambda h, t: (0, t)),
            pl.BlockSpec((1, GDN_DV), lambda h, t: (0, 0)),
        ],
        out_specs=pl.BlockSpec((c, wv), lambda h, t: (t, h)),
        scratch_shapes=[pltpu.VMEM((2 * hq, GDN_DK, GDN_DV), F32), pltpu.VMEM((nlev, c, c), F32),
                        pltpu.VMEM((c, c), F32), pltpu.VMEM((GDN_V_HEADS, c), F32)],
        compiler_params=_params("arbitrary", "arbitrary"),
        name="gdn_delta",
    )(qkvz, qkvz, qkvz, qkvz, bg, gt, o_norm.reshape(1, GDN_DV))


def gdn_layer(x, gain, w_in, conv_w, a_log, dt_bias, o_norm, w_o):
    s, d = x.shape
    qk_dim = GDN_QK_HEADS * GDN_DK
    v_dim = GDN_V_HEADS * GDN_DV
    n_main = 2 * qk_dim + 2 * v_dim
    nh = GDN_V_HEADS
    w_qkvz = w_in[:, :n_main].astype(BF16)
    w_ba = jnp.pad(w_in[:, n_main:], ((0, 0), (0, LANES - 2 * nh))).astype(BF16)
    w_at = w_in[:, n_main + nh:].T.astype(BF16)
    qkvz, bg, gt = gdn_proj(x, gain, w_qkvz, conv_w, w_ba, w_at, a_log, dt_bias)
    o = gdn_delta(qkvz, bg, gt, o_norm)
    return fused_matmul(o, [(w_o.astype(BF16), d)], res=x, name="gdn_out")


def _s5_disc_kernel(lre_ref, lim_ref, ldt_ref, lre_e_ref, lim_e_ref, bre_ref, bim_ref,
                    lbre_ref, lbim_ref, bbre_ref, bbim_ref):
    dt = jnp.exp(ldt_ref[...])

    def zoh(lre, lim):
        mag = jnp.exp(lre * dt)
        ang = lim * dt
        lb_re = mag * jnp.cos(ang)
        lb_im = mag * jnp.sin(ang)
        den = lre * lre + lim * lim
        nr = lb_re - 1.0
        f_re = (nr * lre + lb_im * lim) / den
        f_im = (lb_im * lre - nr * lim) / den
        return lb_re, lb_im, f_re, f_im

    lb_re, lb_im, _, _ = zoh(lre_ref[...], lim_ref[...])
    lbre_ref[...] = lb_re
    lbim_ref[...] = lb_im
    _, _, f_re, f_im = zoh(lre_e_ref[...], lim_e_ref[...])
    bbre_ref[...] = f_re * bre_ref[...] - f_im * bim_ref[...]
    bbim_ref[...] = f_re * bim_ref[...] + f_im * bre_ref[...]


def _s5_scan_kernel(u_ref, bblk_ref, cblk_ref, are_ref, aim_ref, d_ref, o_ref, st_ref, bu_ref, x_ref,
                    *, lt, n_sq):
    p = pl.program_id(1)
    tb = pl.program_id(2)
    last = pl.num_programs(2) - 1
    nseg = S5_SEGMENTS
    half = bu_ref.shape[1] // 2

    @pl.when((p == 0) & (tb == 0))
    def _():
        st_ref[...] = jnp.zeros(st_ref.shape, F32)

    u = u_ref[...]
    bu_ref[...] = jnp.dot(u.astype(BF16), bblk_ref[0], preferred_element_type=F32)
    ar = jnp.broadcast_to(are_ref[0], (nseg, half))
    ai = jnp.broadcast_to(aim_ref[0], (nseg, half))

    def body(tau, carry):
        xr, xi = carry
        r0 = pl.multiple_of(tau * nseg, nseg)
        b = bu_ref[pl.ds(r0, nseg), :]
        nxr = ar * xr - ai * xi + b[:, :half]
        nxi = ar * xi + ai * xr + b[:, half:]
        x_ref[pl.ds(r0, nseg), :half] = nxr
        x_ref[pl.ds(r0, nseg), half:] = nxi
        return nxr, nxi

    st = st_ref[...]
    xr, xi = lax.fori_loop(0, lt, body, (st[:, :half], st[:, half:]), unroll=8)
    st_ref[:, :half] = xr
    st_ref[:, half:] = xi

    @pl.when((p == 0) & (tb == last))
    def _():
        pr, pi = are_ref[0], aim_ref[0]
        for _ in range(n_sq):
            pr, pi = pr * pr - pi * pi, 2.0 * pr * pi
        sr = jnp.zeros((1, half), F32)
        si = jnp.zeros((1, half), F32)
        rows_r, rows_i = [sr], [si]
        for r in range(nseg - 1):
            sr, si = pr * sr - pi * si + xr[r:r + 1, :], pr * si + pi * sr + xi[r:r + 1, :]
            rows_r.append(sr)
            rows_i.append(si)
        st_ref[:, :half] = jnp.concatenate(rows_r, axis=0)
        st_ref[:, half:] = jnp.concatenate(rows_i, axis=0)

    @pl.when(p == 1)
    def _():
        y = jnp.dot(x_ref[...].astype(BF16), cblk_ref[0], preferred_element_type=F32) + d_ref[...] * u
        o_ref[...] = jax.nn.gelu(y).astype(o_ref.dtype)


def s5_layer(x, gain, w_in, lam_re, lam_im, log_dt, b_re, b_im, c_re, c_im, d_skip, w_out):
    s, d = x.shape
    width = w_in.shape[1]
    g, p_st, gc = b_re.shape
    nseg = S5_SEGMENTS
    sseg = s // nseg
    assert sseg & (sseg - 1) == 0
    tm = min(512, sseg)
    nb = sseg // tm
    ncb = width // LANES
    gpb = LANES // gc
    half = gpb * p_st

    u = fused_matmul(x, [(w_in.astype(BF16), width)], gain=gain, tm=tm, tn=width,
                     out_map=lambda i, j: (i % nb, i // nb), out_shape=(sseg, nseg * width), name="s5_in")
    u2 = u.reshape(s, width)

    rep = lambda a: jnp.repeat(a, gc, axis=1)
    vm = pl.BlockSpec(memory_space=pltpu.VMEM)
    lb_re, lb_im, bb_re, bb_im = pl.pallas_call(
        _s5_disc_kernel,
        out_shape=(jax.ShapeDtypeStruct((g, p_st), F32), jax.ShapeDtypeStruct((g, p_st), F32),
                   jax.ShapeDtypeStruct((g, p_st * gc), F32), jax.ShapeDtypeStruct((g, p_st * gc), F32)),
        in_specs=[vm] * 7, out_specs=(vm, vm, vm, vm), name="s5_disc",
    )(lam_re, lam_im, log_dt.reshape(g, 1), rep(lam_re), rep(lam_im),
      b_re.reshape(g, p_st * gc), b_im.reshape(g, p_st * gc))

    eye = jnp.eye(gpb, dtype=F32)

    def in_blocks(bb):
        t = bb.reshape(ncb, gpb, p_st, gc)
        return jnp.einsum("bgpc,gh->bgchp", t, eye).reshape(ncb, gpb * gc, gpb * p_st)

    def out_blocks(cc):
        t = cc.reshape(ncb, gpb, gc, p_st)
        return jnp.einsum("bgcp,gh->bhpgc", t, eye).reshape(ncb, gpb * p_st, gpb * gc)

    bblk = jnp.concatenate([in_blocks(bb_re), in_blocks(bb_im)], axis=2).astype(BF16)
    cblk = jnp.concatenate([out_blocks(c_re), -out_blocks(c_im)], axis=1).astype(BF16)
    a_re = lb_re.reshape(ncb, 1, half)
    a_im = lb_im.reshape(ncb, 1, half)

    lt = min(64, sseg)
    rows = lt * nseg
    yg = pl.pallas_call(
        functools.partial(_s5_scan_kernel, lt=lt, n_sq=sseg.bit_length() - 1),
        out_shape=jax.ShapeDtypeStruct((s, width), BF16),
        grid=(ncb, 2, sseg // lt),
        in_specs=[
            pl.BlockSpec((rows, LANES), lambda cb, p, t: (t, cb)),
            pl.BlockSpec((1, LANES, 2 * half), lambda cb, p, t: (cb, 0, 0)),
            pl.BlockSpec((1, 2 * half, LANES), lambda cb, p, t: (cb, 0, 0)),
            pl.BlockSpec((1, 1, half), lambda cb, p, t: (cb, 0, 0)),
            pl.BlockSpec((1, 1, half), lambda cb, p, t: (cb, 0, 0)),
            pl.BlockSpec((1, LANES), lambda cb, p, t: (0, cb)),
        ],
        out_specs=pl.BlockSpec((rows, LANES), lambda cb, p, t: (t * p, cb)),
        scratch_shapes=[pltpu.VMEM((nseg, 2 * half), F32), pltpu.VMEM((rows, 2 * half), F32),
                        pltpu.VMEM((rows, 2 * half), F32)],
        compiler_params=_params("arbitrary", "arbitrary", "arbitrary"),
        name="s5_scan",
    )(u2, bblk, cblk, a_re, a_im, d_skip.reshape(1, width))

    w_out_b = w_out.astype(BF16)
    return fused_matmul(yg.reshape(sseg, nseg * width), [(w_out_b, d), (w_out_b, d)], act="glu", res=x,
                        w_col_offsets=[0, d], tm=tm, a_block=(tm, width), a_map=lambda i, j: (i % nb, i // nb),
                        name="s5_out")


def _conv_kernel(xg_ref, w_ref, b_ref, lg_ref, lb_ref, wo_ref, res_ref, o_ref, stage_ref, y_ref, a_ref,
                 *, tm, halo, tn):
    i = pl.program_id(0)
    ns = stage_ref.shape[0]
    d = ns * LANES
    for c in range(ns):
        prev = stage_ref[c, tm:tm + halo, :]
        stage_ref[c, 0:halo, :] = jnp.where(i == 0, 0.0, prev)
        stage_ref[c, halo:, :] = xg_ref[:, c * LANES:(c + 1) * LANES]

    def strip(c, carry):
        w = w_ref[c]
        acc = stage_ref[c, pl.ds(halo - (CONV_K - 1), tm), :] * w[0:1, :]
        for k in range(1, CONV_K):
            acc = acc + stage_ref[c, pl.ds(halo - (CONV_K - 1) + k, tm), :] * w[k:k + 1, :]
        y_ref[c] = acc + b_ref[c]
        return carry

    lax.fori_loop(0, ns, strip, 0)

    tot = y_ref[0]
    for c in range(1, ns):
        tot = tot + y_ref[c]
    mu = jnp.sum(tot, axis=-1, keepdims=True) * (1.0 / d)
    sq = jnp.square(y_ref[0] - mu)
    for c in range(1, ns):
        sq = sq + jnp.square(y_ref[c] - mu)
    rstd = lax.rsqrt(jnp.sum(sq, axis=-1, keepdims=True) * (1.0 / d) + EPS)
    for c in range(ns):
        yn = (y_ref[c] - mu) * rstd * lg_ref[c] + lb_ref[c]
        a_ref[:, c * LANES:(c + 1) * LANES] = _silu(yn).astype(BF16)
    a = a_ref[...]
    for jn in range(d // tn):
        sl = slice(jn * tn, (jn + 1) * tn)
        o_ref[:, sl] = res_ref[:, sl] + jnp.dot(a, wo_ref[:, sl], preferred_element_type=F32)


def conv_layer(x, gain, w_in, dw_w, dw_b, ln_g, ln_b, w_out, *, tm=256):
    s, d = x.shape
    ch = dw_w.shape[1]
    ns = ch // LANES
    halo = 32
    w_in_b = w_in.astype(BF16)
    xg = fused_matmul(x, [(w_in_b, ch), (w_in_b, ch)], gain=gain, act="glu", w_col_offsets=[0, ch],
                      name="conv_in")
    strips = lambda a: a.reshape(-1, ns, LANES).transpose(1, 0, 2)
    w_s = strips(jnp.pad(dw_w, ((0, halo - CONV_K), (0, 0))))
    const2 = lambda i: (0, 0)
    const3 = lambda i: (0, 0, 0)
    return pl.pallas_call(
        functools.partial(_conv_kernel, tm=tm, halo=halo, tn=512),
        out_shape=jax.ShapeDtypeStruct((s, d), F32),
        grid=(s // tm,),
        in_specs=[
            pl.BlockSpec((tm, ch), lambda i: (i, 0)),
            pl.BlockSpec((ns, halo, LANES), const3),
            pl.BlockSpec((ns, 1, LANES), const3),
            pl.BlockSpec((ns, 1, LANES), const3),
            pl.BlockSpec((ns, 1, LANES), const3),
            pl.BlockSpec((ch, d), const2),
            pl.BlockSpec((tm, d), lambda i: (i, 0)),
        ],
        out_specs=pl.BlockSpec((tm, d), lambda i: (i, 0)),
        scratch_shapes=[pltpu.VMEM((ns, tm + halo, LANES), F32), pltpu.VMEM((ns, tm, LANES), F32),
                        pltpu.VMEM((tm, ch), BF16)],
        compiler_params=_params("arbitrary"),
        name="conv_mod",
    )(xg, w_s, strips(dw_b.reshape(1, ch)), strips(ln_g.reshape(1, ch)), strips(ln_b.reshape(1, ch)),
      w_out.astype(BF16), x)


def kernel(x, positions, norm_mix, norm_ffn, final_norm, mla_w_in, mla_q_norm, mla_w_uq, mla_kv_norm,
           mla_w_ukv, mla_w_o, gdn_w_in, gdn_conv_w, gdn_a_log, gdn_dt_bias, gdn_o_norm, gdn_w_o, s5_w_in,
           s5_lam_re, s5_lam_im, s5_log_dt, s5_b_re, s5_b_im, s5_c_re, s5_c_im, s5_d, s5_w_out, cv_w_in,
           cv_dw_w, cv_dw_b, cv_ln_g, cv_ln_b, cv_w_out, ffn_w_gate, ffn_w_up, ffn_w_down):
    bsz, s, d = x.shape
    depth = norm_mix.shape[0]
    outs = []
    for b in range(bsz):
        h = x[b]
        pos = positions[b]
        for i in range(depth):
            m, j = i % 4, i // 4
            if m == 0:
                h = mla_layer(h, pos, norm_mix[i], mla_w_in[j], mla_q_norm[j], mla_w_uq[j], mla_kv_norm[j],
                              mla_w_ukv[j], mla_w_o[j])
            elif m == 1:
                h = gdn_layer(h, norm_mix[i], gdn_w_in[j], gdn_conv_w[j], gdn_a_log[j], gdn_dt_bias[j],
                              gdn_o_norm[j], gdn_w_o[j])
            elif m == 2:
                h = s5_layer(h, norm_mix[i], s5_w_in[j], s5_lam_re[j], s5_lam_im[j], s5_log_dt[j], s5_b_re[j],
                             s5_b_im[j], s5_c_re[j], s5_c_im[j], s5_d[j], s5_w_out[j])
            else:
                h = conv_layer(h, norm_mix[i], cv_w_in[j], cv_dw_w[j], cv_dw_b[j], cv_ln_g[j], cv_ln_b[j],
                               cv_w_out[j])
            fg = final_norm if i == depth - 1 else None
            h = ffn(h, norm_ffn[i], ffn_w_gate[i].astype(BF16), ffn_w_up[i].astype(BF16),
                    ffn_w_down[i].astype(BF16), fg)
        outs.append(h)
    return jnp.stack(outs)
```

```python
import functools
import math

import jax
import jax.numpy as jnp
from jax import lax
from jax.experimental import pallas as pl
from jax.experimental.pallas import tpu as pltpu

F32 = jnp.float32
BF16 = jnp.bfloat16
EPS = 1e-6
NEG_INF = -1e30

VMEM_LIMIT_BYTES = 56 * 1024 * 1024
LANES = 128

MLA_HEADS = 16
MLA_Q_RANK = 512
MLA_KV_RANK = 512
MLA_NOPE = 128
MLA_ROPE = 64
MLA_V = 128
MLA_HEAD_PAD = 256
ROPE_THETA = 10000.0

GDN_QK_HEADS = 16
GDN_V_HEADS = 32
GDN_DK = 128
GDN_DV = 128
GDN_CONV = 4
GDN_CHUNK = 64
GDN_SUPER = 256

S5_GROUP = 16
S5_STATE = 64
S5_SEGMENTS = 8

CONV_K = 31


def _params(*sem):
    return pltpu.CompilerParams(dimension_semantics=sem, vmem_limit_bytes=VMEM_LIMIT_BYTES)


def _rms(x, g):
    return x * lax.rsqrt(jnp.mean(x * x, axis=-1, keepdims=True) + EPS) * g


def _silu(x):
    return x * jax.nn.sigmoid(x)


def _mm_kernel(*refs, has_norm, n_w, act, has_res):
    it = iter(refs)
    a_ref = next(it)
    g_ref = next(it) if has_norm else None
    w_refs = [next(it) for _ in range(n_w)]
    res_ref = next(it) if has_res else None
    o_ref = next(it)
    xn_ref = next(it) if has_norm else None

    if has_norm:
        @pl.when(pl.program_id(1) == 0)
        def _():
            xn_ref[...] = _rms(a_ref[...], g_ref[...]).astype(BF16)

        a = xn_ref[...]
    else:
        a = a_ref[...]
    accs = [jnp.dot(a, w[...], preferred_element_type=F32) for w in w_refs]
    if act == "glu":
        y = accs[0] * jax.nn.sigmoid(accs[1])
    else:
        y = accs[0]
    if has_res:
        y = res_ref[...] + y
    o_ref[...] = y.astype(o_ref.dtype)


def fused_matmul(a, ws, *, gain=None, act="none", res=None, out_dtype=F32, tm=512, tn=512,
                 w_col_offsets=None, out_map=None, out_shape=None, a_block=None, a_map=None, name="mm"):
    k = ws[0][0].shape[0]
    m = a.size // k
    n = ws[0][1]
    n_w = len(ws)
    tm = min(tm, m)
    tn = min(tn, n)
    assert m % tm == 0 and n % tn == 0
    has_norm = gain is not None
    has_res = res is not None
    if w_col_offsets is None:
        w_col_offsets = [0] * n_w
    if a_block is None:
        a_block, a_map = (tm, k), (lambda i, j: (i, 0))
    in_specs = [pl.BlockSpec(a_block, a_map)]
    args = [a]
    if has_norm:
        in_specs.append(pl.BlockSpec((1, k), lambda i, j: (0, 0)))
        args.append(gain.reshape(1, k))
    for (w, _), off in zip(ws, w_col_offsets):
        assert off % tn == 0
        ob = off // tn
        in_specs.append(pl.BlockSpec((k, tn), lambda i, j, ob=ob: (0, j + ob)))
        args.append(w)
    if has_res:
        in_specs.append(pl.BlockSpec((tm, tn), lambda i, j: (i, j)))
        args.append(res)
    if out_map is None:
        out_map = lambda i, j: (i, j)
    if out_shape is None:
        out_shape = (m, n)
    scratch = [pltpu.VMEM((tm, k), BF16)] if has_norm else []
    return pl.pallas_call(
        functools.partial(_mm_kernel, has_norm=has_norm, n_w=n_w, act=act, has_res=has_res),
        out_shape=jax.ShapeDtypeStruct(out_shape, out_dtype),
        grid=(m // tm, n // tn),
        in_specs=in_specs,
        out_specs=pl.BlockSpec((tm, tn), out_map),
        scratch_shapes=scratch,
        compiler_params=_params("arbitrary", "arbitrary"),
        name=name,
    )(*args)


def _mm_resident_kernel(*refs, has_norm, w_offs, act, has_res, n, tn):
    it = iter(refs)
    a_ref = next(it)
    g_ref = next(it) if has_norm else None
    w_ref = next(it)
    res_ref = next(it) if has_res else None
    o_ref = next(it)
    a = _rms(a_ref[...], g_ref[...]).astype(BF16) if has_norm else a_ref[...]
    for jn in range(n // tn):
        sl = slice(jn * tn, (jn + 1) * tn)
        accs = [jnp.dot(a, w_ref[:, off + jn * tn:off + (jn + 1) * tn], preferred_element_type=F32)
                for off in w_offs]
        y = accs[0] * jax.nn.sigmoid(accs[1]) if act == "glu" else accs[0]
        if has_res:
            y = res_ref[:, sl] + y
        o_ref[:, sl] = y.astype(o_ref.dtype)


def resident_matmul(a, w, n, *, w_offs=(0,), gain=None, act="none", res=None, out_dtype=F32, tm=512, tn=512,
                    a_block=None, a_map=None, out_map=None, out_shape=None, name="mm"):
    k = w.shape[0]
    m = a.size // k
    tm = min(tm, m)
    tn = min(tn, n)
    assert m % tm == 0 and n % tn == 0
    has_norm = gain is not None
    has_res = res is not None
    row = lambda i: (i, 0)
    const = lambda i: (0, 0)
    in_specs = [pl.BlockSpec(a_block or (tm, k), a_map or row)]
    args = [a]
    if has_norm:
        in_specs.append(pl.BlockSpec((1, k), const))
        args.append(gain.reshape(1, k))
    in_specs.append(pl.BlockSpec(w.shape, const, pipeline_mode=pl.Buffered(1)))
    args.append(w)
    if has_res:
        in_specs.append(pl.BlockSpec((tm, n), row))
        args.append(res)
    return pl.pallas_call(
        functools.partial(_mm_resident_kernel, has_norm=has_norm, w_offs=tuple(w_offs), act=act,
                          has_res=has_res, n=n, tn=tn),
        out_shape=jax.ShapeDtypeStruct(out_shape or (m, n), out_dtype),
        grid=(m // tm,),
        in_specs=in_specs,
        out_specs=pl.BlockSpec((tm, n), out_map or row),
        compiler_params=_params("arbitrary"),
        name=name,
    )(*args)


def _ffn_kernel(*refs, final_norm):
    if final_norm:
        x_ref, g_ref, wg_ref, wu_ref, wd_ref, fg_ref, o_ref, xn_ref = refs
    else:
        x_ref, g_ref, wg_ref, wu_ref, wd_ref, o_ref, xn_ref = refs
        fg_ref = None
    j = pl.program_id(1)

    @pl.when(j == 0)
    def _():
        x = x_ref[...]
        xn_ref[...] = _rms(x, g_ref[...]).astype(BF16)
        o_ref[...] = x

    xn = xn_ref[...]
    hg = jnp.dot(xn, wg_ref[...], preferred_element_type=F32)
    hu = jnp.dot(xn, wu_ref[...], preferred_element_type=F32)
    h = (_silu(hg) * hu).astype(BF16)
    o_ref[...] += jnp.dot(h, wd_ref[...], preferred_element_type=F32)

    if final_norm:
        @pl.when(j == pl.num_programs(1) - 1)
        def _():
            o_ref[...] = _rms(o_ref[...], fg_ref[...])


def ffn(x, gain, wg, wu, wd, final_gain=None, *, tm=1024, th=512):
    m, d = x.shape
    hdim = wg.shape[1]
    assert m % tm == 0 and hdim % th == 0
    final_norm = final_gain is not None
    in_specs = [
        pl.BlockSpec((tm, d), lambda i, j: (i, 0)),
        pl.BlockSpec((1, d), lambda i, j: (0, 0)),
        pl.BlockSpec((d, th), lambda i, j: (0, j)),
        pl.BlockSpec((d, th), lambda i, j: (0, j)),
        pl.BlockSpec((th, d), lambda i, j: (j, 0)),
    ]
    args = [x, gain.reshape(1, d), wg, wu, wd]
    if final_norm:
        in_specs.append(pl.BlockSpec((1, d), lambda i, j: (0, 0)))
        args.append(final_gain.reshape(1, d))
    return pl.pallas_call(
        functools.partial(_ffn_kernel, final_norm=final_norm),
        out_shape=jax.ShapeDtypeStruct((m, d), F32),
        grid=(m // tm, hdim // th),
        in_specs=in_specs,
        out_specs=pl.BlockSpec((tm, d), lambda i, j: (i, 0)),
        scratch_shapes=[pltpu.VMEM((tm, d), BF16)],
        compiler_params=_params("arbitrary", "arbitrary"),
        name="ffn",
    )(*args)


def _rope_block(xb, c, s1, s2):
    return xb * c + pltpu.roll(xb, 96, 1) * s1 + pltpu.roll(xb, 32, 1) * s2


def _mla_proj_kernel(x_ref, g_ref, pos_ref, invf_ref, win_ref, qn_ref, wuq_ref, kvn_ref, wukv_ref,
                     q_ref, k_ref, v_ref):
    xn = _rms(x_ref[...], g_ref[...]).astype(BF16)
    proj = jnp.dot(xn, win_ref[...], preferred_element_type=F32)
    qn = _rms(proj[:, :MLA_Q_RANK], qn_ref[...]).astype(BF16)
    kvn = _rms(proj[:, MLA_Q_RANK:MLA_Q_RANK + MLA_KV_RANK], kvn_ref[...]).astype(BF16)
    kr = proj[:, MLA_Q_RANK + MLA_KV_RANK:]

    ang = pos_ref[...].astype(F32) * invf_ref[...]
    cs = jnp.cos(ang)
    sn = jnp.sin(ang)
    lane = lax.broadcasted_iota(jnp.int32, ang.shape, 1)
    half = MLA_ROPE // 2
    c = jnp.where(lane < MLA_ROPE, cs, 0.0)
    s1 = jnp.where(lane < half, -sn, 0.0)
    s2 = jnp.where((lane >= half) & (lane < MLA_ROPE), sn, 0.0)

    k_rope = _rope_block(kr, c, s1, s2).astype(BF16)
    hp = MLA_HEAD_PAD
    for h in range(MLA_HEADS):
        qh = jnp.dot(qn, wuq_ref[:, h * hp:(h + 1) * hp], preferred_element_type=F32)
        q_ref[:, h * hp:h * hp + LANES] = qh[:, :LANES].astype(BF16)
        q_ref[:, h * hp + LANES:(h + 1) * hp] = _rope_block(qh[:, LANES:], c, s1, s2).astype(BF16)
        kvh = jnp.dot(kvn, wukv_ref[:, h * hp:(h + 1) * hp], preferred_element_type=F32)
        k_ref[:, h * hp:h * hp + LANES] = kvh[:, :LANES].astype(BF16)
        k_ref[:, h * hp + LANES:(h + 1) * hp] = k_rope
        v_ref[:, h * MLA_V:(h + 1) * MLA_V] = kvh[:, LANES:].astype(BF16)


def mla_proj(x, gain, pos, invf, w_in_p, q_norm, w_uq_p, kv_norm, w_ukv, *, tm=256):
    s, d = x.shape
    nq = MLA_HEADS * MLA_HEAD_PAD
    const = lambda i: (0, 0)
    return pl.pallas_call(
        _mla_proj_kernel,
        out_shape=(jax.ShapeDtypeStruct((s, nq), BF16), jax.ShapeDtypeStruct((s, nq), BF16),
                   jax.ShapeDtypeStruct((s, MLA_HEADS * MLA_V), BF16)),
        grid=(s // tm,),
        in_specs=[
            pl.BlockSpec((tm, d), lambda i: (i, 0)),
            pl.BlockSpec((1, d), const),
            pl.BlockSpec((tm, 1), lambda i: (i, 0)),
            pl.BlockSpec((1, LANES), const),
            pl.BlockSpec(w_in_p.shape, const),
            pl.BlockSpec((1, MLA_Q_RANK), const),
            pl.BlockSpec(w_uq_p.shape, const),
            pl.BlockSpec((1, MLA_KV_RANK), const),
            pl.BlockSpec(w_ukv.shape, const),
        ],
        out_specs=(pl.BlockSpec((tm, nq), lambda i: (i, 0)), pl.BlockSpec((tm, nq), lambda i: (i, 0)),
                   pl.BlockSpec((tm, MLA_HEADS * MLA_V), lambda i: (i, 0))),
        compiler_params=_params("arbitrary"),
        name="mla_proj",
    )(x, gain.reshape(1, d), pos, invf, w_in_p, q_norm.reshape(1, -1), w_uq_p, kv_norm.reshape(1, -1), w_ukv)


def _flash_kernel(q_ref, k_ref, v_ref, o_ref, m_ref, l_ref, acc_ref, *, tq, tk, c2):
    qi = pl.program_id(1)
    q = q_ref[...]
    m_ref[...] = jnp.full(m_ref.shape, NEG_INF, F32)
    l_ref[...] = jnp.zeros(l_ref.shape, F32)
    acc_ref[...] = jnp.zeros(acc_ref.shape, F32)

    def step(start, masked):
        k = k_ref[pl.ds(start, tk), :]
        v = v_ref[pl.ds(start, tk), :]
        s = lax.dot_general(q, k, (((1,), (1,)), ((), ())), preferred_element_type=F32)
        if masked:
            row = qi * tq + lax.broadcasted_iota(jnp.int32, s.shape, 0)
            col = start + lax.broadcasted_iota(jnp.int32, s.shape, 1)
            s = jnp.where(col <= row, s, NEG_INF)
        m_prev = m_ref[...]
        m_new = jnp.maximum(m_prev, jnp.max(s, axis=-1, keepdims=True))
        alpha = jnp.exp2((m_prev - m_new) * c2)
        l_new = alpha * l_ref[...]
        ps = []
        for c in range(tk // LANES):
            pc = jnp.exp2((s[:, c * LANES:(c + 1) * LANES] - m_new) * c2)
            l_new = l_new + pc
            ps.append(pc.astype(BF16))
        p = jnp.concatenate(ps, axis=1)
        acc_ref[...] = alpha * acc_ref[...] + jnp.dot(p, v, preferred_element_type=F32)
        l_ref[...] = l_new
        m_ref[...] = m_new

    nd = tq // tk

    def body(kp, carry):
        for u in range(nd):
            step(pl.multiple_of((kp * nd + u) * tk, tk), False)
        return carry

    lax.fori_loop(0, qi, body, 0)
    for dblk in range(nd):
        step(pl.multiple_of((qi * nd + dblk) * tk, tk), True)
    o_ref[...] = (acc_ref[...] / jnp.sum(l_ref[...], axis=-1, keepdims=True)).astype(o_ref.dtype)


def flash_attention(q, k, v, *, tq=1024, tk=512):
    s = q.shape[0]
    tq = min(tq, s)
    tk = min(tk, tq)
    hp, dv = MLA_HEAD_PAD, MLA_V
    c2 = (MLA_NOPE + MLA_ROPE) ** -0.5 * math.log2(math.e)
    return pl.pallas_call(
        functools.partial(_flash_kernel, tq=tq, tk=tk, c2=c2),
        out_shape=jax.ShapeDtypeStruct((s, MLA_HEADS * dv), BF16),
        grid=(MLA_HEADS, s // tq),
        in_specs=[
            pl.BlockSpec((tq, hp), lambda h, i: (i, h)),
            pl.BlockSpec((s, hp), lambda h, i: (0, h)),
            pl.BlockSpec((s, dv), lambda h, i: (0, h)),
        ],
        out_specs=pl.BlockSpec((tq, dv), lambda h, i: (i, h)),
        scratch_shapes=[pltpu.VMEM((tq, LANES), F32), pltpu.VMEM((tq, LANES), F32), pltpu.VMEM((tq, dv), F32)],
        compiler_params=_params("arbitrary", "arbitrary"),
        name="mla_flash",
    )(q, k, v)


def mla_layer(x, positions, gain, w_in, q_norm, w_uq, kv_norm, w_ukv, w_o):
    s, d = x.shape
    half = MLA_ROPE // 2
    inv = ROPE_THETA ** (-jnp.arange(half, dtype=F32) / half)
    invf = jnp.concatenate([inv, inv, jnp.zeros((LANES - MLA_ROPE,), F32)]).reshape(1, LANES)
    w_in_p = jnp.pad(w_in, ((0, 0), (0, LANES - MLA_ROPE))).astype(BF16)
    dq = MLA_NOPE + MLA_ROPE
    w_uq_p = jnp.pad(w_uq.reshape(MLA_Q_RANK, MLA_HEADS, dq), ((0, 0), (0, 0), (0, MLA_HEAD_PAD - dq)))
    w_uq_p = w_uq_p.reshape(MLA_Q_RANK, MLA_HEADS * MLA_HEAD_PAD).astype(BF16)
    q, k, v = mla_proj(x, gain, positions.reshape(s, 1), invf, w_in_p, q_norm, w_uq_p, kv_norm,
                       w_ukv.astype(BF16))
    o = flash_attention(q, k, v)
    return resident_matmul(o, w_o.astype(BF16), d, res=x, name="mla_out")


def _softplus(x):
    return jnp.maximum(x, 0.0) + jnp.log(1.0 + jnp.exp(-jnp.abs(x)))


def _gdn_proj_kernel(x_ref, g_ref, w_ref, cw_ref, wba_ref, wat_ref, alr_ref, dtr_ref, alc_ref, dtc_ref,
                     o_ref, bg_ref, gt_ref, xn_ref, stage_ref, carry_ref, *, tm, tn, sub, n_qk, n_conv, qscale):
    i = pl.program_id(0)
    j = pl.program_id(1)
    nh = GDN_V_HEADS

    @pl.when(j == 0)
    def _():
        xn = _rms(x_ref[...], g_ref[...]).astype(BF16)
        xn_ref[...] = xn
        ba = jnp.dot(xn, wba_ref[...], preferred_element_type=F32)
        lane = lax.broadcasted_iota(jnp.int32, ba.shape, 1)
        gate = -jnp.exp(alr_ref[...]) * _softplus(ba + dtr_ref[...])
        bg_ref[...] = jnp.where(lane < nh, jax.nn.sigmoid(ba), gate)
        at = lax.dot_general(wat_ref[...], xn, (((1,), (1,)), ((), ())), preferred_element_type=F32)
        gt_ref[...] = -jnp.exp(alc_ref[...]) * _softplus(at + dtc_ref[...])

    nsub = tn // sub

    def proj(c):
        return jnp.dot(xn_ref[...], w_ref[:, c * sub:(c + 1) * sub], preferred_element_type=F32)

    def conv_silu(c):
        acc = proj(c)
        outs = []
        for hh in range(sub // LANES):
            k = c * (sub // LANES) + hh
            ks = slice(k * LANES, (k + 1) * LANES)
            a = acc[:, hh * LANES:(hh + 1) * LANES]
            stage_ref[k, 0:8, :] = jnp.where(i == 0, 0.0, carry_ref[j, :, ks])
            stage_ref[k, 8:, :] = a
            carry_ref[j, :, ks] = a[tm - 8:, :]
            cw = cw_ref[:, ks]
            y = stage_ref[k, pl.ds(8 - (GDN_CONV - 1), tm), :] * cw[0:1, :]
            for t in range(1, GDN_CONV):
                y = y + stage_ref[k, pl.ds(8 - (GDN_CONV - 1) + t, tm), :] * cw[t:t + 1, :]
            outs.append((ks, _silu(y)))
        return outs

    @pl.when(j < n_qk)
    def _():
        sc = jnp.where(j < n_qk // 2, qscale, 1.0)
        for c in range(nsub):
            for ks, yb in conv_silu(c):
                nrm = lax.rsqrt(jnp.sum(yb * yb, axis=-1, keepdims=True) + EPS) * sc
                o_ref[:, ks] = (yb * nrm).astype(o_ref.dtype)

    @pl.when((j >= n_qk) & (j < n_conv))
    def _():
        for c in range(nsub):
            for ks, yb in conv_silu(c):
                o_ref[:, ks] = yb.astype(o_ref.dtype)

    @pl.when(j >= n_conv)
    def _():
        for c in range(nsub):
            o_ref[:, c * sub:(c + 1) * sub] = proj(c).astype(o_ref.dtype)


def gdn_proj(x, gain, w_qkvz, conv_w_p, w_ba, w_at, a_log, dt_bias, *, tm=512, tn=1024, sub=256):
    s, d = x.shape
    qk_dim = GDN_QK_HEADS * GDN_DK
    v_dim = GDN_V_HEADS * GDN_DV
    n = 2 * qk_dim + 2 * v_dim
    assert n % tn == 0 and (2 * qk_dim) % tn == 0 and v_dim % tn == 0 and tn % sub == 0
    n_qk = 2 * qk_dim // tn
    n_conv = (2 * qk_dim + v_dim) // tn
    nh = GDN_V_HEADS
    pad = jnp.zeros((nh,), F32)
    alr = jnp.concatenate([pad, a_log, pad, pad]).reshape(1, LANES)
    dtr = jnp.concatenate([pad, dt_bias, pad, pad]).reshape(1, LANES)
    const = lambda i, j: (0, 0)
    kern = functools.partial(_gdn_proj_kernel, tm=tm, tn=tn, sub=sub, n_qk=n_qk, n_conv=n_conv,
                             qscale=GDN_DK ** -0.5)
    return pl.pallas_call(
        kern,
        out_shape=(jax.ShapeDtypeStruct((s, n), BF16), jax.ShapeDtypeStruct((s, LANES), F32),
                   jax.ShapeDtypeStruct((nh, s), F32)),
        grid=(s // tm, n // tn),
        in_specs=[
            pl.BlockSpec((tm, d), lambda i, j: (i, 0)),
            pl.BlockSpec((1, d), const),
            pl.BlockSpec((d, tn), lambda i, j: (0, j)),
            pl.BlockSpec((GDN_CONV, tn), lambda i, j: (0, jnp.minimum(j, n_conv - 1))),
            pl.BlockSpec((d, LANES), const),
            pl.BlockSpec((nh, d), const),
            pl.BlockSpec((1, LANES), const),
            pl.BlockSpec((1, LANES), const),
            pl.BlockSpec((nh, 1), const),
            pl.BlockSpec((nh, 1), const),
        ],
        out_specs=(pl.BlockSpec((tm, tn), lambda i, j: (i, j)),
                   pl.BlockSpec((tm, LANES), lambda i, j: (i, 0)),
                   pl.BlockSpec((nh, tm), lambda i, j: (0, i))),
        scratch_shapes=[pltpu.VMEM((tm, d), BF16), pltpu.VMEM((tn // LANES, tm + 8, LANES), F32),
                        pltpu.VMEM((n_conv, 8, tn), F32)],
        compiler_params=_params("arbitrary", "arbitrary"),
        name="gdn_proj",
    )(x, gain.reshape(1, d), w_qkvz, conv_w_p, w_ba, w_at, alr, dtr, a_log.reshape(nh, 1), dt_bias.reshape(nh, 1))


def _gdn_delta_kernel(q_ref, k_ref, v_ref, z_ref, bg_ref, gt_ref, on_ref, o_ref, state_ref, mask_ref, tri_ref,
                      gtc_ref, *, hq, c):
    hb = pl.program_id(0)
    t = pl.program_id(1)
    nlev = c.bit_length() - 1
    row = lax.broadcasted_iota(jnp.int32, (c, c), 0)
    col = lax.broadcasted_iota(jnp.int32, (c, c), 1)

    @pl.when((hb == 0) & (t == 0))
    def _():
        for l in range(nlev):
            m = ((row >> (l + 1)) == (col >> (l + 1))) & (((row >> l) & 1) == 1) & (((col >> l) & 1) == 0)
            mask_ref[l] = m.astype(F32)
        tri_ref[...] = (col <= row).astype(F32)

    @pl.when(t == 0)
    def _():
        state_ref[...] = jnp.zeros(state_ref.shape, F32)

    tri = tri_ref[...]
    incl = col <= row
    strict = col < row
    bg = bg_ref[...]
    lane = lax.broadcasted_iota(jnp.int32, bg.shape, 1)
    hi = lax.Precision.HIGHEST
    nt = (((1,), (1,)), ((), ()))

    gall = jnp.dot(tri, bg, precision=hi, preferred_element_type=F32)
    gtc_ref[...] = lax.dot_general(gt_ref[...], tri, nt, precision=hi, preferred_element_type=F32)

    nb = 2 * hq
    qs, ks, kfs, amats, attns, betas, gcs, glasts = [], [], [], [], [], [], [], []
    for a in range(hq):
        q = q_ref[:, a * GDN_DK:(a + 1) * GDN_DK]
        k = k_ref[:, a * GDN_DK:(a + 1) * GDN_DK]
        kk = lax.dot_general(k, k, nt, preferred_element_type=F32)
        qk = lax.dot_general(q, k, nt, preferred_element_type=F32)
        for b in range(2):
            vh = (hb * hq + a) * 2 + b
            beta = jnp.sum(jnp.where(lane == vh, bg, 0.0), axis=-1, keepdims=True)
            gc_col = jnp.sum(jnp.where(lane == vh + GDN_V_HEADS, gall, 0.0), axis=-1, keepdims=True)
            gc_row = gtc_ref[pl.ds(vh, 1), :]
            dec = jnp.exp(jnp.where(incl, gc_col - gc_row, NEG_INF))
            amats.append(jnp.where(strict, kk * dec, 0.0) * beta)
            attns.append((qk * dec).astype(BF16))
            qs.append(q)
            ks.append(k)
            betas.append(beta)
            gcs.append(gc_col)
            glasts.append(gc_col[c - 1:c, :])
    rs = [-(amats[i] * mask_ref[0]) for i in range(nb)]
    for l in range(1, nlev):
        es = [amats[i] * mask_ref[l] for i in range(nb)]
        rbs = [rs[i].astype(BF16) for i in range(nb)]
        xs = [es[i] + jnp.dot(es[i].astype(BF16), rbs[i], preferred_element_type=F32) for i in range(nb)]
        rs = [rs[i] - xs[i] - jnp.dot(rbs[i], xs[i].astype(BF16), preferred_element_type=F32) for i in range(nb)]
    egs = [jnp.exp(gcs[i]) for i in range(nb)]
    kfs = [ks[i].astype(F32) for i in range(nb)]
    rhss = [jnp.concatenate([v_ref[:, i * GDN_DV:(i + 1) * GDN_DV].astype(F32) * betas[i],
                             kfs[i] * (betas[i] * egs[i])], axis=1) for i in range(nb)]
    uws = [rhss[i] + jnp.dot(rs[i].astype(BF16), rhss[i].astype(BF16), preferred_element_type=F32)
           for i in range(nb)]
    sts = [state_ref[i] for i in range(nb)]
    wss = [jnp.dot(jnp.concatenate([uws[i][:, GDN_DV:].astype(BF16),
                                    (qs[i].astype(F32) * egs[i]).astype(BF16)], axis=0),
                   sts[i].astype(BF16), preferred_element_type=F32) for i in range(nb)]
    vnbs = [(uws[i][:, :GDN_DV] - wss[i][:c]).astype(BF16) for i in range(nb)]
    os_ = [wss[i][c:] + jnp.dot(attns[i], vnbs[i], preferred_element_type=F32) for i in range(nb)]
    for i in range(nb):
        kdec = (kfs[i] * jnp.exp(glasts[i] - gcs[i])).astype(BF16)
        state_ref[i] = sts[i] * jnp.exp(glasts[i]) + lax.dot_general(
            kdec, vnbs[i], (((0,), (0,)), ((), ())), preferred_element_type=F32)
        z = z_ref[:, i * GDN_DV:(i + 1) * GDN_DV].astype(F32)
        o_ref[:, i * GDN_DV:(i + 1) * GDN_DV] = (_rms(os_[i], on_ref[...]) * _silu(z)).astype(o_ref.dtype)


def gdn_delta(qkvz, bg, gt, o_norm, *, hq=2, c=GDN_SUPER):
    s = qkvz.shape[0]
    qk_dim = GDN_QK_HEADS * GDN_DK
    v_dim = GDN_V_HEADS * GDN_DV
    wq = hq * GDN_DK
    wv = 2 * hq * GDN_DV
    nlev = c.bit_length() - 1
    return pl.pallas_call(
        functools.partial(_gdn_delta_kernel, hq=hq, c=c),
        out_shape=jax.ShapeDtypeStruct((s, v_dim), BF16),
        grid=(GDN_QK_HEADS // hq, s // c),
        in_specs=[
            pl.BlockSpec((c, wq), lambda h, t: (t, h)),
            pl.BlockSpec((c, wq), lambda h, t: (t, qk_dim // wq + h)),
            pl.BlockSpec((c, wv), lambda h, t: (t, 2 * qk_dim // wv + h)),
            pl.BlockSpec((c, wv), lambda h, t: (t, (2 * qk_dim + v_dim) // wv + h)),
            pl.BlockSpec((c, LANES), lambda h, t: (t, 0)),
            pl.BlockSpec((GDN_V_HEADS, c), lambda h, t: (0, t)),
            pl.BlockSpec((1, GDN_DV), lambda h, t: (0, 0)),
        ],
        out_specs=pl.BlockSpec((c, wv), lambda h, t: (t, h)),
        scratch_shapes=[pltpu.VMEM((2 * hq, GDN_DK, GDN_DV), F32), pltpu.VMEM((nlev, c, c), F32),
                        pltpu.VMEM((c, c), F32), pltpu.VMEM((GDN_V_HEADS, c), F32)],
        compiler_params=_params("arbitrary", "arbitrary"),
        name="gdn_delta",
    )(qkvz, qkvz, qkvz, qkvz, bg, gt, o_norm.reshape(1, GDN_DV))


def gdn_layer(x, gain, w_in, conv_w, a_log, dt_bias, o_norm, w_o):
    s, d = x.shape
    qk_dim = GDN_QK_HEADS * GDN_DK
    v_dim = GDN_V_HEADS * GDN_DV
    n_main = 2 * qk_dim + 2 * v_dim
    nh = GDN_V_HEADS
    w_b = w_in.astype(BF16)
    w_ba = jnp.pad(w_b[:, n_main:], ((0, 0), (0, LANES - 2 * nh)))
    w_at = w_b[:, n_main + nh:].T
    qkvz, bg, gt = gdn_proj(x, gain, w_b, conv_w, w_ba, w_at, a_log, dt_bias)
    o = gdn_delta(qkvz, bg, gt, o_norm)
    return resident_matmul(o, w_o.astype(BF16), d, res=x, name="gdn_out")


def _s5_disc_kernel(lre_ref, lim_ref, ldt_ref, lre_e_ref, lim_e_ref, bre_ref, bim_ref,
                    lbre_ref, lbim_ref, bbre_ref, bbim_ref):
    dt = jnp.exp(ldt_ref[...])

    def zoh(lre, lim):
        mag = jnp.exp(lre * dt)
        ang = lim * dt
        lb_re = mag * jnp.cos(ang)
        lb_im = mag * jnp.sin(ang)
        den = lre * lre + lim * lim
        nr = lb_re - 1.0
        f_re = (nr * lre + lb_im * lim) / den
        f_im = (lb_im * lre - nr * lim) / den
        return lb_re, lb_im, f_re, f_im

    lb_re, lb_im, _, _ = zoh(lre_ref[...], lim_ref[...])
    lbre_ref[...] = lb_re
    lbim_ref[...] = lb_im
    _, _, f_re, f_im = zoh(lre_e_ref[...], lim_e_ref[...])
    bbre_ref[...] = f_re * bre_ref[...] - f_im * bim_ref[...]
    bbim_ref[...] = f_re * bim_ref[...] + f_im * bre_ref[...]


def _s5_scan_kernel(u_ref, bblk_ref, cblk_ref, are_ref, aim_ref, d_ref, o_ref, st_ref, bu_ref, x_ref,
                    *, lt, n_sq):
    p = pl.program_id(1)
    tb = pl.program_id(2)
    last = pl.num_programs(2) - 1
    nseg = S5_SEGMENTS
    half = bu_ref.shape[1] // 2

    @pl.when((p == 0) & (tb == 0))
    def _():
        st_ref[...] = jnp.zeros(st_ref.shape, F32)

    u = u_ref[...]
    bu_ref[...] = jnp.dot(u.astype(BF16), bblk_ref[0], preferred_element_type=F32)
    ar = jnp.broadcast_to(are_ref[0], (nseg, half))
    ai = jnp.broadcast_to(aim_ref[0], (nseg, half))

    def body(tau, carry):
        xr, xi = carry
        r0 = pl.multiple_of(tau * nseg, nseg)
        b = bu_ref[pl.ds(r0, nseg), :]
        nxr = ar * xr - ai * xi + b[:, :half]
        nxi = ar * xi + ai * xr + b[:, half:]
        x_ref[pl.ds(r0, nseg), :half] = nxr
        x_ref[pl.ds(r0, nseg), half:] = nxi
        return nxr, nxi

    st = st_ref[...]
    xr, xi = lax.fori_loop(0, lt, body, (st[:, :half], st[:, half:]), unroll=8)
    st_ref[:, :half] = xr
    st_ref[:, half:] = xi

    @pl.when((p == 0) & (tb == last))
    def _():
        pr, pi = are_ref[0], aim_ref[0]
        for _ in range(n_sq):
            pr, pi = pr * pr - pi * pi, 2.0 * pr * pi
        sr = jnp.zeros((1, half), F32)
        si = jnp.zeros((1, half), F32)
        rows_r, rows_i = [sr], [si]
        for r in range(nseg - 1):
            sr, si = pr * sr - pi * si + xr[r:r + 1, :], pr * si + pi * sr + xi[r:r + 1, :]
            rows_r.append(sr)
            rows_i.append(si)
        st_ref[:, :half] = jnp.concatenate(rows_r, axis=0)
        st_ref[:, half:] = jnp.concatenate(rows_i, axis=0)

    @pl.when(p == 1)
    def _():
        y = jnp.dot(x_ref[...].astype(BF16), cblk_ref[0], preferred_element_type=F32) + d_ref[...] * u
        o_ref[...] = jax.nn.gelu(y).astype(o_ref.dtype)


def s5_layer(x, gain, w_in, lam_re, lam_im, log_dt, b_re, b_im, c_re, c_im, d_skip, w_out):
    s, d = x.shape
    width = w_in.shape[1]
    g, p_st, gc = b_re.shape
    nseg = S5_SEGMENTS
    sseg = s // nseg
    assert sseg & (sseg - 1) == 0
    tm = min(512, sseg)
    nb = sseg // tm
    ncb = width // LANES
    gpb = LANES // gc
    half = gpb * p_st

    u = resident_matmul(x, w_in.astype(BF16), width, gain=gain, tm=tm,
                        out_map=lambda i: (i % nb, i // nb), out_shape=(sseg, nseg * width), name="s5_in")
    u2 = u.reshape(s, width)

    rep = lambda a: jnp.repeat(a, gc, axis=1)
    vm = pl.BlockSpec(memory_space=pltpu.VMEM)
    lb_re, lb_im, bb_re, bb_im = pl.pallas_call(
        _s5_disc_kernel,
        out_shape=(jax.ShapeDtypeStruct((g, p_st), F32), jax.ShapeDtypeStruct((g, p_st), F32),
                   jax.ShapeDtypeStruct((g, p_st * gc), F32), jax.ShapeDtypeStruct((g, p_st * gc), F32)),
        in_specs=[vm] * 7, out_specs=(vm, vm, vm, vm), name="s5_disc",
    )(lam_re, lam_im, log_dt.reshape(g, 1), rep(lam_re), rep(lam_im),
      b_re.reshape(g, p_st * gc), b_im.reshape(g, p_st * gc))

    eye = jnp.eye(gpb, dtype=F32)

    def in_blocks(bb):
        t = bb.reshape(ncb, gpb, p_st, gc)
        return jnp.einsum("bgpc,gh->bgchp", t, eye).reshape(ncb, gpb * gc, gpb * p_st)

    def out_blocks(cc):
        t = cc.reshape(ncb, gpb, gc, p_st)
        return jnp.einsum("bgcp,gh->bhpgc", t, eye).reshape(ncb, gpb * p_st, gpb * gc)

    bblk = jnp.concatenate([in_blocks(bb_re), in_blocks(bb_im)], axis=2).astype(BF16)
    cblk = jnp.concatenate([out_blocks(c_re), -out_blocks(c_im)], axis=1).astype(BF16)
    a_re = lb_re.reshape(ncb, 1, half)
    a_im = lb_im.reshape(ncb, 1, half)

    lt = min(64, sseg)
    rows = lt * nseg
    yg = pl.pallas_call(
        functools.partial(_s5_scan_kernel, lt=lt, n_sq=sseg.bit_length() - 1),
        out_shape=jax.ShapeDtypeStruct((s, width), BF16),
        grid=(ncb, 2, sseg // lt),
        in_specs=[
            pl.BlockSpec((rows, LANES), lambda cb, p, t: (t, cb)),
            pl.BlockSpec((1, LANES, 2 * half), lambda cb, p, t: (cb, 0, 0)),
            pl.BlockSpec((1, 2 * half, LANES), lambda cb, p, t: (cb, 0, 0)),
            pl.BlockSpec((1, 1, half), lambda cb, p, t: (cb, 0, 0)),
            pl.BlockSpec((1, 1, half), lambda cb, p, t: (cb, 0, 0)),
            pl.BlockSpec((1, LANES), lambda cb, p, t: (0, cb)),
        ],
        out_specs=pl.BlockSpec((rows, LANES), lambda cb, p, t: (t * p, cb)),
        scratch_shapes=[pltpu.VMEM((nseg, 2 * half), F32), pltpu.VMEM((rows, 2 * half), F32),
                        pltpu.VMEM((rows, 2 * half), F32)],
        compiler_params=_params("arbitrary", "arbitrary", "arbitrary"),
        name="s5_scan",
    )(u2, bblk, cblk, a_re, a_im, d_skip.reshape(1, width))

    return resident_matmul(yg.reshape(sseg, nseg * width), w_out.astype(BF16), d, w_offs=(0, d), act="glu",
                           res=x, tm=tm, a_block=(tm, width), a_map=lambda i: (i % nb, i // nb), name="s5_out")


def _conv_kernel(xg_ref, w_ref, b_ref, lg_ref, lb_ref, wo_ref, res_ref, o_ref, stage_ref, y_ref, a_ref,
                 *, tm, halo, tn):
    i = pl.program_id(0)
    ns = stage_ref.shape[0]
    d = ns * LANES
    for c in range(ns):
        prev = stage_ref[c, tm:tm + halo, :]
        stage_ref[c, 0:halo, :] = jnp.where(i == 0, 0.0, prev)
        stage_ref[c, halo:, :] = xg_ref[:, c * LANES:(c + 1) * LANES]

    def strip(c, carry):
        w = w_ref[c]
        acc = stage_ref[c, pl.ds(halo - (CONV_K - 1), tm), :] * w[0:1, :]
        for k in range(1, CONV_K):
            acc = acc + stage_ref[c, pl.ds(halo - (CONV_K - 1) + k, tm), :] * w[k:k + 1, :]
        y_ref[c] = acc + b_ref[c]
        return carry

    lax.fori_loop(0, ns, strip, 0)

    tot = y_ref[0]
    for c in range(1, ns):
        tot = tot + y_ref[c]
    mu = jnp.sum(tot, axis=-1, keepdims=True) * (1.0 / d)
    sq = jnp.square(y_ref[0] - mu)
    for c in range(1, ns):
        sq = sq + jnp.square(y_ref[c] - mu)
    rstd = lax.rsqrt(jnp.sum(sq, axis=-1, keepdims=True) * (1.0 / d) + EPS)
    for c in range(ns):
        yn = (y_ref[c] - mu) * rstd * lg_ref[c] + lb_ref[c]
        a_ref[:, c * LANES:(c + 1) * LANES] = _silu(yn).astype(BF16)
    a = a_ref[...]
    for jn in range(d // tn):
        sl = slice(jn * tn, (jn + 1) * tn)
        o_ref[:, sl] = res_ref[:, sl] + jnp.dot(a, wo_ref[:, sl], preferred_element_type=F32)


def conv_layer(x, gain, w_in, dw_w, dw_b, ln_g, ln_b, w_out, *, tm=256):
    s, d = x.shape
    ch = dw_w.shape[1]
    ns = ch // LANES
    halo = 32
    xg = resident_matmul(x, w_in.astype(BF16), ch, w_offs=(0, ch), gain=gain, act="glu", name="conv_in")
    strips = lambda a: a.reshape(-1, ns, LANES).transpose(1, 0, 2)
    w_s = strips(jnp.pad(dw_w, ((0, halo - CONV_K), (0, 0))))
    const2 = lambda i: (0, 0)
    const3 = lambda i: (0, 0, 0)
    return pl.pallas_call(
        functools.partial(_conv_kernel, tm=tm, halo=halo, tn=512),
        out_shape=jax.ShapeDtypeStruct((s, d), F32),
        grid=(s // tm,),
        in_specs=[
            pl.BlockSpec((tm, ch), lambda i: (i, 0)),
            pl.BlockSpec((ns, halo, LANES), const3),
            pl.BlockSpec((ns, 1, LANES), const3),
            pl.BlockSpec((ns, 1, LANES), const3),
            pl.BlockSpec((ns, 1, LANES), const3),
            pl.BlockSpec((ch, d), const2),
            pl.BlockSpec((tm, d), lambda i: (i, 0)),
        ],
        out_specs=pl.BlockSpec((tm, d), lambda i: (i, 0)),
        scratch_shapes=[pltpu.VMEM((ns, tm + halo, LANES), F32), pltpu.VMEM((ns, tm, LANES), F32),
                        pltpu.VMEM((tm, ch), BF16)],
        compiler_params=_params("arbitrary"),
        name="conv_mod",
    )(xg, w_s, strips(dw_b.reshape(1, ch)), strips(ln_g.reshape(1, ch)), strips(ln_b.reshape(1, ch)),
      w_out.astype(BF16), x)


def kernel(x, positions, norm_mix, norm_ffn, final_norm, mla_w_in, mla_q_norm, mla_w_uq, mla_kv_norm,
           mla_w_ukv, mla_w_o, gdn_w_in, gdn_conv_w, gdn_a_log, gdn_dt_bias, gdn_o_norm, gdn_w_o, s5_w_in,
           s5_lam_re, s5_lam_im, s5_log_dt, s5_b_re, s5_b_im, s5_c_re, s5_c_im, s5_d, s5_w_out, cv_w_in,
           cv_dw_w, cv_dw_b, cv_ln_g, cv_ln_b, cv_w_out, ffn_w_gate, ffn_w_up, ffn_w_down):
    bsz, s, d = x.shape
    depth = norm_mix.shape[0]
    outs = []
    for b in range(bsz):
        h = x[b]
        pos = positions[b]
        for i in range(depth):
            m, j = i % 4, i // 4
            if m == 0:
                h = mla_layer(h, pos, norm_mix[i], mla_w_in[j], mla_q_norm[j], mla_w_uq[j], mla_kv_norm[j],
                              mla_w_ukv[j], mla_w_o[j])
            elif m == 1:
                h = gdn_layer(h, norm_mix[i], gdn_w_in[j], gdn_conv_w[j], gdn_a_log[j], gdn_dt_bias[j],
                              gdn_o_norm[j], gdn_w_o[j])
            elif m == 2:
                h = s5_layer(h, norm_mix[i], s5_w_in[j], s5_lam_re[j], s5_lam_im[j], s5_log_dt[j], s5_b_re[j],
                             s5_b_im[j], s5_c_re[j], s5_c_im[j], s5_d[j], s5_w_out[j])
            else:
                h = conv_layer(h, norm_mix[i], cv_w_in[j], cv_dw_w[j], cv_dw_b[j], cv_ln_g[j], cv_ln_b[j],
                               cv_w_out[j])
            fg = final_norm if i == depth - 1 else None
            h = ffn(h, norm_ffn[i], ffn_w_gate[i].astype(BF16), ffn_w_up[i].astype(BF16),
                    ffn_w_down[i].astype(BF16), fg)
        outs.append(h)
    return outs[0][None] if bsz == 1 else jnp.stack(outs)
```

```python
import functools
import math

import jax
import jax.numpy as jnp
from jax import lax
from jax.experimental import pallas as pl
from jax.experimental.pallas import tpu as pltpu

F32 = jnp.float32
BF16 = jnp.bfloat16
EPS = 1e-6
NEG_INF = -1e30

VMEM_LIMIT_BYTES = 56 * 1024 * 1024
LANES = 128

MLA_HEADS = 16
MLA_Q_RANK = 512
MLA_KV_RANK = 512
MLA_NOPE = 128
MLA_ROPE = 64
MLA_V = 128
MLA_HEAD_PAD = 256
ROPE_THETA = 10000.0
MLA_QSCALE = (MLA_NOPE + MLA_ROPE) ** -0.5 * math.log2(math.e)

GDN_QK_HEADS = 16
GDN_V_HEADS = 32
GDN_DK = 128
GDN_DV = 128
GDN_CONV = 4
GDN_CHUNK = 64
GDN_SUPER = 256

S5_GROUP = 16
S5_STATE = 64
S5_SEGMENTS = 8

CONV_K = 31


def _params(*sem):
    return pltpu.CompilerParams(dimension_semantics=sem, vmem_limit_bytes=VMEM_LIMIT_BYTES)


def _rms(x, g):
    return x * lax.rsqrt(jnp.mean(x * x, axis=-1, keepdims=True) + EPS) * g


def _silu(x):
    return x * jax.nn.sigmoid(x)


def _mm_kernel(*refs, has_norm, n_w, act, has_res):
    it = iter(refs)
    a_ref = next(it)
    g_ref = next(it) if has_norm else None
    w_refs = [next(it) for _ in range(n_w)]
    res_ref = next(it) if has_res else None
    o_ref = next(it)
    xn_ref = next(it) if has_norm else None

    if has_norm:
        @pl.when(pl.program_id(1) == 0)
        def _():
            xn_ref[...] = _rms(a_ref[...], g_ref[...]).astype(BF16)

        a = xn_ref[...]
    else:
        a = a_ref[...]
    accs = [jnp.dot(a, w[...], preferred_element_type=F32) for w in w_refs]
    if act == "glu":
        y = accs[0] * jax.nn.sigmoid(accs[1])
    else:
        y = accs[0]
    if has_res:
        y = res_ref[...] + y
    o_ref[...] = y.astype(o_ref.dtype)


def fused_matmul(a, ws, *, gain=None, act="none", res=None, out_dtype=F32, tm=512, tn=512,
                 w_col_offsets=None, out_map=None, out_shape=None, a_block=None, a_map=None, name="mm"):
    k = ws[0][0].shape[0]
    m = a.size // k
    n = ws[0][1]
    n_w = len(ws)
    tm = min(tm, m)
    tn = min(tn, n)
    assert m % tm == 0 and n % tn == 0
    has_norm = gain is not None
    has_res = res is not None
    if w_col_offsets is None:
        w_col_offsets = [0] * n_w
    if a_block is None:
        a_block, a_map = (tm, k), (lambda i, j: (i, 0))
    in_specs = [pl.BlockSpec(a_block, a_map)]
    args = [a]
    if has_norm:
        in_specs.append(pl.BlockSpec((1, k), lambda i, j: (0, 0)))
        args.append(gain.reshape(1, k))
    for (w, _), off in zip(ws, w_col_offsets):
        assert off % tn == 0
        ob = off // tn
        in_specs.append(pl.BlockSpec((k, tn), lambda i, j, ob=ob: (0, j + ob)))
        args.append(w)
    if has_res:
        in_specs.append(pl.BlockSpec((tm, tn), lambda i, j: (i, j)))
        args.append(res)
    if out_map is None:
        out_map = lambda i, j: (i, j)
    if out_shape is None:
        out_shape = (m, n)
    scratch = [pltpu.VMEM((tm, k), BF16)] if has_norm else []
    return pl.pallas_call(
        functools.partial(_mm_kernel, has_norm=has_norm, n_w=n_w, act=act, has_res=has_res),
        out_shape=jax.ShapeDtypeStruct(out_shape, out_dtype),
        grid=(m // tm, n // tn),
        in_specs=in_specs,
        out_specs=pl.BlockSpec((tm, tn), out_map),
        scratch_shapes=scratch,
        compiler_params=_params("arbitrary", "arbitrary"),
        name=name,
    )(*args)


def _mm_resident_kernel(*refs, has_norm, w_offs, act, has_res, n, tn):
    it = iter(refs)
    a_ref = next(it)
    g_ref = next(it) if has_norm else None
    w_ref = next(it)
    res_ref = next(it) if has_res else None
    o_ref = next(it)
    a = _rms(a_ref[...], g_ref[...]).astype(BF16) if has_norm else a_ref[...]
    for jn in range(n // tn):
        sl = slice(jn * tn, (jn + 1) * tn)
        accs = [jnp.dot(a, w_ref[:, off + jn * tn:off + (jn + 1) * tn], preferred_element_type=F32)
                for off in w_offs]
        y = accs[0] * jax.nn.sigmoid(accs[1]) if act == "glu" else accs[0]
        if has_res:
            y = res_ref[:, sl] + y
        o_ref[:, sl] = y.astype(o_ref.dtype)


def resident_matmul(a, w, n, *, w_offs=(0,), gain=None, act="none", res=None, out_dtype=F32, tm=512, tn=512,
                    a_block=None, a_map=None, out_map=None, out_shape=None, name="mm"):
    k = w.shape[0]
    m = a.size // k
    tm = min(tm, m)
    tn = min(tn, n)
    assert m % tm == 0 and n % tn == 0
    has_norm = gain is not None
    has_res = res is not None
    row = lambda i: (i, 0)
    const = lambda i: (0, 0)
    in_specs = [pl.BlockSpec(a_block or (tm, k), a_map or row)]
    args = [a]
    if has_norm:
        in_specs.append(pl.BlockSpec((1, k), const))
        args.append(gain.reshape(1, k))
    in_specs.append(pl.BlockSpec(w.shape, const, pipeline_mode=pl.Buffered(1)))
    args.append(w)
    if has_res:
        in_specs.append(pl.BlockSpec((tm, n), row))
        args.append(res)
    return pl.pallas_call(
        functools.partial(_mm_resident_kernel, has_norm=has_norm, w_offs=tuple(w_offs), act=act,
                          has_res=has_res, n=n, tn=tn),
        out_shape=jax.ShapeDtypeStruct(out_shape or (m, n), out_dtype),
        grid=(m // tm,),
        in_specs=in_specs,
        out_specs=pl.BlockSpec((tm, n), out_map or row),
        compiler_params=_params("arbitrary"),
        name=name,
    )(*args)


def _ffn_kernel(*refs, final_norm):
    if final_norm:
        x_ref, g_ref, wg_ref, wu_ref, wd_ref, fg_ref, o_ref, xn_ref = refs
    else:
        x_ref, g_ref, wg_ref, wu_ref, wd_ref, o_ref, xn_ref = refs
        fg_ref = None
    j = pl.program_id(1)

    @pl.when(j == 0)
    def _():
        x = x_ref[...]
        xn_ref[...] = _rms(x, g_ref[...]).astype(BF16)
        o_ref[...] = x

    xn = xn_ref[...]
    hg = jnp.dot(xn, wg_ref[...], preferred_element_type=F32)
    hu = jnp.dot(xn, wu_ref[...], preferred_element_type=F32)
    h = (_silu(hg) * hu).astype(BF16)
    o_ref[...] += jnp.dot(h, wd_ref[...], preferred_element_type=F32)

    if final_norm:
        @pl.when(j == pl.num_programs(1) - 1)
        def _():
            o_ref[...] = _rms(o_ref[...], fg_ref[...])


def ffn(x, gain, wg, wu, wd, final_gain=None, *, layer=0, tm=1024, th=512):
    m, d = x.shape
    hdim = wg.shape[2]
    assert m % tm == 0 and hdim % th == 0
    final_norm = final_gain is not None
    in_specs = [
        pl.BlockSpec((tm, d), lambda i, j: (i, 0)),
        pl.BlockSpec((1, d), lambda i, j: (0, 0)),
        pl.BlockSpec((None, d, th), lambda i, j: (layer, 0, j)),
        pl.BlockSpec((None, d, th), lambda i, j: (layer, 0, j)),
        pl.BlockSpec((None, th, d), lambda i, j: (layer, j, 0)),
    ]
    args = [x, gain.reshape(1, d), wg, wu, wd]
    if final_norm:
        in_specs.append(pl.BlockSpec((1, d), lambda i, j: (0, 0)))
        args.append(final_gain.reshape(1, d))
    return pl.pallas_call(
        functools.partial(_ffn_kernel, final_norm=final_norm),
        out_shape=jax.ShapeDtypeStruct((m, d), F32),
        grid=(m // tm, hdim // th),
        in_specs=in_specs,
        out_specs=pl.BlockSpec((tm, d), lambda i, j: (i, 0)),
        scratch_shapes=[pltpu.VMEM((tm, d), BF16)],
        compiler_params=_params("arbitrary", "arbitrary"),
        name="ffn",
    )(*args)


def _rope_block(xb, c, s1, s2):
    return xb * c + pltpu.roll(xb, 96, 1) * s1 + pltpu.roll(xb, 32, 1) * s2


def _mla_proj_kernel(x_ref, g_ref, pos_ref, invf_ref, win_ref, qn_ref, wuq_ref, kvn_ref, wukv_ref,
                     q_ref, k_ref, v_ref):
    xn = _rms(x_ref[...], g_ref[...]).astype(BF16)
    proj = jnp.dot(xn, win_ref[...], preferred_element_type=F32)
    qn = _rms(proj[:, :MLA_Q_RANK], qn_ref[...]).astype(BF16)
    kvn = _rms(proj[:, MLA_Q_RANK:MLA_Q_RANK + MLA_KV_RANK], kvn_ref[...]).astype(BF16)
    kr = proj[:, MLA_Q_RANK + MLA_KV_RANK:]

    ang = pos_ref[...].astype(F32) * invf_ref[...]
    cs = jnp.cos(ang)
    sn = jnp.sin(ang)
    lane = lax.broadcasted_iota(jnp.int32, ang.shape, 1)
    half = MLA_ROPE // 2
    c = jnp.where(lane < MLA_ROPE, cs, 0.0)
    s1 = jnp.where(lane < half, -sn, 0.0)
    s2 = jnp.where((lane >= half) & (lane < MLA_ROPE), sn, 0.0)

    k_rope = _rope_block(kr, c, s1, s2).astype(BF16)
    hp = MLA_HEAD_PAD
    for h in range(MLA_HEADS):
        qh = jnp.dot(qn, wuq_ref[:, h * hp:(h + 1) * hp], preferred_element_type=F32) * MLA_QSCALE
        q_ref[:, h * hp:h * hp + LANES] = qh[:, :LANES].astype(BF16)
        q_ref[:, h * hp + LANES:(h + 1) * hp] = _rope_block(qh[:, LANES:], c, s1, s2).astype(BF16)
        kvh = jnp.dot(kvn, wukv_ref[:, h * hp:(h + 1) * hp], preferred_element_type=F32)
        k_ref[:, h * hp:h * hp + LANES] = kvh[:, :LANES].astype(BF16)
        k_ref[:, h * hp + LANES:(h + 1) * hp] = k_rope
        v_ref[:, h * MLA_V:(h + 1) * MLA_V] = kvh[:, LANES:].astype(BF16)


def mla_proj(x, gain, pos, invf, w_in_p, q_norm, w_uq_p, kv_norm, w_ukv, *, tm=256):
    s, d = x.shape
    nq = MLA_HEADS * MLA_HEAD_PAD
    const = lambda i: (0, 0)
    return pl.pallas_call(
        _mla_proj_kernel,
        out_shape=(jax.ShapeDtypeStruct((s, nq), BF16), jax.ShapeDtypeStruct((s, nq), BF16),
                   jax.ShapeDtypeStruct((s, MLA_HEADS * MLA_V), BF16)),
        grid=(s // tm,),
        in_specs=[
            pl.BlockSpec((tm, d), lambda i: (i, 0)),
            pl.BlockSpec((1, d), const),
            pl.BlockSpec((tm, 1), lambda i: (i, 0)),
            pl.BlockSpec((1, LANES), const),
            pl.BlockSpec(w_in_p.shape, const),
            pl.BlockSpec((1, MLA_Q_RANK), const),
            pl.BlockSpec(w_uq_p.shape, const),
            pl.BlockSpec((1, MLA_KV_RANK), const),
            pl.BlockSpec(w_ukv.shape, const),
        ],
        out_specs=(pl.BlockSpec((tm, nq), lambda i: (i, 0)), pl.BlockSpec((tm, nq), lambda i: (i, 0)),
                   pl.BlockSpec((tm, MLA_HEADS * MLA_V), lambda i: (i, 0))),
        compiler_params=_params("arbitrary"),
        name="mla_proj",
    )(x, gain.reshape(1, d), pos, invf, w_in_p, q_norm.reshape(1, -1), w_uq_p, kv_norm.reshape(1, -1), w_ukv)


def _flash_kernel(q_ref, k_ref, v_ref, o_ref, m_ref, l_ref, acc_ref, *, tq, tk):
    qi = pl.program_id(1)
    m_ref[...] = jnp.full(m_ref.shape, NEG_INF, F32)
    l_ref[...] = jnp.zeros(l_ref.shape, F32)
    acc_ref[...] = jnp.zeros(acc_ref.shape, F32)

    def step(start, masked, r0=0):
        k = k_ref[pl.ds(start, tk), :]
        v = v_ref[pl.ds(start, tk), :]
        s = lax.dot_general(q_ref[r0:, :], k, (((1,), (1,)), ((), ())), preferred_element_type=F32)
        if masked:
            row = qi * tq + r0 + lax.broadcasted_iota(jnp.int32, s.shape, 0)
            col = start + lax.broadcasted_iota(jnp.int32, s.shape, 1)
            s = jnp.where(col <= row, s, NEG_INF)
        m_prev = m_ref[r0:, :]
        m_new = jnp.maximum(m_prev, jnp.max(s, axis=-1, keepdims=True))
        alpha = jnp.exp2(m_prev - m_new)
        l_new = alpha * l_ref[r0:, :]
        ps = []
        for c in range(tk // LANES):
            pc = jnp.exp2(s[:, c * LANES:(c + 1) * LANES] - m_new)
            l_new = l_new + pc
            ps.append(pc.astype(BF16))
        p = jnp.concatenate(ps, axis=1)
        acc_ref[r0:, :] = alpha * acc_ref[r0:, :] + jnp.dot(p, v, preferred_element_type=F32)
        l_ref[r0:, :] = l_new
        m_ref[r0:, :] = m_new

    nd = tq // tk

    def body(kp, carry):
        for u in range(nd):
            step(pl.multiple_of((kp * nd + u) * tk, tk), False)
        return carry

    lax.fori_loop(0, qi, body, 0)
    for dblk in range(nd):
        step(pl.multiple_of((qi * nd + dblk) * tk, tk), True, r0=dblk * tk)
    o_ref[...] = (acc_ref[...] / jnp.sum(l_ref[...], axis=-1, keepdims=True)).astype(o_ref.dtype)


def flash_attention(q, k, v, *, tq=1024, tk=512):
    s = q.shape[0]
    tq = min(tq, s)
    tk = min(tk, tq)
    hp, dv = MLA_HEAD_PAD, MLA_V
    return pl.pallas_call(
        functools.partial(_flash_kernel, tq=tq, tk=tk),
        out_shape=jax.ShapeDtypeStruct((s, MLA_HEADS * dv), BF16),
        grid=(MLA_HEADS, s // tq),
        in_specs=[
            pl.BlockSpec((tq, hp), lambda h, i: (i, h)),
            pl.BlockSpec((s, hp), lambda h, i: (0, h)),
            pl.BlockSpec((s, dv), lambda h, i: (0, h)),
        ],
        out_specs=pl.BlockSpec((tq, dv), lambda h, i: (i, h)),
        scratch_shapes=[pltpu.VMEM((tq, LANES), F32), pltpu.VMEM((tq, LANES), F32), pltpu.VMEM((tq, dv), F32)],
        compiler_params=_params("arbitrary", "arbitrary"),
        name="mla_flash",
    )(q, k, v)


def mla_layer(x, positions, gain, w_in, q_norm, w_uq, kv_norm, w_ukv, w_o):
    s, d = x.shape
    half = MLA_ROPE // 2
    inv = ROPE_THETA ** (-jnp.arange(half, dtype=F32) / half)
    invf = jnp.concatenate([inv, inv, jnp.zeros((LANES - MLA_ROPE,), F32)]).reshape(1, LANES)
    w_in_p = jnp.pad(w_in, ((0, 0), (0, LANES - MLA_ROPE))).astype(BF16)
    dq = MLA_NOPE + MLA_ROPE
    w_uq_p = jnp.pad(w_uq.reshape(MLA_Q_RANK, MLA_HEADS, dq), ((0, 0), (0, 0), (0, MLA_HEAD_PAD - dq)))
    w_uq_p = w_uq_p.reshape(MLA_Q_RANK, MLA_HEADS * MLA_HEAD_PAD).astype(BF16)
    q, k, v = mla_proj(x, gain, positions.reshape(s, 1), invf, w_in_p, q_norm, w_uq_p, kv_norm,
                       w_ukv.astype(BF16))
    o = flash_attention(q, k, v)
    return resident_matmul(o, w_o.astype(BF16), d, res=x, name="mla_out")


def _softplus(x):
    return jnp.maximum(x, 0.0) + jnp.log(1.0 + jnp.exp(-jnp.abs(x)))


def _gdn_proj_kernel(x_ref, g_ref, w_ref, cw_ref, wba_ref, alr_ref, dtr_ref,
                     o_ref, bg_ref, gt_ref, xn_ref, stage_ref, carry_ref, *, tm, tn, sub, n_qk, n_conv, qscale):
    i = pl.program_id(0)
    j = pl.program_id(1)
    nh = GDN_V_HEADS

    @pl.when(j == 0)
    def _():
        xn = _rms(x_ref[...], g_ref[...]).astype(BF16)
        xn_ref[...] = xn
        ba = jnp.dot(xn, wba_ref[...], preferred_element_type=F32)
        lane = lax.broadcasted_iota(jnp.int32, ba.shape, 1)
        gate = -jnp.exp(alr_ref[...]) * _softplus(ba + dtr_ref[...])
        bg_ref[...] = jnp.where(lane < nh, jax.nn.sigmoid(ba), gate)
        gt_ref[...] = gate.T[nh:2 * nh, :]

    nsub = tn // sub

    def proj(c):
        return jnp.dot(xn_ref[...], w_ref[:, c * sub:(c + 1) * sub], preferred_element_type=F32)

    def conv_silu(c):
        acc = proj(c)
        outs = []
        for hh in range(sub // LANES):
            k = c * (sub // LANES) + hh
            ks = slice(k * LANES, (k + 1) * LANES)
            a = acc[:, hh * LANES:(hh + 1) * LANES]
            stage_ref[k, 0:8, :] = jnp.where(i == 0, 0.0, carry_ref[j, :, ks])
            stage_ref[k, 8:, :] = a
            carry_ref[j, :, ks] = a[tm - 8:, :]
            cw = cw_ref[:, ks]
            y = stage_ref[k, pl.ds(8 - (GDN_CONV - 1), tm), :] * cw[0:1, :]
            for t in range(1, GDN_CONV):
                y = y + stage_ref[k, pl.ds(8 - (GDN_CONV - 1) + t, tm), :] * cw[t:t + 1, :]
            outs.append((ks, _silu(y)))
        return outs

    @pl.when(j < n_qk)
    def _():
        sc = jnp.where(j < n_qk // 2, qscale, 1.0)
        for c in range(nsub):
            for ks, yb in conv_silu(c):
                nrm = lax.rsqrt(jnp.sum(yb * yb, axis=-1, keepdims=True) + EPS) * sc
                o_ref[:, ks] = (yb * nrm).astype(o_ref.dtype)

    @pl.when((j >= n_qk) & (j < n_conv))
    def _():
        for c in range(nsub):
            for ks, yb in conv_silu(c):
                o_ref[:, ks] = yb.astype(o_ref.dtype)

    @pl.when(j >= n_conv)
    def _():
        for c in range(nsub):
            o_ref[:, c * sub:(c + 1) * sub] = proj(c).astype(o_ref.dtype)


def gdn_proj(x, gain, w_qkvz, conv_w_p, w_ba, a_log, dt_bias, *, tm=512, tn=1024, sub=256):
    s, d = x.shape
    qk_dim = GDN_QK_HEADS * GDN_DK
    v_dim = GDN_V_HEADS * GDN_DV
    n = 2 * qk_dim + 2 * v_dim
    assert n % tn == 0 and (2 * qk_dim) % tn == 0 and v_dim % tn == 0 and tn % sub == 0
    n_qk = 2 * qk_dim // tn
    n_conv = (2 * qk_dim + v_dim) // tn
    nh = GDN_V_HEADS
    pad = jnp.zeros((nh,), F32)
    alr = jnp.concatenate([pad, a_log, pad, pad]).reshape(1, LANES)
    dtr = jnp.concatenate([pad, dt_bias, pad, pad]).reshape(1, LANES)
    const = lambda i, j: (0, 0)
    kern = functools.partial(_gdn_proj_kernel, tm=tm, tn=tn, sub=sub, n_qk=n_qk, n_conv=n_conv,
                             qscale=GDN_DK ** -0.5)
    return pl.pallas_call(
        kern,
        out_shape=(jax.ShapeDtypeStruct((s, n), BF16), jax.ShapeDtypeStruct((s, LANES), F32),
                   jax.ShapeDtypeStruct((nh, s), F32)),
        grid=(s // tm, n // tn),
        in_specs=[
            pl.BlockSpec((tm, d), lambda i, j: (i, 0)),
            pl.BlockSpec((1, d), const),
            pl.BlockSpec((d, tn), lambda i, j: (0, j)),
            pl.BlockSpec((GDN_CONV, tn), lambda i, j: (0, jnp.minimum(j, n_conv - 1))),
            pl.BlockSpec((d, LANES), const),
            pl.BlockSpec((1, LANES), const),
            pl.BlockSpec((1, LANES), const),
        ],
        out_specs=(pl.BlockSpec((tm, tn), lambda i, j: (i, j)),
                   pl.BlockSpec((tm, LANES), lambda i, j: (i, 0)),
                   pl.BlockSpec((nh, tm), lambda i, j: (0, i))),
        scratch_shapes=[pltpu.VMEM((tm, d), BF16), pltpu.VMEM((tn // LANES, tm + 8, LANES), F32),
                        pltpu.VMEM((n_conv, 8, tn), F32)],
        compiler_params=_params("arbitrary", "arbitrary"),
        name="gdn_proj",
    )(x, gain.reshape(1, d), w_qkvz, conv_w_p, w_ba, alr, dtr)


def _gdn_delta_kernel(q_ref, k_ref, v_ref, z_ref, bg_ref, gt_ref, on_ref, o_ref, state_ref, mask_ref, tri_ref,
                      gtc_ref, *, hq, c):
    hb = pl.program_id(0)
    t = pl.program_id(1)
    nlev = c.bit_length() - 1
    row = lax.broadcasted_iota(jnp.int32, (c, c), 0)
    col = lax.broadcasted_iota(jnp.int32, (c, c), 1)

    @pl.when((hb == 0) & (t == 0))
    def _():
        for l in range(nlev):
            m = ((row >> (l + 1)) == (col >> (l + 1))) & (((row >> l) & 1) == 1) & (((col >> l) & 1) == 0)
            mask_ref[l] = jnp.where(m, 1.0, 0.0).astype(BF16)
        tri_ref[...] = (col <= row).astype(F32)

    @pl.when(t == 0)
    def _():
        state_ref[...] = jnp.zeros(state_ref.shape, F32)

    tri = tri_ref[...]
    incl = col <= row
    strict = col < row
    bg = bg_ref[...]
    lane = lax.broadcasted_iota(jnp.int32, bg.shape, 1)
    hi = lax.Precision.HIGHEST
    nt = (((1,), (1,)), ((), ()))

    gall = jnp.dot(tri, bg, precision=hi, preferred_element_type=F32)
    gtc_ref[...] = lax.dot_general(gt_ref[...], tri, nt, precision=hi, preferred_element_type=F32)

    nb = 2 * hq
    qs, ks, kfs, amats, attns, betas, gcs, glasts = [], [], [], [], [], [], [], []
    for a in range(hq):
        q = q_ref[:, a * GDN_DK:(a + 1) * GDN_DK]
        k = k_ref[:, a * GDN_DK:(a + 1) * GDN_DK]
        kk = lax.dot_general(k, k, nt, preferred_element_type=F32)
        qk = lax.dot_general(q, k, nt, preferred_element_type=F32)
        for b in range(2):
            vh = (hb * hq + a) * 2 + b
            beta = jnp.sum(jnp.where(lane == vh, bg, 0.0), axis=-1, keepdims=True)
            gc_col = jnp.sum(jnp.where(lane == vh + GDN_V_HEADS, gall, 0.0), axis=-1, keepdims=True)
            gc_row = gtc_ref[pl.ds(vh, 1), :]
            dec = jnp.exp(jnp.where(incl, gc_col - gc_row, NEG_INF))
            amats.append(jnp.where(strict, kk * dec, 0.0) * beta)
            attns.append((qk * dec).astype(BF16))
            qs.append(q)
            ks.append(k)
            betas.append(beta)
            gcs.append(gc_col)
            glasts.append(gc_col[c - 1:c, :])
    eye = jnp.where(row == col, 1.0, 0.0).astype(BF16)
    abs_ = [amats[i].astype(BF16) for i in range(nb)]
    ts = [eye - abs_[i] * mask_ref[0] for i in range(nb)]
    for l in range(1, nlev):
        xs = [jnp.dot(abs_[i] * mask_ref[l], ts[i], preferred_element_type=F32).astype(BF16) for i in range(nb)]
        ts = [ts[i] - jnp.dot(ts[i], xs[i], preferred_element_type=F32).astype(BF16) for i in range(nb)]
    rs = [ts[i] - eye for i in range(nb)]
    egs = [jnp.exp(gcs[i]) for i in range(nb)]
    kfs = [ks[i].astype(F32) for i in range(nb)]
    rhss = [jnp.concatenate([v_ref[:, i * GDN_DV:(i + 1) * GDN_DV].astype(F32) * betas[i],
                             kfs[i] * (betas[i] * egs[i])], axis=1) for i in range(nb)]
    uws = [rhss[i] + jnp.dot(rs[i], rhss[i].astype(BF16), preferred_element_type=F32)
           for i in range(nb)]
    sts = [state_ref[i] for i in range(nb)]
    wss = [jnp.dot(jnp.concatenate([uws[i][:, GDN_DV:].astype(BF16),
                                    (qs[i].astype(F32) * egs[i]).astype(BF16)], axis=0),
                   sts[i].astype(BF16), preferred_element_type=F32) for i in range(nb)]
    vnbs = [(uws[i][:, :GDN_DV] - wss[i][:c]).astype(BF16) for i in range(nb)]
    os_ = [wss[i][c:] + jnp.dot(attns[i], vnbs[i], preferred_element_type=F32) for i in range(nb)]
    for i in range(nb):
        kdec = (kfs[i] * jnp.exp(glasts[i] - gcs[i])).astype(BF16)
        state_ref[i] = sts[i] * jnp.exp(glasts[i]) + lax.dot_general(
            kdec, vnbs[i], (((0,), (0,)), ((), ())), preferred_element_type=F32)
        z = z_ref[:, i * GDN_DV:(i + 1) * GDN_DV].astype(F32)
        o_ref[:, i * GDN_DV:(i + 1) * GDN_DV] = (_rms(os_[i], on_ref[...]) * _silu(z)).astype(o_ref.dtype)


def gdn_delta(qkvz, bg, gt, o_norm, *, hq=4, c=GDN_SUPER):
    s = qkvz.shape[0]
    qk_dim = GDN_QK_HEADS * GDN_DK
    v_dim = GDN_V_HEADS * GDN_DV
    wq = hq * GDN_DK
    wv = 2 * hq * GDN_DV
    nlev = c.bit_length() - 1
    return pl.pallas_call(
        functools.partial(_gdn_delta_kernel, hq=hq, c=c),
        out_shape=jax.ShapeDtypeStruct((s, v_dim), BF16),
        grid=(GDN_QK_HEADS // hq, s // c),
        in_specs=[
            pl.BlockSpec((c, wq), lambda h, t: (t, h)),
            pl.BlockSpec((c, wq), lambda h, t: (t, qk_dim // wq + h)),
            pl.BlockSpec((c, wv), lambda h, t: (t, 2 * qk_dim // wv + h)),
            pl.BlockSpec((c, wv), lambda h, t: (t, (2 * qk_dim + v_dim) // wv + h)),
            pl.BlockSpec((c, LANES), lambda h, t: (t, 0)),
            pl.BlockSpec((GDN_V_HEADS, c), lambda h, t: (0, t)),
            pl.BlockSpec((1, GDN_DV), lambda h, t: (0, 0)),
        ],
        out_specs=pl.BlockSpec((c, wv), lambda h, t: (t, h)),
        scratch_shapes=[pltpu.VMEM((2 * hq, GDN_DK, GDN_DV), F32), pltpu.VMEM((nlev, c, c), BF16),
                        pltpu.VMEM((c, c), F32), pltpu.VMEM((GDN_V_HEADS, c), F32)],
        compiler_params=_params("arbitrary", "arbitrary"),
        name="gdn_delta",
    )(qkvz, qkvz, qkvz, qkvz, bg, gt, o_norm.reshape(1, GDN_DV))


def gdn_layer(x, gain, w_in, conv_w, a_log, dt_bias, o_norm, w_o):
    s, d = x.shape
    qk_dim = GDN_QK_HEADS * GDN_DK
    v_dim = GDN_V_HEADS * GDN_DV
    n_main = 2 * qk_dim + 2 * v_dim
    nh = GDN_V_HEADS
    w_b = w_in.astype(BF16)
    w_ba = jnp.pad(w_in[:, n_main:], ((0, 0), (0, LANES - 2 * nh))).astype(BF16)
    qkvz, bg, gt = gdn_proj(x, gain, w_b, conv_w, w_ba, a_log, dt_bias)
    o = gdn_delta(qkvz, bg, gt, o_norm)
    return resident_matmul(o, w_o.astype(BF16), d, res=x, name="gdn_out")


def _s5_disc_kernel(lre_ref, lim_ref, ldt_ref, lre_e_ref, lim_e_ref, bre_ref, bim_ref,
                    lbre_ref, lbim_ref, bbre_ref, bbim_ref):
    dt = jnp.exp(ldt_ref[...])

    def zoh(lre, lim):
        mag = jnp.exp(lre * dt)
        ang = lim * dt
        lb_re = mag * jnp.cos(ang)
        lb_im = mag * jnp.sin(ang)
        den = lre * lre + lim * lim
        nr = lb_re - 1.0
        f_re = (nr * lre + lb_im * lim) / den
        f_im = (lb_im * lre - nr * lim) / den
        return lb_re, lb_im, f_re, f_im

    lb_re, lb_im, _, _ = zoh(lre_ref[...], lim_ref[...])
    lbre_ref[...] = lb_re
    lbim_ref[...] = lb_im
    _, _, f_re, f_im = zoh(lre_e_ref[...], lim_e_ref[...])
    bbre_ref[...] = f_re * bre_ref[...] - f_im * bim_ref[...]
    bbim_ref[...] = f_re * bim_ref[...] + f_im * bre_ref[...]


def _s5_scan_kernel(u_ref, bblk_ref, cblk_ref, are_ref, aim_ref, d_ref, o_ref, st_ref, bu_ref, x_ref,
                    *, lt, n_sq):
    p = pl.program_id(1)
    tb = pl.program_id(2)
    last = pl.num_programs(2) - 1
    nseg = S5_SEGMENTS
    half = bu_ref.shape[1] // 2

    @pl.when((p == 0) & (tb == 0))
    def _():
        st_ref[...] = jnp.zeros(st_ref.shape, F32)

    u = u_ref[...]
    bu_ref[...] = jnp.dot(u.astype(BF16), bblk_ref[0], preferred_element_type=F32)
    ar = jnp.broadcast_to(are_ref[0], (nseg, half))
    ai = jnp.broadcast_to(aim_ref[0], (nseg, half))

    def body(tau, carry):
        xr, xi = carry
        r0 = pl.multiple_of(tau * nseg, nseg)
        b = bu_ref[pl.ds(r0, nseg), :]
        nxr = ar * xr - ai * xi + b[:, :half]
        nxi = ar * xi + ai * xr + b[:, half:]
        x_ref[pl.ds(r0, nseg), :half] = nxr
        x_ref[pl.ds(r0, nseg), half:] = nxi
        return nxr, nxi

    st = st_ref[...]
    xr, xi = lax.fori_loop(0, lt, body, (st[:, :half], st[:, half:]), unroll=8)
    st_ref[:, :half] = xr
    st_ref[:, half:] = xi

    @pl.when((p == 0) & (tb == last))
    def _():
        pr, pi = are_ref[0], aim_ref[0]
        for _ in range(n_sq):
            pr, pi = pr * pr - pi * pi, 2.0 * pr * pi
        sr = jnp.zeros((1, half), F32)
        si = jnp.zeros((1, half), F32)
        rows_r, rows_i = [sr], [si]
        for r in range(nseg - 1):
            sr, si = pr * sr - pi * si + xr[r:r + 1, :], pr * si + pi * sr + xi[r:r + 1, :]
            rows_r.append(sr)
            rows_i.append(si)
        st_ref[:, :half] = jnp.concatenate(rows_r, axis=0)
        st_ref[:, half:] = jnp.concatenate(rows_i, axis=0)

    @pl.when(p == 1)
    def _():
        y = jnp.dot(x_ref[...].astype(BF16), cblk_ref[0], preferred_element_type=F32) + d_ref[...] * u
        o_ref[...] = jax.nn.gelu(y).astype(o_ref.dtype)


def s5_layer(x, gain, w_in, lam_re, lam_im, log_dt, b_re, b_im, c_re, c_im, d_skip, w_out):
    s, d = x.shape
    width = w_in.shape[1]
    g, p_st, gc = b_re.shape
    nseg = S5_SEGMENTS
    sseg = s // nseg
    assert sseg & (sseg - 1) == 0
    tm = min(512, sseg)
    nb = sseg // tm
    ncb = width // LANES
    gpb = LANES // gc
    half = gpb * p_st

    u = resident_matmul(x, w_in.astype(BF16), width, gain=gain, tm=tm,
                        out_map=lambda i: (i % nb, i // nb), out_shape=(sseg, nseg * width), name="s5_in")
    u2 = u.reshape(s, width)

    rep = lambda a: jnp.repeat(a, gc, axis=1)
    vm = pl.BlockSpec(memory_space=pltpu.VMEM)
    lb_re, lb_im, bb_re, bb_im = pl.pallas_call(
        _s5_disc_kernel,
        out_shape=(jax.ShapeDtypeStruct((g, p_st), F32), jax.ShapeDtypeStruct((g, p_st), F32),
                   jax.ShapeDtypeStruct((g, p_st * gc), F32), jax.ShapeDtypeStruct((g, p_st * gc), F32)),
        in_specs=[vm] * 7, out_specs=(vm, vm, vm, vm), name="s5_disc",
    )(lam_re, lam_im, log_dt.reshape(g, 1), rep(lam_re), rep(lam_im),
      b_re.reshape(g, p_st * gc), b_im.reshape(g, p_st * gc))

    eye = jnp.eye(gpb, dtype=F32)

    def in_blocks(bb):
        t = bb.reshape(ncb, gpb, p_st, gc)
        return jnp.einsum("bgpc,gh->bgchp", t, eye).reshape(ncb, gpb * gc, gpb * p_st)

    def out_blocks(cc):
        t = cc.reshape(ncb, gpb, gc, p_st)
        return jnp.einsum("bgcp,gh->bhpgc", t, eye).reshape(ncb, gpb * p_st, gpb * gc)

    bblk = jnp.concatenate([in_blocks(bb_re), in_blocks(bb_im)], axis=2).astype(BF16)
    cblk = jnp.concatenate([out_blocks(c_re), -out_blocks(c_im)], axis=1).astype(BF16)
    a_re = lb_re.reshape(ncb, 1, half)
    a_im = lb_im.reshape(ncb, 1, half)

    lt = min(128, sseg)
    rows = lt * nseg
    yg = pl.pallas_call(
        functools.partial(_s5_scan_kernel, lt=lt, n_sq=sseg.bit_length() - 1),
        out_shape=jax.ShapeDtypeStruct((s, width), BF16),
        grid=(ncb, 2, sseg // lt),
        in_specs=[
            pl.BlockSpec((rows, LANES), lambda cb, p, t: (t, cb)),
            pl.BlockSpec((1, LANES, 2 * half), lambda cb, p, t: (cb, 0, 0)),
            pl.BlockSpec((1, 2 * half, LANES), lambda cb, p, t: (cb, 0, 0)),
            pl.BlockSpec((1, 1, half), lambda cb, p, t: (cb, 0, 0)),
            pl.BlockSpec((1, 1, half), lambda cb, p, t: (cb, 0, 0)),
            pl.BlockSpec((1, LANES), lambda cb, p, t: (0, cb)),
        ],
        out_specs=pl.BlockSpec((rows, LANES), lambda cb, p, t: (t * p, cb)),
        scratch_shapes=[pltpu.VMEM((nseg, 2 * half), F32), pltpu.VMEM((rows, 2 * half), F32),
                        pltpu.VMEM((rows, 2 * half), F32)],
        compiler_params=_params("arbitrary", "arbitrary", "arbitrary"),
        name="s5_scan",
    )(u2, bblk, cblk, a_re, a_im, d_skip.reshape(1, width))

    return resident_matmul(yg.reshape(sseg, nseg * width), w_out.astype(BF16), d, w_offs=(0, d), act="glu",
                           res=x, tm=tm, a_block=(tm, width), a_map=lambda i: (i % nb, i // nb), name="s5_out")


def _conv_kernel(xg_ref, w_ref, b_ref, lg_ref, lb_ref, wo_ref, res_ref, o_ref, stage_ref, y_ref, a_ref,
                 *, tm, halo, tn):
    i = pl.program_id(0)
    ns = stage_ref.shape[0]
    d = ns * LANES
    for c in range(ns):
        prev = stage_ref[c, tm:tm + halo, :]
        stage_ref[c, 0:halo, :] = jnp.where(i == 0, 0.0, prev)
        stage_ref[c, halo:, :] = xg_ref[:, c * LANES:(c + 1) * LANES]

    def strip(c, carry):
        w = w_ref[c]
        acc = stage_ref[c, pl.ds(halo - (CONV_K - 1), tm), :] * w[0:1, :]
        for k in range(1, CONV_K):
            acc = acc + stage_ref[c, pl.ds(halo - (CONV_K - 1) + k, tm), :] * w[k:k + 1, :]
        y_ref[c] = acc + b_ref[c]
        return carry

    lax.fori_loop(0, ns, strip, 0)

    tot = y_ref[0]
    for c in range(1, ns):
        tot = tot + y_ref[c]
    mu = jnp.sum(tot, axis=-1, keepdims=True) * (1.0 / d)
    sq = jnp.square(y_ref[0] - mu)
    for c in range(1, ns):
        sq = sq + jnp.square(y_ref[c] - mu)
    rstd = lax.rsqrt(jnp.sum(sq, axis=-1, keepdims=True) * (1.0 / d) + EPS)
    for c in range(ns):
        yn = (y_ref[c] - mu) * rstd * lg_ref[c] + lb_ref[c]
        a_ref[:, c * LANES:(c + 1) * LANES] = _silu(yn).astype(BF16)
    a = a_ref[...]
    for jn in range(d // tn):
        sl = slice(jn * tn, (jn + 1) * tn)
        o_ref[:, sl] = res_ref[:, sl] + jnp.dot(a, wo_ref[:, sl], preferred_element_type=F32)


def conv_layer(x, gain, w_in, dw_w, dw_b, ln_g, ln_b, w_out, *, tm=256):
    s, d = x.shape
    ch = dw_w.shape[1]
    ns = ch // LANES
    halo = 32
    xg = resident_matmul(x, w_in.astype(BF16), ch, w_offs=(0, ch), gain=gain, act="glu", name="conv_in")
    strips = lambda a: a.reshape(-1, ns, LANES).transpose(1, 0, 2)
    w_s = strips(jnp.pad(dw_w, ((0, halo - CONV_K), (0, 0))))
    const2 = lambda i: (0, 0)
    const3 = lambda i: (0, 0, 0)
    return pl.pallas_call(
        functools.partial(_conv_kernel, tm=tm, halo=halo, tn=512),
        out_shape=jax.ShapeDtypeStruct((s, d), F32),
        grid=(s // tm,),
        in_specs=[
            pl.BlockSpec((tm, ch), lambda i: (i, 0)),
            pl.BlockSpec((ns, halo, LANES), const3),
            pl.BlockSpec((ns, 1, LANES), const3),
            pl.BlockSpec((ns, 1, LANES), const3),
            pl.BlockSpec((ns, 1, LANES), const3),
            pl.BlockSpec((ch, d), const2),
            pl.BlockSpec((tm, d), lambda i: (i, 0)),
        ],
        out_specs=pl.BlockSpec((tm, d), lambda i: (i, 0)),
        scratch_shapes=[pltpu.VMEM((ns, tm + halo, LANES), F32), pltpu.VMEM((ns, tm, LANES), F32),
                        pltpu.VMEM((tm, ch), BF16)],
        compiler_params=_params("arbitrary"),
        name="conv_mod",
    )(xg, w_s, strips(dw_b.reshape(1, ch)), strips(ln_g.reshape(1, ch)), strips(ln_b.reshape(1, ch)),
      w_out.astype(BF16), x)


def kernel(x, positions, norm_mix, norm_ffn, final_norm, mla_w_in, mla_q_norm, mla_w_uq, mla_kv_norm,
           mla_w_ukv, mla_w_o, gdn_w_in, gdn_conv_w, gdn_a_log, gdn_dt_bias, gdn_o_norm, gdn_w_o, s5_w_in,
           s5_lam_re, s5_lam_im, s5_log_dt, s5_b_re, s5_b_im, s5_c_re, s5_c_im, s5_d, s5_w_out, cv_w_in,
           cv_dw_w, cv_dw_b, cv_ln_g, cv_ln_b, cv_w_out, ffn_w_gate, ffn_w_up, ffn_w_down):
    bsz, s, d = x.shape
    depth = norm_mix.shape[0]
    wg_b, wu_b, wd_b = ffn_w_gate.astype(BF16), ffn_w_up.astype(BF16), ffn_w_down.astype(BF16)
    outs = []
    for b in range(bsz):
        h = x[b]
        pos = positions[b]
        for i in range(depth):
            m, j = i % 4, i // 4
            if m == 0:
                h = mla_layer(h, pos, norm_mix[i], mla_w_in[j], mla_q_norm[j], mla_w_uq[j], mla_kv_norm[j],
                              mla_w_ukv[j], mla_w_o[j])
            elif m == 1:
                h = gdn_layer(h, norm_mix[i], gdn_w_in[j], gdn_conv_w[j], gdn_a_log[j], gdn_dt_bias[j],
                              gdn_o_norm[j], gdn_w_o[j])
            elif m == 2:
                h = s5_layer(h, norm_mix[i], s5_w_in[j], s5_lam_re[j], s5_lam_im[j], s5_log_dt[j], s5_b_re[j],
                             s5_b_im[j], s5_c_re[j], s5_c_im[j], s5_d[j], s5_w_out[j])
            else:
                h = conv_layer(h, norm_mix[i], cv_w_in[j], cv_dw_w[j], cv_dw_b[j], cv_ln_g[j], cv_ln_b[j],
                               cv_w_out[j])
            fg = final_norm if i == depth - 1 else None
            h = ffn(h, norm_ffn[i], wg_b, wu_b, wd_b, fg, layer=i)
        outs.append(h)
    return outs[0][None] if bsz == 1 else jnp.stack(outs)
```

```python
import functools
import math

import jax
import jax.numpy as jnp
from jax import lax
from jax.experimental import pallas as pl
from jax.experimental.pallas import tpu as pltpu

F32 = jnp.float32
BF16 = jnp.bfloat16
EPS = 1e-6
NEG_INF = -1e30

VMEM_LIMIT_BYTES = 56 * 1024 * 1024
LANES = 128

MLA_HEADS = 16
MLA_Q_RANK = 512
MLA_KV_RANK = 512
MLA_NOPE = 128
MLA_ROPE = 64
MLA_V = 128
MLA_HEAD_PAD = 256
ROPE_THETA = 10000.0
MLA_QSCALE = (MLA_NOPE + MLA_ROPE) ** -0.5 * math.log2(math.e)

GDN_QK_HEADS = 16
GDN_V_HEADS = 32
GDN_DK = 128
GDN_DV = 128
GDN_CONV = 4
GDN_CHUNK = 64
GDN_SUPER = 256

S5_GROUP = 16
S5_STATE = 64
S5_SEGMENTS = 8

CONV_K = 31


def _params(*sem):
    return pltpu.CompilerParams(dimension_semantics=sem, vmem_limit_bytes=VMEM_LIMIT_BYTES)


def _rms(x, g):
    return x * lax.rsqrt(jnp.mean(x * x, axis=-1, keepdims=True) + EPS) * g


def _silu(x):
    return x * jax.nn.sigmoid(x)


def _mm_kernel(*refs, has_norm, n_w, act, has_res):
    it = iter(refs)
    a_ref = next(it)
    g_ref = next(it) if has_norm else None
    w_refs = [next(it) for _ in range(n_w)]
    res_ref = next(it) if has_res else None
    o_ref = next(it)
    xn_ref = next(it) if has_norm else None

    if has_norm:
        @pl.when(pl.program_id(1) == 0)
        def _():
            xn_ref[...] = _rms(a_ref[...], g_ref[...]).astype(BF16)

        a = xn_ref[...]
    else:
        a = a_ref[...]
    accs = [jnp.dot(a, w[...], preferred_element_type=F32) for w in w_refs]
    if act == "glu":
        y = accs[0] * jax.nn.sigmoid(accs[1])
    else:
        y = accs[0]
    if has_res:
        y = res_ref[...] + y
    o_ref[...] = y.astype(o_ref.dtype)


def fused_matmul(a, ws, *, gain=None, act="none", res=None, out_dtype=F32, tm=512, tn=512,
                 w_col_offsets=None, out_map=None, out_shape=None, a_block=None, a_map=None, name="mm"):
    k = ws[0][0].shape[0]
    m = a.size // k
    n = ws[0][1]
    n_w = len(ws)
    tm = min(tm, m)
    tn = min(tn, n)
    assert m % tm == 0 and n % tn == 0
    has_norm = gain is not None
    has_res = res is not None
    if w_col_offsets is None:
        w_col_offsets = [0] * n_w
    if a_block is None:
        a_block, a_map = (tm, k), (lambda i, j: (i, 0))
    in_specs = [pl.BlockSpec(a_block, a_map)]
    args = [a]
    if has_norm:
        in_specs.append(pl.BlockSpec((1, k), lambda i, j: (0, 0)))
        args.append(gain.reshape(1, k))
    for (w, _), off in zip(ws, w_col_offsets):
        assert off % tn == 0
        ob = off // tn
        in_specs.append(pl.BlockSpec((k, tn), lambda i, j, ob=ob: (0, j + ob)))
        args.append(w)
    if has_res:
        in_specs.append(pl.BlockSpec((tm, tn), lambda i, j: (i, j)))
        args.append(res)
    if out_map is None:
        out_map = lambda i, j: (i, j)
    if out_shape is None:
        out_shape = (m, n)
    scratch = [pltpu.VMEM((tm, k), BF16)] if has_norm else []
    return pl.pallas_call(
        functools.partial(_mm_kernel, has_norm=has_norm, n_w=n_w, act=act, has_res=has_res),
        out_shape=jax.ShapeDtypeStruct(out_shape, out_dtype),
        grid=(m // tm, n // tn),
        in_specs=in_specs,
        out_specs=pl.BlockSpec((tm, tn), out_map),
        scratch_shapes=scratch,
        compiler_params=_params("arbitrary", "arbitrary"),
        name=name,
    )(*args)


def _mm_resident_kernel(*refs, has_norm, w_offs, act, has_res, n, tn):
    it = iter(refs)
    a_ref = next(it)
    g_ref = next(it) if has_norm else None
    w_ref = next(it)
    res_ref = next(it) if has_res else None
    o_ref = next(it)
    a = _rms(a_ref[...], g_ref[...]).astype(BF16) if has_norm else a_ref[...]
    for jn in range(n // tn):
        sl = slice(jn * tn, (jn + 1) * tn)
        accs = [jnp.dot(a, w_ref[:, off + jn * tn:off + (jn + 1) * tn], preferred_element_type=F32)
                for off in w_offs]
        y = accs[0] * jax.nn.sigmoid(accs[1]) if act == "glu" else accs[0]
        if has_res:
            y = res_ref[:, sl] + y
        o_ref[:, sl] = y.astype(o_ref.dtype)


def resident_matmul(a, w, n, *, w_offs=(0,), gain=None, act="none", res=None, out_dtype=F32, tm=512, tn=512,
                    a_block=None, a_map=None, out_map=None, out_shape=None, name="mm"):
    k = w.shape[0]
    m = a.size // k
    tm = min(tm, m)
    tn = min(tn, n)
    assert m % tm == 0 and n % tn == 0
    has_norm = gain is not None
    has_res = res is not None
    row = lambda i: (i, 0)
    const = lambda i: (0, 0)
    in_specs = [pl.BlockSpec(a_block or (tm, k), a_map or row)]
    args = [a]
    if has_norm:
        in_specs.append(pl.BlockSpec((1, k), const))
        args.append(gain.reshape(1, k))
    in_specs.append(pl.BlockSpec(w.shape, const, pipeline_mode=pl.Buffered(1)))
    args.append(w)
    if has_res:
        in_specs.append(pl.BlockSpec((tm, n), row))
        args.append(res)
    return pl.pallas_call(
        functools.partial(_mm_resident_kernel, has_norm=has_norm, w_offs=tuple(w_offs), act=act,
                          has_res=has_res, n=n, tn=tn),
        out_shape=jax.ShapeDtypeStruct(out_shape or (m, n), out_dtype),
        grid=(m // tm,),
        in_specs=in_specs,
        out_specs=pl.BlockSpec((tm, n), out_map or row),
        compiler_params=_params("arbitrary"),
        name=name,
    )(*args)


def _ffn_kernel(*refs, final_norm):
    if final_norm:
        x_ref, g_ref, wg_ref, wu_ref, wd_ref, fg_ref, o_ref, xn_ref = refs
    else:
        x_ref, g_ref, wg_ref, wu_ref, wd_ref, o_ref, xn_ref = refs
        fg_ref = None
    j = pl.program_id(1)

    @pl.when(j == 0)
    def _():
        x = x_ref[...]
        xn_ref[...] = _rms(x, g_ref[...]).astype(BF16)
        o_ref[...] = x

    xn = xn_ref[...]
    hg = jnp.dot(xn, wg_ref[...], preferred_element_type=F32)
    hu = jnp.dot(xn, wu_ref[...], preferred_element_type=F32)
    h = (_silu(hg) * hu).astype(BF16)
    o_ref[...] += jnp.dot(h, wd_ref[...], preferred_element_type=F32)

    if final_norm:
        @pl.when(j == pl.num_programs(1) - 1)
        def _():
            o_ref[...] = _rms(o_ref[...], fg_ref[...])


def ffn(x, gain, wg, wu, wd, final_gain=None, *, layer=0, tm=1024, th=512):
    m, d = x.shape
    hdim = wg.shape[2]
    assert m % tm == 0 and hdim % th == 0
    final_norm = final_gain is not None
    in_specs = [
        pl.BlockSpec((tm, d), lambda i, j: (i, 0)),
        pl.BlockSpec((1, d), lambda i, j: (0, 0)),
        pl.BlockSpec((None, d, th), lambda i, j: (layer, 0, j)),
        pl.BlockSpec((None, d, th), lambda i, j: (layer, 0, j)),
        pl.BlockSpec((None, th, d), lambda i, j: (layer, j, 0)),
    ]
    args = [x, gain.reshape(1, d), wg, wu, wd]
    if final_norm:
        in_specs.append(pl.BlockSpec((1, d), lambda i, j: (0, 0)))
        args.append(final_gain.reshape(1, d))
    return pl.pallas_call(
        functools.partial(_ffn_kernel, final_norm=final_norm),
        out_shape=jax.ShapeDtypeStruct((m, d), F32),
        grid=(m // tm, hdim // th),
        in_specs=in_specs,
        out_specs=pl.BlockSpec((tm, d), lambda i, j: (i, 0)),
        scratch_shapes=[pltpu.VMEM((tm, d), BF16)],
        compiler_params=_params("arbitrary", "arbitrary"),
        name="ffn",
    )(*args)


def _rope_block(xb, c, s1, s2):
    return xb * c + pltpu.roll(xb, 96, 1) * s1 + pltpu.roll(xb, 32, 1) * s2


def _mla_proj_kernel(x_ref, g_ref, pos_ref, invf_ref, win_ref, qn_ref, wuq_ref, kvn_ref, wukv_ref,
                     q_ref, k_ref, v_ref):
    xn = _rms(x_ref[...], g_ref[...]).astype(BF16)
    proj = jnp.dot(xn, win_ref[...], preferred_element_type=F32)
    qn = _rms(proj[:, :MLA_Q_RANK], qn_ref[...]).astype(BF16)
    kvn = _rms(proj[:, MLA_Q_RANK:MLA_Q_RANK + MLA_KV_RANK], kvn_ref[...]).astype(BF16)
    kr = proj[:, MLA_Q_RANK + MLA_KV_RANK:]

    ang = pos_ref[...].astype(F32) * invf_ref[...]
    cs = jnp.cos(ang)
    sn = jnp.sin(ang)
    lane = lax.broadcasted_iota(jnp.int32, ang.shape, 1)
    half = MLA_ROPE // 2
    c = jnp.where(lane < MLA_ROPE, cs, 0.0)
    s1 = jnp.where(lane < half, -sn, 0.0)
    s2 = jnp.where((lane >= half) & (lane < MLA_ROPE), sn, 0.0)

    k_rope = _rope_block(kr, c, s1, s2).astype(BF16)
    hp = MLA_HEAD_PAD
    for h in range(MLA_HEADS):
        qh = jnp.dot(qn, wuq_ref[:, h * hp:(h + 1) * hp], preferred_element_type=F32) * MLA_QSCALE
        q_ref[:, h * hp:h * hp + LANES] = qh[:, :LANES].astype(BF16)
        q_ref[:, h * hp + LANES:(h + 1) * hp] = _rope_block(qh[:, LANES:], c, s1, s2).astype(BF16)
        kvh = jnp.dot(kvn, wukv_ref[:, h * hp:(h + 1) * hp], preferred_element_type=F32)
        k_ref[:, h * hp:h * hp + LANES] = kvh[:, :LANES].astype(BF16)
        k_ref[:, h * hp + LANES:(h + 1) * hp] = k_rope
        v_ref[:, h * MLA_V:(h + 1) * MLA_V] = kvh[:, LANES:].astype(BF16)


def mla_proj(x, gain, pos, invf, w_in_p, q_norm, w_uq_p, kv_norm, w_ukv, *, tm=512):
    s, d = x.shape
    nq = MLA_HEADS * MLA_HEAD_PAD
    const = lambda i: (0, 0)
    return pl.pallas_call(
        _mla_proj_kernel,
        out_shape=(jax.ShapeDtypeStruct((s, nq), BF16), jax.ShapeDtypeStruct((s, nq), BF16),
                   jax.ShapeDtypeStruct((s, MLA_HEADS * MLA_V), BF16)),
        grid=(s // tm,),
        in_specs=[
            pl.BlockSpec((tm, d), lambda i: (i, 0)),
            pl.BlockSpec((1, d), const),
            pl.BlockSpec((tm, 1), lambda i: (i, 0)),
            pl.BlockSpec((1, LANES), const),
            pl.BlockSpec(w_in_p.shape, const, pipeline_mode=pl.Buffered(1)),
            pl.BlockSpec((1, MLA_Q_RANK), const),
            pl.BlockSpec(w_uq_p.shape, const, pipeline_mode=pl.Buffered(1)),
            pl.BlockSpec((1, MLA_KV_RANK), const),
            pl.BlockSpec(w_ukv.shape, const, pipeline_mode=pl.Buffered(1)),
        ],
        out_specs=(pl.BlockSpec((tm, nq), lambda i: (i, 0)), pl.BlockSpec((tm, nq), lambda i: (i, 0)),
                   pl.BlockSpec((tm, MLA_HEADS * MLA_V), lambda i: (i, 0))),
        compiler_params=_params("arbitrary"),
        name="mla_proj",
    )(x, gain.reshape(1, d), pos, invf, w_in_p, q_norm.reshape(1, -1), w_uq_p, kv_norm.reshape(1, -1), w_ukv)


def _flash_kernel(q_ref, k_ref, v_ref, o_ref, m_ref, l_ref, acc_ref, *, tq, tk):
    qi = pl.program_id(1)
    m_ref[...] = jnp.full(m_ref.shape, NEG_INF, F32)
    l_ref[...] = jnp.zeros(l_ref.shape, F32)
    acc_ref[...] = jnp.zeros(acc_ref.shape, F32)

    def step(start, masked, r0=0):
        k = k_ref[pl.ds(start, tk), :]
        v = v_ref[pl.ds(start, tk), :]
        s = lax.dot_general(q_ref[r0:, :], k, (((1,), (1,)), ((), ())), preferred_element_type=F32)
        if masked:
            row = qi * tq + r0 + lax.broadcasted_iota(jnp.int32, s.shape, 0)
            col = start + lax.broadcasted_iota(jnp.int32, s.shape, 1)
            s = jnp.where(col <= row, s, NEG_INF)
        m_prev = m_ref[r0:, :]
        m_new = jnp.maximum(m_prev, jnp.max(s, axis=-1, keepdims=True))
        alpha = jnp.exp2(m_prev - m_new)
        l_new = alpha * l_ref[r0:, :]
        ps = []
        for c in range(tk // LANES):
            pc = jnp.exp2(s[:, c * LANES:(c + 1) * LANES] - m_new)
            l_new = l_new + pc
            ps.append(pc.astype(BF16))
        p = jnp.concatenate(ps, axis=1)
        acc_ref[r0:, :] = alpha * acc_ref[r0:, :] + jnp.dot(p, v, preferred_element_type=F32)
        l_ref[r0:, :] = l_new
        m_ref[r0:, :] = m_new

    nd = tq // tk

    def body(kp, carry):
        for u in range(nd):
            step(pl.multiple_of((kp * nd + u) * tk, tk), False)
        return carry

    lax.fori_loop(0, qi, body, 0)
    for dblk in range(nd):
        step(pl.multiple_of((qi * nd + dblk) * tk, tk), True, r0=dblk * tk)
    o_ref[...] = (acc_ref[...] / jnp.sum(l_ref[...], axis=-1, keepdims=True)).astype(o_ref.dtype)


def flash_attention(q, k, v, *, tq=2048, tk=512):
    s = q.shape[0]
    tq = min(tq, s)
    tk = min(tk, tq)
    hp, dv = MLA_HEAD_PAD, MLA_V
    return pl.pallas_call(
        functools.partial(_flash_kernel, tq=tq, tk=tk),
        out_shape=jax.ShapeDtypeStruct((s, MLA_HEADS * dv), BF16),
        grid=(MLA_HEADS, s // tq),
        in_specs=[
            pl.BlockSpec((tq, hp), lambda h, i: (i, h)),
            pl.BlockSpec((s, hp), lambda h, i: (0, h)),
            pl.BlockSpec((s, dv), lambda h, i: (0, h)),
        ],
        out_specs=pl.BlockSpec((tq, dv), lambda h, i: (i, h)),
        scratch_shapes=[pltpu.VMEM((tq, LANES), F32), pltpu.VMEM((tq, LANES), F32), pltpu.VMEM((tq, dv), F32)],
        compiler_params=_params("arbitrary", "arbitrary"),
        name="mla_flash",
    )(q, k, v)


def mla_layer(x, positions, gain, w_in, q_norm, w_uq, kv_norm, w_ukv, w_o):
    s, d = x.shape
    half = MLA_ROPE // 2
    inv = ROPE_THETA ** (-jnp.arange(half, dtype=F32) / half)
    invf = jnp.concatenate([inv, inv, jnp.zeros((LANES - MLA_ROPE,), F32)]).reshape(1, LANES)
    w_in_p = jnp.pad(w_in, ((0, 0), (0, LANES - MLA_ROPE))).astype(BF16)
    dq = MLA_NOPE + MLA_ROPE
    w_uq_p = jnp.pad(w_uq.reshape(MLA_Q_RANK, MLA_HEADS, dq), ((0, 0), (0, 0), (0, MLA_HEAD_PAD - dq)))
    w_uq_p = w_uq_p.reshape(MLA_Q_RANK, MLA_HEADS * MLA_HEAD_PAD).astype(BF16)
    q, k, v = mla_proj(x, gain, positions.reshape(s, 1), invf, w_in_p, q_norm, w_uq_p, kv_norm,
                       w_ukv.astype(BF16))
    o = flash_attention(q, k, v)
    return resident_matmul(o, w_o.astype(BF16), d, res=x, name="mla_out")


def _softplus(x):
    return jnp.maximum(x, 0.0) + jnp.log(1.0 + jnp.exp(-jnp.abs(x)))


def _gdn_proj_kernel(x_ref, g_ref, w_ref, cw_ref, wba_ref, alr_ref, dtr_ref,
                     o_ref, bg_ref, gt_ref, xn_ref, stage_ref, carry_ref, *, tm, tn, sub, n_qk, n_conv, qscale):
    i = pl.program_id(0)
    j = pl.program_id(1)
    nh = GDN_V_HEADS

    @pl.when(j == 0)
    def _():
        xn = _rms(x_ref[...], g_ref[...]).astype(BF16)
        xn_ref[...] = xn
        ba = jnp.dot(xn, wba_ref[...], preferred_element_type=F32)
        lane = lax.broadcasted_iota(jnp.int32, ba.shape, 1)
        gate = -jnp.exp(alr_ref[...]) * _softplus(ba + dtr_ref[...])
        bg_ref[...] = jnp.where(lane < nh, jax.nn.sigmoid(ba), gate)
        gt_ref[...] = gate.T[nh:2 * nh, :]

    nsub = tn // sub

    def proj(c):
        return jnp.dot(xn_ref[...], w_ref[:, c * sub:(c + 1) * sub], preferred_element_type=F32)

    def conv_silu(c):
        acc = proj(c)
        outs = []
        for hh in range(sub // LANES):
            k = c * (sub // LANES) + hh
            ks = slice(k * LANES, (k + 1) * LANES)
            a = acc[:, hh * LANES:(hh + 1) * LANES]
            stage_ref[k, 0:8, :] = jnp.where(i == 0, 0.0, carry_ref[j, :, ks])
            stage_ref[k, 8:, :] = a
            carry_ref[j, :, ks] = a[tm - 8:, :]
            cw = cw_ref[:, ks]
            y = stage_ref[k, pl.ds(8 - (GDN_CONV - 1), tm), :] * cw[0:1, :]
            for t in range(1, GDN_CONV):
                y = y + stage_ref[k, pl.ds(8 - (GDN_CONV - 1) + t, tm), :] * cw[t:t + 1, :]
            outs.append((ks, _silu(y)))
        return outs

    @pl.when(j < n_qk)
    def _():
        sc = jnp.where(j < n_qk // 2, qscale, 1.0)
        for c in range(nsub):
            for ks, yb in conv_silu(c):
                nrm = lax.rsqrt(jnp.sum(yb * yb, axis=-1, keepdims=True) + EPS) * sc
                o_ref[:, ks] = (yb * nrm).astype(o_ref.dtype)

    @pl.when((j >= n_qk) & (j < n_conv))
    def _():
        for c in range(nsub):
            for ks, yb in conv_silu(c):
                o_ref[:, ks] = yb.astype(o_ref.dtype)

    @pl.when(j >= n_conv)
    def _():
        for c in range(nsub):
            o_ref[:, c * sub:(c + 1) * sub] = proj(c).astype(o_ref.dtype)


def gdn_proj(x, gain, w_qkvz, conv_w_p, w_ba, a_log, dt_bias, *, tm=512, tn=2048, sub=256):
    s, d = x.shape
    qk_dim = GDN_QK_HEADS * GDN_DK
    v_dim = GDN_V_HEADS * GDN_DV
    n = 2 * qk_dim + 2 * v_dim
    assert n % tn == 0 and (2 * qk_dim) % tn == 0 and v_dim % tn == 0 and tn % sub == 0
    n_qk = 2 * qk_dim // tn
    n_conv = (2 * qk_dim + v_dim) // tn
    nh = GDN_V_HEADS
    pad = jnp.zeros((nh,), F32)
    alr = jnp.concatenate([pad, a_log, pad, pad]).reshape(1, LANES)
    dtr = jnp.concatenate([pad, dt_bias, pad, pad]).reshape(1, LANES)
    const = lambda i, j: (0, 0)
    kern = functools.partial(_gdn_proj_kernel, tm=tm, tn=tn, sub=sub, n_qk=n_qk, n_conv=n_conv,
                             qscale=GDN_DK ** -0.5)
    return pl.pallas_call(
        kern,
        out_shape=(jax.ShapeDtypeStruct((s, n), BF16), jax.ShapeDtypeStruct((s, LANES), F32),
                   jax.ShapeDtypeStruct((nh, s), F32)),
        grid=(s // tm, n // tn),
        in_specs=[
            pl.BlockSpec((tm, d), lambda i, j: (i, 0)),
            pl.BlockSpec((1, d), const),
            pl.BlockSpec((d, tn), lambda i, j: (0, j)),
            pl.BlockSpec((GDN_CONV, tn), lambda i, j: (0, jnp.minimum(j, n_conv - 1))),
            pl.BlockSpec((d, LANES), const),
            pl.BlockSpec((1, LANES), const),
            pl.BlockSpec((1, LANES), const),
        ],
        out_specs=(pl.BlockSpec((tm, tn), lambda i, j: (i, j)),
                   pl.BlockSpec((tm, LANES), lambda i, j: (i, 0)),
                   pl.BlockSpec((nh, tm), lambda i, j: (0, i))),
        scratch_shapes=[pltpu.VMEM((tm, d), BF16), pltpu.VMEM((tn // LANES, tm + 8, LANES), F32),
                        pltpu.VMEM((n_conv, 8, tn), F32)],
        compiler_params=_params("arbitrary", "arbitrary"),
        name="gdn_proj",
    )(x, gain.reshape(1, d), w_qkvz, conv_w_p, w_ba, alr, dtr)


def _gdn_delta_kernel(q_ref, k_ref, v_ref, z_ref, bg_ref, gt_ref, on_ref, o_ref, state_ref, mask_ref, tri_ref,
                      gtc_ref, *, hq, c):
    hb = pl.program_id(0)
    t = pl.program_id(1)
    nlev = c.bit_length() - 1
    row = lax.broadcasted_iota(jnp.int32, (c, c), 0)
    col = lax.broadcasted_iota(jnp.int32, (c, c), 1)

    @pl.when((hb == 0) & (t == 0))
    def _():
        for l in range(nlev):
            m = ((row >> (l + 1)) == (col >> (l + 1))) & (((row >> l) & 1) == 1) & (((col >> l) & 1) == 0)
            mask_ref[l] = jnp.where(m, 1.0, 0.0).astype(BF16)
        tri_ref[...] = jnp.where(col <= row, 1.0, 0.0).astype(BF16)

    @pl.when(t == 0)
    def _():
        state_ref[...] = jnp.zeros(state_ref.shape, F32)

    tri = tri_ref[...]
    incl = col <= row
    strict = col < row
    bg = bg_ref[...]
    lane = lax.broadcasted_iota(jnp.int32, bg.shape, 1)
    nt = (((1,), (1,)), ((), ()))

    def split3(x):
        x1 = x.astype(BF16)
        r1 = x - x1.astype(F32)
        x2 = r1.astype(BF16)
        return x1, x2, (r1 - x2.astype(F32)).astype(BF16)

    gall = sum(jnp.dot(tri, p, preferred_element_type=F32) for p in split3(bg))
    gtc_ref[...] = sum(lax.dot_general(p, tri, nt, preferred_element_type=F32) for p in split3(gt_ref[...]))

    nb = 2 * hq
    qs, ks, kfs, amats, attns, betas, gcs, glasts = [], [], [], [], [], [], [], []
    for a in range(hq):
        q = q_ref[:, a * GDN_DK:(a + 1) * GDN_DK]
        k = k_ref[:, a * GDN_DK:(a + 1) * GDN_DK]
        kk = lax.dot_general(k, k, nt, preferred_element_type=F32)
        qk = lax.dot_general(q, k, nt, preferred_element_type=F32)
        for b in range(2):
            vh = (hb * hq + a) * 2 + b
            beta = jnp.sum(jnp.where(lane == vh, bg, 0.0), axis=-1, keepdims=True)
            gc_col = jnp.sum(jnp.where(lane == vh + GDN_V_HEADS, gall, 0.0), axis=-1, keepdims=True)
            gc_row = gtc_ref[pl.ds(vh, 1), :]
            dec = jnp.exp(jnp.where(incl, gc_col - gc_row, NEG_INF))
            amats.append(jnp.where(strict, kk * dec, 0.0) * beta)
            attns.append((qk * dec).astype(BF16))
            qs.append(q)
            ks.append(k)
            betas.append(beta)
            gcs.append(gc_col)
            glasts.append(gc_col[c - 1:c, :])
    eye = jnp.where(row == col, 1.0, 0.0).astype(BF16)
    abs_ = [amats[i].astype(BF16) for i in range(nb)]
    ts = [eye - abs_[i] * mask_ref[0] for i in range(nb)]
    for l in range(1, nlev):
        s = 1 << l
        if s < 16:
            xs = [jnp.dot(abs_[i] * mask_ref[l], ts[i], preferred_element_type=F32).astype(BF16)
                  for i in range(nb)]
            ts = [ts[i] - jnp.dot(ts[i], xs[i], preferred_element_type=F32).astype(BF16) for i in range(nb)]
            continue
        odd = [slice((2 * k + 1) * s, (2 * k + 2) * s) for k in range(c // (2 * s))]
        even = [slice(2 * k * s, (2 * k + 1) * s) for k in range(c // (2 * s))]
        pick = lambda a, sls: jnp.concatenate([a[sl] for sl in sls], axis=0) if len(sls) > 1 else a[sls[0]]
        zero = jnp.zeros((s, c), BF16)
        new_ts = []
        m_odd = jnp.concatenate([mask_ref[l, sl, :] for sl in odd], axis=0) if len(odd) > 1 else mask_ref[l, odd[0], :]
        e_odd = [pick(abs_[i], odd) * m_odd for i in range(nb)]
        x_odd = [jnp.dot(e_odd[i], ts[i], preferred_element_type=F32).astype(BF16) for i in range(nb)]
        for i in range(nb):
            x_full = jnp.concatenate([p for k in range(len(odd)) for p in (zero, x_odd[i][k * s:(k + 1) * s])],
                                     axis=0)
            t_odd = pick(ts[i], odd)
            t_odd = t_odd - jnp.dot(t_odd, x_full, preferred_element_type=F32).astype(BF16)
            new_ts.append(jnp.concatenate(
                [p for k in range(len(odd)) for p in (ts[i][even[k]], t_odd[k * s:(k + 1) * s])], axis=0))
        ts = new_ts
    rs = [ts[i] - eye for i in range(nb)]
    egs = [jnp.exp(gcs[i]) for i in range(nb)]
    kfs = [ks[i].astype(F32) for i in range(nb)]
    rhss = [jnp.concatenate([v_ref[:, i * GDN_DV:(i + 1) * GDN_DV].astype(F32) * betas[i],
                             kfs[i] * (betas[i] * egs[i])], axis=1) for i in range(nb)]
    uws = [rhss[i] + jnp.dot(rs[i], rhss[i].astype(BF16), preferred_element_type=F32)
           for i in range(nb)]
    sts = [state_ref[i] for i in range(nb)]
    wss = [jnp.dot(jnp.concatenate([uws[i][:, GDN_DV:].astype(BF16),
                                    (qs[i].astype(F32) * egs[i]).astype(BF16)], axis=0),
                   sts[i].astype(BF16), preferred_element_type=F32) for i in range(nb)]
    vnbs = [(uws[i][:, :GDN_DV] - wss[i][:c]).astype(BF16) for i in range(nb)]
    os_ = [wss[i][c:] + jnp.dot(attns[i], vnbs[i], preferred_element_type=F32) for i in range(nb)]
    for i in range(nb):
        kdec = (kfs[i] * jnp.exp(glasts[i] - gcs[i])).astype(BF16)
        state_ref[i] = sts[i] * jnp.exp(glasts[i]) + lax.dot_general(
            kdec, vnbs[i], (((0,), (0,)), ((), ())), preferred_element_type=F32)
        z = z_ref[:, i * GDN_DV:(i + 1) * GDN_DV].astype(F32)
        o_ref[:, i * GDN_DV:(i + 1) * GDN_DV] = (_rms(os_[i], on_ref[...]) * _silu(z)).astype(o_ref.dtype)


def gdn_delta(qkvz, bg, gt, o_norm, *, hq=4, c=GDN_SUPER):
    s = qkvz.shape[0]
    qk_dim = GDN_QK_HEADS * GDN_DK
    v_dim = GDN_V_HEADS * GDN_DV
    wq = hq * GDN_DK
    wv = 2 * hq * GDN_DV
    nlev = c.bit_length() - 1
    return pl.pallas_call(
        functools.partial(_gdn_delta_kernel, hq=hq, c=c),
        out_shape=jax.ShapeDtypeStruct((s, v_dim), BF16),
        grid=(GDN_QK_HEADS // hq, s // c),
        in_specs=[
            pl.BlockSpec((c, wq), lambda h, t: (t, h)),
            pl.BlockSpec((c, wq), lambda h, t: (t, qk_dim // wq + h)),
            pl.BlockSpec((c, wv), lambda h, t: (t, 2 * qk_dim // wv + h)),
            pl.BlockSpec((c, wv), lambda h, t: (t, (2 * qk_dim + v_dim) // wv + h)),
            pl.BlockSpec((c, LANES), lambda h, t: (t, 0)),
            pl.BlockSpec((GDN_V_HEADS, c), lambda h, t: (0, t)),
            pl.BlockSpec((1, GDN_DV), lambda h, t: (0, 0)),
        ],
        out_specs=pl.BlockSpec((c, wv), lambda h, t: (t, h)),
        scratch_shapes=[pltpu.VMEM((2 * hq, GDN_DK, GDN_DV), F32), pltpu.VMEM((nlev, c, c), BF16),
                        pltpu.VMEM((c, c), BF16), pltpu.VMEM((GDN_V_HEADS, c), F32)],
        compiler_params=_params("arbitrary", "arbitrary"),
        name="gdn_delta",
    )(qkvz, qkvz, qkvz, qkvz, bg, gt, o_norm.reshape(1, GDN_DV))


def gdn_layer(x, gain, w_in, conv_w, a_log, dt_bias, o_norm, w_o):
    s, d = x.shape
    qk_dim = GDN_QK_HEADS * GDN_DK
    v_dim = GDN_V_HEADS * GDN_DV
    n_main = 2 * qk_dim + 2 * v_dim
    nh = GDN_V_HEADS
    w_b = w_in.astype(BF16)
    w_ba = jnp.pad(w_in[:, n_main:], ((0, 0), (0, LANES - 2 * nh))).astype(BF16)
    qkvz, bg, gt = gdn_proj(x, gain, w_b, conv_w, w_ba, a_log, dt_bias)
    o = gdn_delta(qkvz, bg, gt, o_norm)
    return resident_matmul(o, w_o.astype(BF16), d, res=x, name="gdn_out")


def _s5_disc_kernel(lre_ref, lim_ref, ldt_ref, lre_e_ref, lim_e_ref, bre_ref, bim_ref,
                    lbre_ref, lbim_ref, bbre_ref, bbim_ref):
    dt = jnp.exp(ldt_ref[...])

    def zoh(lre, lim):
        mag = jnp.exp(lre * dt)
        ang = lim * dt
        lb_re = mag * jnp.cos(ang)
        lb_im = mag * jnp.sin(ang)
        den = lre * lre + lim * lim
        nr = lb_re - 1.0
        f_re = (nr * lre + lb_im * lim) / den
        f_im = (lb_im * lre - nr * lim) / den
        return lb_re, lb_im, f_re, f_im

    lb_re, lb_im, _, _ = zoh(lre_ref[...], lim_ref[...])
    lbre_ref[...] = lb_re
    lbim_ref[...] = lb_im
    _, _, f_re, f_im = zoh(lre_e_ref[...], lim_e_ref[...])
    bbre_ref[...] = f_re * bre_ref[...] - f_im * bim_ref[...]
    bbim_ref[...] = f_re * bim_ref[...] + f_im * bre_ref[...]


def _s5_scan_kernel(u_ref, bblk_ref, cblk_ref, are_ref, aim_ref, d_ref, o_ref, st_ref, bu_ref, x_ref,
                    *, lt, n_sq):
    p = pl.program_id(1)
    tb = pl.program_id(2)
    last = pl.num_programs(2) - 1
    nseg = S5_SEGMENTS
    half = bu_ref.shape[1] // 2

    @pl.when((p == 0) & (tb == 0))
    def _():
        st_ref[...] = jnp.zeros(st_ref.shape, F32)

    u = u_ref[...]
    bu_ref[...] = jnp.dot(u.astype(BF16), bblk_ref[0], preferred_element_type=F32)
    ar = jnp.broadcast_to(are_ref[0], (nseg, half))
    ai = jnp.broadcast_to(aim_ref[0], (nseg, half))

    def body(tau, carry):
        xr, xi = carry
        r0 = pl.multiple_of(tau * nseg, nseg)
        b = bu_ref[pl.ds(r0, nseg), :]
        nxr = ar * xr - ai * xi + b[:, :half]
        nxi = ar * xi + ai * xr + b[:, half:]
        x_ref[pl.ds(r0, nseg), :half] = nxr
        x_ref[pl.ds(r0, nseg), half:] = nxi
        return nxr, nxi

    st = st_ref[...]
    xr, xi = lax.fori_loop(0, lt, body, (st[:, :half], st[:, half:]), unroll=8)
    st_ref[:, :half] = xr
    st_ref[:, half:] = xi

    @pl.when((p == 0) & (tb == last))
    def _():
        pr, pi = are_ref[0], aim_ref[0]
        for _ in range(n_sq):
            pr, pi = pr * pr - pi * pi, 2.0 * pr * pi
        sr = jnp.zeros((1, half), F32)
        si = jnp.zeros((1, half), F32)
        rows_r, rows_i = [sr], [si]
        for r in range(nseg - 1):
            sr, si = pr * sr - pi * si + xr[r:r + 1, :], pr * si + pi * sr + xi[r:r + 1, :]
            rows_r.append(sr)
            rows_i.append(si)
        st_ref[:, :half] = jnp.concatenate(rows_r, axis=0)
        st_ref[:, half:] = jnp.concatenate(rows_i, axis=0)

    @pl.when(p == 1)
    def _():
        y = jnp.dot(x_ref[...].astype(BF16), cblk_ref[0], preferred_element_type=F32) + d_ref[...] * u
        o_ref[...] = jax.nn.gelu(y).astype(o_ref.dtype)


def s5_layer(x, gain, w_in, lam_re, lam_im, log_dt, b_re, b_im, c_re, c_im, d_skip, w_out):
    s, d = x.shape
    width = w_in.shape[1]
    g, p_st, gc = b_re.shape
    nseg = S5_SEGMENTS
    sseg = s // nseg
    assert sseg & (sseg - 1) == 0
    tm = min(512, sseg)
    nb = sseg // tm
    ncb = width // LANES
    gpb = LANES // gc
    half = gpb * p_st

    u = resident_matmul(x, w_in.astype(BF16), width, gain=gain, tm=tm,
                        out_map=lambda i: (i % nb, i // nb), out_shape=(sseg, nseg * width), name="s5_in")
    u2 = u.reshape(s, width)

    rep = lambda a: jnp.repeat(a, gc, axis=1)
    vm = pl.BlockSpec(memory_space=pltpu.VMEM)
    lb_re, lb_im, bb_re, bb_im = pl.pallas_call(
        _s5_disc_kernel,
        out_shape=(jax.ShapeDtypeStruct((g, p_st), F32), jax.ShapeDtypeStruct((g, p_st), F32),
                   jax.ShapeDtypeStruct((g, p_st * gc), F32), jax.ShapeDtypeStruct((g, p_st * gc), F32)),
        in_specs=[vm] * 7, out_specs=(vm, vm, vm, vm), name="s5_disc",
    )(lam_re, lam_im, log_dt.reshape(g, 1), rep(lam_re), rep(lam_im),
      b_re.reshape(g, p_st * gc), b_im.reshape(g, p_st * gc))

    eye = jnp.eye(gpb, dtype=F32)

    def in_blocks(bb):
        t = bb.reshape(ncb, gpb, p_st, gc)
        return jnp.einsum("bgpc,gh->bgchp", t, eye).reshape(ncb, gpb * gc, gpb * p_st)

    def out_blocks(cc):
        t = cc.reshape(ncb, gpb, gc, p_st)
        return jnp.einsum("bgcp,gh->bhpgc", t, eye).reshape(ncb, gpb * p_st, gpb * gc)

    bblk = jnp.concatenate([in_blocks(bb_re), in_blocks(bb_im)], axis=2).astype(BF16)
    cblk = jnp.concatenate([out_blocks(c_re), -out_blocks(c_im)], axis=1).astype(BF16)
    a_re = lb_re.reshape(ncb, 1, half)
    a_im = lb_im.reshape(ncb, 1, half)

    lt = min(128, sseg)
    rows = lt * nseg
    yg = pl.pallas_call(
        functools.partial(_s5_scan_kernel, lt=lt, n_sq=sseg.bit_length() - 1),
        out_shape=jax.ShapeDtypeStruct((s, width), BF16),
        grid=(ncb, 2, sseg // lt),
        in_specs=[
            pl.BlockSpec((rows, LANES), lambda cb, p, t: (t, cb)),
            pl.BlockSpec((1, LANES, 2 * half), lambda cb, p, t: (cb, 0, 0)),
            pl.BlockSpec((1, 2 * half, LANES), lambda cb, p, t: (cb, 0, 0)),
            pl.BlockSpec((1, 1, half), lambda cb, p, t: (cb, 0, 0)),
            pl.BlockSpec((1, 1, half), lambda cb, p, t: (cb, 0, 0)),
            pl.BlockSpec((1, LANES), lambda cb, p, t: (0, cb)),
        ],
        out_specs=pl.BlockSpec((rows, LANES), lambda cb, p, t: (t * p, cb)),
        scratch_shapes=[pltpu.VMEM((nseg, 2 * half), F32), pltpu.VMEM((rows, 2 * half), F32),
                        pltpu.VMEM((rows, 2 * half), F32)],
        compiler_params=_params("arbitrary", "arbitrary", "arbitrary"),
        name="s5_scan",
    )(u2, bblk, cblk, a_re, a_im, d_skip.reshape(1, width))

    return resident_matmul(yg.reshape(sseg, nseg * width), w_out.astype(BF16), d, w_offs=(0, d), act="glu",
                           res=x, tm=tm, a_block=(tm, width), a_map=lambda i: (i % nb, i // nb), name="s5_out")


def _conv_kernel(xg_ref, w_ref, b_ref, lg_ref, lb_ref, wo_ref, res_ref, o_ref, stage_ref, y_ref, a_ref,
                 *, tm, halo, tn):
    i = pl.program_id(0)
    ns = stage_ref.shape[0]
    d = ns * LANES
    for c in range(ns):
        prev = stage_ref[c, tm:tm + halo, :]
        stage_ref[c, 0:halo, :] = jnp.where(i == 0, 0.0, prev)
        stage_ref[c, halo:, :] = xg_ref[:, c * LANES:(c + 1) * LANES]

    def strip(c, carry):
        w = w_ref[c]
        acc = stage_ref[c, pl.ds(halo - (CONV_K - 1), tm), :] * w[0:1, :]
        for k in range(1, CONV_K):
            acc = acc + stage_ref[c, pl.ds(halo - (CONV_K - 1) + k, tm), :] * w[k:k + 1, :]
        y_ref[c] = acc + b_ref[c]
        return carry

    lax.fori_loop(0, ns, strip, 0)

    tot = y_ref[0]
    for c in range(1, ns):
        tot = tot + y_ref[c]
    mu = jnp.sum(tot, axis=-1, keepdims=True) * (1.0 / d)
    sq = jnp.square(y_ref[0] - mu)
    for c in range(1, ns):
        sq = sq + jnp.square(y_ref[c] - mu)
    rstd = lax.rsqrt(jnp.sum(sq, axis=-1, keepdims=True) * (1.0 / d) + EPS)
    for c in range(ns):
        yn = (y_ref[c] - mu) * rstd * lg_ref[c] + lb_ref[c]
        a_ref[:, c * LANES:(c + 1) * LANES] = _silu(yn).astype(BF16)
    a = a_ref[...]
    for jn in range(d // tn):
        sl = slice(jn * tn, (jn + 1) * tn)
        o_ref[:, sl] = res_ref[:, sl] + jnp.dot(a, wo_ref[:, sl], preferred_element_type=F32)


def conv_layer(x, gain, w_in, dw_w, dw_b, ln_g, ln_b, w_out, *, tm=256):
    s, d = x.shape
    ch = dw_w.shape[1]
    ns = ch // LANES
    halo = 32
    xg = resident_matmul(x, w_in.astype(BF16), ch, w_offs=(0, ch), gain=gain, act="glu", name="conv_in")
    strips = lambda a: a.reshape(-1, ns, LANES).transpose(1, 0, 2)
    w_s = strips(jnp.pad(dw_w, ((0, halo - CONV_K), (0, 0))))
    const2 = lambda i: (0, 0)
    const3 = lambda i: (0, 0, 0)
    row = lambda i: (i, 0)
    return pl.pallas_call(
        functools.partial(_conv_kernel, tm=tm, halo=halo, tn=512),
        out_shape=jax.ShapeDtypeStruct((s, d), F32),
        grid=(s // tm,),
        in_specs=[
            pl.BlockSpec((tm, ch), row),
            pl.BlockSpec((ns, halo, LANES), const3),
            pl.BlockSpec((ns, 1, LANES), const3),
            pl.BlockSpec((ns, 1, LANES), const3),
            pl.BlockSpec((ns, 1, LANES), const3),
            pl.BlockSpec((ch, d), const2, pipeline_mode=pl.Buffered(1)),
            pl.BlockSpec((tm, d), row),
        ],
        out_specs=pl.BlockSpec((tm, d), row),
        scratch_shapes=[pltpu.VMEM((ns, tm + halo, LANES), F32), pltpu.VMEM((ns, tm, LANES), F32),
                        pltpu.VMEM((tm, ch), BF16)],
        compiler_params=_params("arbitrary"),
        name="conv_mod",
    )(xg, w_s, strips(dw_b.reshape(1, ch)), strips(ln_g.reshape(1, ch)), strips(ln_b.reshape(1, ch)),
      w_out.astype(BF16), x)


def kernel(x, positions, norm_mix, norm_ffn, final_norm, mla_w_in, mla_q_norm, mla_w_uq, mla_kv_norm,
           mla_w_ukv, mla_w_o, gdn_w_in, gdn_conv_w, gdn_a_log, gdn_dt_bias, gdn_o_norm, gdn_w_o, s5_w_in,
           s5_lam_re, s5_lam_im, s5_log_dt, s5_b_re, s5_b_im, s5_c_re, s5_c_im, s5_d, s5_w_out, cv_w_in,
           cv_dw_w, cv_dw_b, cv_ln_g, cv_ln_b, cv_w_out, ffn_w_gate, ffn_w_up, ffn_w_down):
    bsz, s, d = x.shape
    depth = norm_mix.shape[0]
    wg_b, wu_b, wd_b = ffn_w_gate.astype(BF16), ffn_w_up.astype(BF16), ffn_w_down.astype(BF16)
    outs = []
    for b in range(bsz):
        h = x[b]
        pos = positions[b]
        for i in range(depth):
            m, j = i % 4, i // 4
            if m == 0:
                h = mla_layer(h, pos, norm_mix[i], mla_w_in[j], mla_q_norm[j], mla_w_uq[j], mla_kv_norm[j],
                              mla_w_ukv[j], mla_w_o[j])
            elif m == 1:
                h = gdn_layer(h, norm_mix[i], gdn_w_in[j], gdn_conv_w[j], gdn_a_log[j], gdn_dt_bias[j],
                              gdn_o_norm[j], gdn_w_o[j])
            elif m == 2:
                h = s5_layer(h, norm_mix[i], s5_w_in[j], s5_lam_re[j], s5_lam_im[j], s5_log_dt[j], s5_b_re[j],
                             s5_b_im[j], s5_c_re[j], s5_c_im[j], s5_d[j], s5_w_out[j])
            else:
                h = conv_layer(h, norm_mix[i], cv_w_in[j], cv_dw_w[j], cv_dw_b[j], cv_ln_g[j], cv_ln_b[j],
                               cv_w_out[j])
            fg = final_norm if i == depth - 1 else None
            h = ffn(h, norm_ffn[i], wg_b, wu_b, wd_b, fg, layer=i)
        outs.append(h)
    return outs[0][None] if bsz == 1 else jnp.stack(outs)
```

```python
import functools
import math

import jax
import jax.numpy as jnp
from jax import lax
from jax.experimental import pallas as pl
from jax.experimental.pallas import tpu as pltpu

F32 = jnp.float32
BF16 = jnp.bfloat16
EPS = 1e-6
NEG_INF = -1e30

VMEM_LIMIT_BYTES = 56 * 1024 * 1024
LANES = 128

MLA_HEADS = 16
MLA_Q_RANK = 512
MLA_KV_RANK = 512
MLA_NOPE = 128
MLA_ROPE = 64
MLA_V = 128
MLA_HEAD_PAD = 256
ROPE_THETA = 10000.0
MLA_QSCALE = (MLA_NOPE + MLA_ROPE) ** -0.5 * math.log2(math.e)

GDN_QK_HEADS = 16
GDN_V_HEADS = 32
GDN_DK = 128
GDN_DV = 128
GDN_CONV = 4
GDN_CHUNK = 64
GDN_SUPER = 256

S5_GROUP = 16
S5_STATE = 64
S5_SEGMENTS = 8

CONV_K = 31


def _params(*sem):
    return pltpu.CompilerParams(dimension_semantics=sem, vmem_limit_bytes=VMEM_LIMIT_BYTES)


def _rms(x, g):
    return x * lax.rsqrt(jnp.mean(x * x, axis=-1, keepdims=True) + EPS) * g


def _silu(x):
    return x * jax.nn.sigmoid(x)


def _mm_kernel(*refs, has_norm, n_w, act, has_res):
    it = iter(refs)
    a_ref = next(it)
    g_ref = next(it) if has_norm else None
    w_refs = [next(it) for _ in range(n_w)]
    res_ref = next(it) if has_res else None
    o_ref = next(it)
    xn_ref = next(it) if has_norm else None

    if has_norm:
        @pl.when(pl.program_id(1) == 0)
        def _():
            xn_ref[...] = _rms(a_ref[...], g_ref[...]).astype(BF16)

        a = xn_ref[...]
    else:
        a = a_ref[...]
    accs = [jnp.dot(a, w[...], preferred_element_type=F32) for w in w_refs]
    if act == "glu":
        y = accs[0] * jax.nn.sigmoid(accs[1])
    else:
        y = accs[0]
    if has_res:
        y = res_ref[...] + y
    o_ref[...] = y.astype(o_ref.dtype)


def fused_matmul(a, ws, *, gain=None, act="none", res=None, out_dtype=F32, tm=512, tn=512,
                 w_col_offsets=None, out_map=None, out_shape=None, a_block=None, a_map=None, name="mm"):
    k = ws[0][0].shape[0]
    m = a.size // k
    n = ws[0][1]
    n_w = len(ws)
    tm = min(tm, m)
    tn = min(tn, n)
    assert m % tm == 0 and n % tn == 0
    has_norm = gain is not None
    has_res = res is not None
    if w_col_offsets is None:
        w_col_offsets = [0] * n_w
    if a_block is None:
        a_block, a_map = (tm, k), (lambda i, j: (i, 0))
    in_specs = [pl.BlockSpec(a_block, a_map)]
    args = [a]
    if has_norm:
        in_specs.append(pl.BlockSpec((1, k), lambda i, j: (0, 0)))
        args.append(gain.reshape(1, k))
    for (w, _), off in zip(ws, w_col_offsets):
        assert off % tn == 0
        ob = off // tn
        in_specs.append(pl.BlockSpec((k, tn), lambda i, j, ob=ob: (0, j + ob)))
        args.append(w)
    if has_res:
        in_specs.append(pl.BlockSpec((tm, tn), lambda i, j: (i, j)))
        args.append(res)
    if out_map is None:
        out_map = lambda i, j: (i, j)
    if out_shape is None:
        out_shape = (m, n)
    scratch = [pltpu.VMEM((tm, k), BF16)] if has_norm else []
    return pl.pallas_call(
        functools.partial(_mm_kernel, has_norm=has_norm, n_w=n_w, act=act, has_res=has_res),
        out_shape=jax.ShapeDtypeStruct(out_shape, out_dtype),
        grid=(m // tm, n // tn),
        in_specs=in_specs,
        out_specs=pl.BlockSpec((tm, tn), out_map),
        scratch_shapes=scratch,
        compiler_params=_params("arbitrary", "arbitrary"),
        name=name,
    )(*args)


def _mm_resident_kernel(*refs, has_norm, w_offs, act, has_res, n, tn):
    it = iter(refs)
    a_ref = next(it)
    g_ref = next(it) if has_norm else None
    w_ref = next(it)
    res_ref = next(it) if has_res else None
    o_ref = next(it)
    a = _rms(a_ref[...], g_ref[...]).astype(BF16) if has_norm else a_ref[...]
    for jn in range(n // tn):
        sl = slice(jn * tn, (jn + 1) * tn)
        accs = [jnp.dot(a, w_ref[:, off + jn * tn:off + (jn + 1) * tn], preferred_element_type=F32)
                for off in w_offs]
        y = accs[0] * jax.nn.sigmoid(accs[1]) if act == "glu" else accs[0]
        if has_res:
            y = res_ref[:, sl] + y
        o_ref[:, sl] = y.astype(o_ref.dtype)


def resident_matmul(a, w, n, *, w_offs=(0,), gain=None, act="none", res=None, out_dtype=F32, tm=512, tn=512,
                    a_block=None, a_map=None, out_map=None, out_shape=None, name="mm"):
    k = w.shape[0]
    m = a.size // k
    tm = min(tm, m)
    tn = min(tn, n)
    assert m % tm == 0 and n % tn == 0
    has_norm = gain is not None
    has_res = res is not None
    row = lambda i: (i, 0)
    const = lambda i: (0, 0)
    in_specs = [pl.BlockSpec(a_block or (tm, k), a_map or row)]
    args = [a]
    if has_norm:
        in_specs.append(pl.BlockSpec((1, k), const))
        args.append(gain.reshape(1, k))
    in_specs.append(pl.BlockSpec(w.shape, const, pipeline_mode=pl.Buffered(1)))
    args.append(w)
    if has_res:
        in_specs.append(pl.BlockSpec((tm, n), row))
        args.append(res)
    return pl.pallas_call(
        functools.partial(_mm_resident_kernel, has_norm=has_norm, w_offs=tuple(w_offs), act=act,
                          has_res=has_res, n=n, tn=tn),
        out_shape=jax.ShapeDtypeStruct(out_shape or (m, n), out_dtype),
        grid=(m // tm,),
        in_specs=in_specs,
        out_specs=pl.BlockSpec((tm, n), out_map or row),
        compiler_params=_params("arbitrary"),
        name=name,
    )(*args)


def _ffn_kernel(*refs, final_norm):
    if final_norm:
        x_ref, g_ref, wg_ref, wu_ref, wd_ref, fg_ref, o_ref, xn_ref = refs
    else:
        x_ref, g_ref, wg_ref, wu_ref, wd_ref, o_ref, xn_ref = refs
        fg_ref = None
    j = pl.program_id(1)

    @pl.when(j == 0)
    def _():
        x = x_ref[...]
        xn_ref[...] = _rms(x, g_ref[...]).astype(BF16)
        o_ref[...] = x

    xn = xn_ref[...]
    hg = jnp.dot(xn, wg_ref[...].astype(BF16), preferred_element_type=F32)
    hu = jnp.dot(xn, wu_ref[...].astype(BF16), preferred_element_type=F32)
    h = (_silu(hg) * hu).astype(BF16)
    o_ref[...] += jnp.dot(h, wd_ref[...].astype(BF16), preferred_element_type=F32)

    if final_norm:
        @pl.when(j == pl.num_programs(1) - 1)
        def _():
            o_ref[...] = _rms(o_ref[...], fg_ref[...])


def ffn(x, gain, wg, wu, wd, final_gain=None, *, layer=0, tm=1024, th=256):
    m, d = x.shape
    hdim = wg.shape[2]
    assert m % tm == 0 and hdim % th == 0
    final_norm = final_gain is not None
    in_specs = [
        pl.BlockSpec((tm, d), lambda i, j: (i, 0)),
        pl.BlockSpec((1, d), lambda i, j: (0, 0)),
        pl.BlockSpec((None, d, th), lambda i, j: (layer, 0, j)),
        pl.BlockSpec((None, d, th), lambda i, j: (layer, 0, j)),
        pl.BlockSpec((None, th, d), lambda i, j: (layer, j, 0)),
    ]
    args = [x, gain.reshape(1, d), wg, wu, wd]
    if final_norm:
        in_specs.append(pl.BlockSpec((1, d), lambda i, j: (0, 0)))
        args.append(final_gain.reshape(1, d))
    return pl.pallas_call(
        functools.partial(_ffn_kernel, final_norm=final_norm),
        out_shape=jax.ShapeDtypeStruct((m, d), F32),
        grid=(m // tm, hdim // th),
        in_specs=in_specs,
        out_specs=pl.BlockSpec((tm, d), lambda i, j: (i, 0)),
        scratch_shapes=[pltpu.VMEM((tm, d), BF16)],
        compiler_params=_params("arbitrary", "arbitrary"),
        name="ffn",
    )(*args)


def _rope_block(xb, c, s1, s2):
    return xb * c + pltpu.roll(xb, 96, 1) * s1 + pltpu.roll(xb, 32, 1) * s2


def _mla_proj_kernel(x_ref, g_ref, pos_ref, invf_ref, win_ref, qn_ref, wuq_ref, kvn_ref, wukv_ref,
                     q_ref, k_ref, v_ref):
    xn = _rms(x_ref[...], g_ref[...]).astype(BF16)
    proj = jnp.dot(xn, win_ref[...], preferred_element_type=F32)
    qn = _rms(proj[:, :MLA_Q_RANK], qn_ref[...]).astype(BF16)
    kvn = _rms(proj[:, MLA_Q_RANK:MLA_Q_RANK + MLA_KV_RANK], kvn_ref[...]).astype(BF16)
    kr = proj[:, MLA_Q_RANK + MLA_KV_RANK:]

    ang = pos_ref[...].astype(F32) * invf_ref[...]
    cs = jnp.cos(ang)
    sn = jnp.sin(ang)
    lane = lax.broadcasted_iota(jnp.int32, ang.shape, 1)
    half = MLA_ROPE // 2
    c = jnp.where(lane < MLA_ROPE, cs, 0.0)
    s1 = jnp.where(lane < half, -sn, 0.0)
    s2 = jnp.where((lane >= half) & (lane < MLA_ROPE), sn, 0.0)

    k_rope = _rope_block(kr, c, s1, s2).astype(BF16)
    hp = MLA_HEAD_PAD
    for h in range(MLA_HEADS):
        qh = jnp.dot(qn, wuq_ref[:, h * hp:(h + 1) * hp], preferred_element_type=F32) * MLA_QSCALE
        q_ref[:, h * hp:h * hp + LANES] = qh[:, :LANES].astype(BF16)
        q_ref[:, h * hp + LANES:(h + 1) * hp] = _rope_block(qh[:, LANES:], c, s1, s2).astype(BF16)
        kvh = jnp.dot(kvn, wukv_ref[:, h * hp:(h + 1) * hp], preferred_element_type=F32)
        k_ref[:, h * hp:h * hp + LANES] = kvh[:, :LANES].astype(BF16)
        k_ref[:, h * hp + LANES:(h + 1) * hp] = k_rope
        v_ref[:, h * MLA_V:(h + 1) * MLA_V] = kvh[:, LANES:].astype(BF16)


def mla_proj(x, gain, pos, invf, w_in_p, q_norm, w_uq_p, kv_norm, w_ukv, *, tm=512):
    s, d = x.shape
    nq = MLA_HEADS * MLA_HEAD_PAD
    const = lambda i: (0, 0)
    return pl.pallas_call(
        _mla_proj_kernel,
        out_shape=(jax.ShapeDtypeStruct((s, nq), BF16), jax.ShapeDtypeStruct((s, nq), BF16),
                   jax.ShapeDtypeStruct((s, MLA_HEADS * MLA_V), BF16)),
        grid=(s // tm,),
        in_specs=[
            pl.BlockSpec((tm, d), lambda i: (i, 0)),
            pl.BlockSpec((1, d), const),
            pl.BlockSpec((tm, 1), lambda i: (i, 0)),
            pl.BlockSpec((1, LANES), const),
            pl.BlockSpec(w_in_p.shape, const, pipeline_mode=pl.Buffered(1)),
            pl.BlockSpec((1, MLA_Q_RANK), const),
            pl.BlockSpec(w_uq_p.shape, const, pipeline_mode=pl.Buffered(1)),
            pl.BlockSpec((1, MLA_KV_RANK), const),
            pl.BlockSpec(w_ukv.shape, const, pipeline_mode=pl.Buffered(1)),
        ],
        out_specs=(pl.BlockSpec((tm, nq), lambda i: (i, 0)), pl.BlockSpec((tm, nq), lambda i: (i, 0)),
                   pl.BlockSpec((tm, MLA_HEADS * MLA_V), lambda i: (i, 0))),
        compiler_params=_params("arbitrary"),
        name="mla_proj",
    )(x, gain.reshape(1, d), pos, invf, w_in_p, q_norm.reshape(1, -1), w_uq_p, kv_norm.reshape(1, -1), w_ukv)


def _flash_kernel(q_ref, k_ref, v_ref, o_ref, m_ref, l_ref, acc_ref, *, tq, tk):
    qi = pl.program_id(1)
    m_ref[...] = jnp.full(m_ref.shape, NEG_INF, F32)
    l_ref[...] = jnp.zeros(l_ref.shape, F32)
    acc_ref[...] = jnp.zeros(acc_ref.shape, F32)

    def step(start, masked, r0=0):
        k = k_ref[pl.ds(start, tk), :]
        v = v_ref[pl.ds(start, tk), :]
        s = lax.dot_general(q_ref[r0:, :], k, (((1,), (1,)), ((), ())), preferred_element_type=F32)
        if masked:
            row = qi * tq + r0 + lax.broadcasted_iota(jnp.int32, s.shape, 0)
            col = start + lax.broadcasted_iota(jnp.int32, s.shape, 1)
            s = jnp.where(col <= row, s, NEG_INF)
        m_prev = m_ref[r0:, :]
        m_new = jnp.maximum(m_prev, jnp.max(s, axis=-1, keepdims=True))
        alpha = jnp.exp2(m_prev - m_new)
        l_new = alpha * l_ref[r0:, :]
        ps = []
        for c in range(tk // LANES):
            pc = jnp.exp2(s[:, c * LANES:(c + 1) * LANES] - m_new)
            l_new = l_new + pc
            ps.append(pc.astype(BF16))
        p = jnp.concatenate(ps, axis=1)
        acc_ref[r0:, :] = alpha * acc_ref[r0:, :] + jnp.dot(p, v, preferred_element_type=F32)
        l_ref[r0:, :] = l_new
        m_ref[r0:, :] = m_new

    nd = tq // tk

    def body(kp, carry):
        for u in range(nd):
            step(pl.multiple_of((kp * nd + u) * tk, tk), False)
        return carry

    lax.fori_loop(0, qi, body, 0)
    for dblk in range(nd):
        step(pl.multiple_of((qi * nd + dblk) * tk, tk), True, r0=dblk * tk)
    o_ref[...] = (acc_ref[...] / jnp.sum(l_ref[...], axis=-1, keepdims=True)).astype(o_ref.dtype)


def flash_attention(q, k, v, *, tq=2048, tk=512):
    s = q.shape[0]
    tq = min(tq, s)
    tk = min(tk, tq)
    hp, dv = MLA_HEAD_PAD, MLA_V
    return pl.pallas_call(
        functools.partial(_flash_kernel, tq=tq, tk=tk),
        out_shape=jax.ShapeDtypeStruct((s, MLA_HEADS * dv), BF16),
        grid=(MLA_HEADS, s // tq),
        in_specs=[
            pl.BlockSpec((tq, hp), lambda h, i: (i, h)),
            pl.BlockSpec((s, hp), lambda h, i: (0, h)),
            pl.BlockSpec((s, dv), lambda h, i: (0, h)),
        ],
        out_specs=pl.BlockSpec((tq, dv), lambda h, i: (i, h)),
        scratch_shapes=[pltpu.VMEM((tq, LANES), F32), pltpu.VMEM((tq, LANES), F32), pltpu.VMEM((tq, dv), F32)],
        compiler_params=_params("arbitrary", "arbitrary"),
        name="mla_flash",
    )(q, k, v)


def mla_layer(x, positions, gain, w_in, q_norm, w_uq, kv_norm, w_ukv, w_o):
    s, d = x.shape
    half = MLA_ROPE // 2
    inv = ROPE_THETA ** (-jnp.arange(half, dtype=F32) / half)
    invf = jnp.concatenate([inv, inv, jnp.zeros((LANES - MLA_ROPE,), F32)]).reshape(1, LANES)
    w_in_p = jnp.pad(w_in, ((0, 0), (0, LANES - MLA_ROPE))).astype(BF16)
    dq = MLA_NOPE + MLA_ROPE
    w_uq_p = jnp.pad(w_uq.reshape(MLA_Q_RANK, MLA_HEADS, dq), ((0, 0), (0, 0), (0, MLA_HEAD_PAD - dq)))
    w_uq_p = w_uq_p.reshape(MLA_Q_RANK, MLA_HEADS * MLA_HEAD_PAD).astype(BF16)
    q, k, v = mla_proj(x, gain, positions.reshape(s, 1), invf, w_in_p, q_norm, w_uq_p, kv_norm,
                       w_ukv.astype(BF16))
    o = flash_attention(q, k, v)
    return resident_matmul(o, w_o.astype(BF16), d, res=x, name="mla_out")


def _softplus(x):
    return jnp.maximum(x, 0.0) + jnp.log(1.0 + jnp.exp(-jnp.abs(x)))


def _gdn_proj_kernel(x_ref, g_ref, w_ref, cw_ref, wba_ref, alr_ref, dtr_ref,
                     o_ref, bg_ref, gt_ref, xn_ref, stage_ref, carry_ref, *, tm, tn, sub, n_qk, n_conv, qscale):
    i = pl.program_id(0)
    j = pl.program_id(1)
    nh = GDN_V_HEADS

    @pl.when(j == 0)
    def _():
        xn = _rms(x_ref[...], g_ref[...]).astype(BF16)
        xn_ref[...] = xn
        ba = jnp.dot(xn, wba_ref[...], preferred_element_type=F32)
        lane = lax.broadcasted_iota(jnp.int32, ba.shape, 1)
        gate = -jnp.exp(alr_ref[...]) * _softplus(ba + dtr_ref[...])
        bg_ref[...] = jnp.where(lane < nh, jax.nn.sigmoid(ba), gate)
        gt_ref[...] = gate.T[nh:2 * nh, :]

    nsub = tn // sub

    def proj(c):
        return jnp.dot(xn_ref[...], w_ref[:, c * sub:(c + 1) * sub], preferred_element_type=F32)

    def conv_silu(c):
        acc = proj(c)
        outs = []
        for hh in range(sub // LANES):
            k = c * (sub // LANES) + hh
            ks = slice(k * LANES, (k + 1) * LANES)
            a = acc[:, hh * LANES:(hh + 1) * LANES]
            stage_ref[k, 0:8, :] = jnp.where(i == 0, 0.0, carry_ref[j, :, ks])
            stage_ref[k, 8:, :] = a
            carry_ref[j, :, ks] = a[tm - 8:, :]
            cw = cw_ref[:, ks]
            y = stage_ref[k, pl.ds(8 - (GDN_CONV - 1), tm), :] * cw[0:1, :]
            for t in range(1, GDN_CONV):
                y = y + stage_ref[k, pl.ds(8 - (GDN_CONV - 1) + t, tm), :] * cw[t:t + 1, :]
            outs.append((ks, _silu(y)))
        return outs

    @pl.when(j < n_qk)
    def _():
        sc = jnp.where(j < n_qk // 2, qscale, 1.0)
        for c in range(nsub):
            for ks, yb in conv_silu(c):
                nrm = lax.rsqrt(jnp.sum(yb * yb, axis=-1, keepdims=True) + EPS) * sc
                o_ref[:, ks] = (yb * nrm).astype(o_ref.dtype)

    @pl.when((j >= n_qk) & (j < n_conv))
    def _():
        for c in range(nsub):
            for ks, yb in conv_silu(c):
                o_ref[:, ks] = yb.astype(o_ref.dtype)

    @pl.when(j >= n_conv)
    def _():
        for c in range(nsub):
            o_ref[:, c * sub:(c + 1) * sub] = proj(c).astype(o_ref.dtype)


def gdn_proj(x, gain, w_qkvz, conv_w_p, w_ba, a_log, dt_bias, *, tm=512, tn=2048, sub=256):
    s, d = x.shape
    qk_dim = GDN_QK_HEADS * GDN_DK
    v_dim = GDN_V_HEADS * GDN_DV
    n = 2 * qk_dim + 2 * v_dim
    assert n % tn == 0 and (2 * qk_dim) % tn == 0 and v_dim % tn == 0 and tn % sub == 0
    n_qk = 2 * qk_dim // tn
    n_conv = (2 * qk_dim + v_dim) // tn
    nh = GDN_V_HEADS
    pad = jnp.zeros((nh,), F32)
    alr = jnp.concatenate([pad, a_log, pad, pad]).reshape(1, LANES)
    dtr = jnp.concatenate([pad, dt_bias, pad, pad]).reshape(1, LANES)
    const = lambda i, j: (0, 0)
    kern = functools.partial(_gdn_proj_kernel, tm=tm, tn=tn, sub=sub, n_qk=n_qk, n_conv=n_conv,
                             qscale=GDN_DK ** -0.5)
    return pl.pallas_call(
        kern,
        out_shape=(jax.ShapeDtypeStruct((s, n), BF16), jax.ShapeDtypeStruct((s, LANES), F32),
                   jax.ShapeDtypeStruct((nh, s), F32)),
        grid=(s // tm, n // tn),
        in_specs=[
            pl.BlockSpec((tm, d), lambda i, j: (i, 0)),
            pl.BlockSpec((1, d), const),
            pl.BlockSpec((d, tn), lambda i, j: (0, j)),
            pl.BlockSpec((GDN_CONV, tn), lambda i, j: (0, jnp.minimum(j, n_conv - 1))),
            pl.BlockSpec((d, LANES), const),
            pl.BlockSpec((1, LANES), const),
            pl.BlockSpec((1, LANES), const),
        ],
        out_specs=(pl.BlockSpec((tm, tn), lambda i, j: (i, j)),
                   pl.BlockSpec((tm, LANES), lambda i, j: (i, 0)),
                   pl.BlockSpec((nh, tm), lambda i, j: (0, i))),
        scratch_shapes=[pltpu.VMEM((tm, d), BF16), pltpu.VMEM((tn // LANES, tm + 8, LANES), F32),
                        pltpu.VMEM((n_conv, 8, tn), F32)],
        compiler_params=_params("arbitrary", "arbitrary"),
        name="gdn_proj",
    )(x, gain.reshape(1, d), w_qkvz, conv_w_p, w_ba, alr, dtr)


def _gdn_delta_kernel(q_ref, k_ref, v_ref, z_ref, bg_ref, gt_ref, on_ref, o_ref, state_ref, mask_ref, tri_ref,
                      gtc_ref, *, hq, c):
    hb = pl.program_id(0)
    t = pl.program_id(1)
    nlev = c.bit_length() - 1
    row = lax.broadcasted_iota(jnp.int32, (c, c), 0)
    col = lax.broadcasted_iota(jnp.int32, (c, c), 1)

    @pl.when((hb == 0) & (t == 0))
    def _():
        for l in range(nlev):
            m = ((row >> (l + 1)) == (col >> (l + 1))) & (((row >> l) & 1) == 1) & (((col >> l) & 1) == 0)
            mask_ref[l] = jnp.where(m, 1.0, 0.0).astype(BF16)
        tri_ref[...] = jnp.where(col <= row, 1.0, 0.0).astype(BF16)

    @pl.when(t == 0)
    def _():
        state_ref[...] = jnp.zeros(state_ref.shape, F32)

    tri = tri_ref[...]
    incl = col <= row
    strict = col < row
    bg = bg_ref[...]
    lane = lax.broadcasted_iota(jnp.int32, bg.shape, 1)
    nt = (((1,), (1,)), ((), ()))

    def split3(x):
        x1 = x.astype(BF16)
        r1 = x - x1.astype(F32)
        x2 = r1.astype(BF16)
        return x1, x2, (r1 - x2.astype(F32)).astype(BF16)

    gall = sum(jnp.dot(tri, p, preferred_element_type=F32) for p in split3(bg))
    gtc_ref[...] = sum(lax.dot_general(p, tri, nt, preferred_element_type=F32) for p in split3(gt_ref[...]))

    nb = 2 * hq
    qs, ks, kfs, amats, attns, betas, gcs, glasts = [], [], [], [], [], [], [], []
    for a in range(hq):
        q = q_ref[:, a * GDN_DK:(a + 1) * GDN_DK]
        k = k_ref[:, a * GDN_DK:(a + 1) * GDN_DK]
        kk = lax.dot_general(k, k, nt, preferred_element_type=F32)
        qk = lax.dot_general(q, k, nt, preferred_element_type=F32)
        for b in range(2):
            vh = (hb * hq + a) * 2 + b
            beta = jnp.sum(jnp.where(lane == vh, bg, 0.0), axis=-1, keepdims=True)
            gc_col = jnp.sum(jnp.where(lane == vh + GDN_V_HEADS, gall, 0.0), axis=-1, keepdims=True)
            gc_row = gtc_ref[pl.ds(vh, 1), :]
            dec = jnp.exp(jnp.where(incl, gc_col - gc_row, NEG_INF))
            amats.append(jnp.where(strict, kk * dec, 0.0) * beta)
            attns.append((qk * dec).astype(BF16))
            qs.append(q)
            ks.append(k)
            betas.append(beta)
            gcs.append(gc_col)
            glasts.append(gc_col[c - 1:c, :])
    eye = jnp.where(row == col, 1.0, 0.0).astype(BF16)
    abs_ = [amats[i].astype(BF16) for i in range(nb)]
    ts = [eye - abs_[i] * mask_ref[0] for i in range(nb)]
    for l in range(1, nlev):
        s = 1 << l
        if s < 16:
            xs = [jnp.dot(abs_[i] * mask_ref[l], ts[i], preferred_element_type=F32).astype(BF16)
                  for i in range(nb)]
            ts = [ts[i] - jnp.dot(ts[i], xs[i], preferred_element_type=F32).astype(BF16) for i in range(nb)]
            continue
        odd = [slice((2 * k + 1) * s, (2 * k + 2) * s) for k in range(c // (2 * s))]
        even = [slice(2 * k * s, (2 * k + 1) * s) for k in range(c // (2 * s))]
        pick = lambda a, sls: jnp.concatenate([a[sl] for sl in sls], axis=0) if len(sls) > 1 else a[sls[0]]
        zero = jnp.zeros((s, c), BF16)
        new_ts = []
        m_odd = jnp.concatenate([mask_ref[l, sl, :] for sl in odd], axis=0) if len(odd) > 1 else mask_ref[l, odd[0], :]
        e_odd = [pick(abs_[i], odd) * m_odd for i in range(nb)]
        x_odd = [jnp.dot(e_odd[i], ts[i], preferred_element_type=F32).astype(BF16) for i in range(nb)]
        for i in range(nb):
            x_full = jnp.concatenate([p for k in range(len(odd)) for p in (zero, x_odd[i][k * s:(k + 1) * s])],
                                     axis=0)
            t_odd = pick(ts[i], odd)
            t_odd = t_odd - jnp.dot(t_odd, x_full, preferred_element_type=F32).astype(BF16)
            new_ts.append(jnp.concatenate(
                [p for k in range(len(odd)) for p in (ts[i][even[k]], t_odd[k * s:(k + 1) * s])], axis=0))
        ts = new_ts
    rs = [ts[i] - eye for i in range(nb)]
    egs = [jnp.exp(gcs[i]) for i in range(nb)]
    kfs = [ks[i].astype(F32) for i in range(nb)]
    rhss = [jnp.concatenate([v_ref[:, i * GDN_DV:(i + 1) * GDN_DV].astype(F32) * betas[i],
                             kfs[i] * (betas[i] * egs[i])], axis=1) for i in range(nb)]
    uws = [rhss[i] + jnp.dot(rs[i], rhss[i].astype(BF16), preferred_element_type=F32)
           for i in range(nb)]
    sts = [state_ref[i] for i in range(nb)]
    wss = [jnp.dot(jnp.concatenate([uws[i][:, GDN_DV:].astype(BF16),
                                    (qs[i].astype(F32) * egs[i]).astype(BF16)], axis=0),
                   sts[i].astype(BF16), preferred_element_type=F32) for i in range(nb)]
    vnbs = [(uws[i][:, :GDN_DV] - wss[i][:c]).astype(BF16) for i in range(nb)]
    os_ = [wss[i][c:] + jnp.dot(attns[i], vnbs[i], preferred_element_type=F32) for i in range(nb)]
    for i in range(nb):
        kdec = (kfs[i] * jnp.exp(glasts[i] - gcs[i])).astype(BF16)
        state_ref[i] = sts[i] * jnp.exp(glasts[i]) + lax.dot_general(
            kdec, vnbs[i], (((0,), (0,)), ((), ())), preferred_element_type=F32)
        z = z_ref[:, i * GDN_DV:(i + 1) * GDN_DV].astype(F32)
        o_ref[:, i * GDN_DV:(i + 1) * GDN_DV] = (_rms(os_[i], on_ref[...]) * _silu(z)).astype(o_ref.dtype)


def gdn_delta(qkvz, bg, gt, o_norm, *, hq=8, c=GDN_SUPER):
    s = qkvz.shape[0]
    qk_dim = GDN_QK_HEADS * GDN_DK
    v_dim = GDN_V_HEADS * GDN_DV
    wq = hq * GDN_DK
    wv = 2 * hq * GDN_DV
    nlev = c.bit_length() - 1
    return pl.pallas_call(
        functools.partial(_gdn_delta_kernel, hq=hq, c=c),
        out_shape=jax.ShapeDtypeStruct((s, v_dim), BF16),
        grid=(GDN_QK_HEADS // hq, s // c),
        in_specs=[
            pl.BlockSpec((c, wq), lambda h, t: (t, h)),
            pl.BlockSpec((c, wq), lambda h, t: (t, qk_dim // wq + h)),
            pl.BlockSpec((c, wv), lambda h, t: (t, 2 * qk_dim // wv + h)),
            pl.BlockSpec((c, wv), lambda h, t: (t, (2 * qk_dim + v_dim) // wv + h)),
            pl.BlockSpec((c, LANES), lambda h, t: (t, 0)),
            pl.BlockSpec((GDN_V_HEADS, c), lambda h, t: (0, t)),
            pl.BlockSpec((1, GDN_DV), lambda h, t: (0, 0)),
        ],
        out_specs=pl.BlockSpec((c, wv), lambda h, t: (t, h)),
        scratch_shapes=[pltpu.VMEM((2 * hq, GDN_DK, GDN_DV), F32), pltpu.VMEM((nlev, c, c), BF16),
                        pltpu.VMEM((c, c), BF16), pltpu.VMEM((GDN_V_HEADS, c), F32)],
        compiler_params=_params("arbitrary", "arbitrary"),
        name="gdn_delta",
    )(qkvz, qkvz, qkvz, qkvz, bg, gt, o_norm.reshape(1, GDN_DV))


def gdn_layer(x, gain, w_in, conv_w, a_log, dt_bias, o_norm, w_o):
    s, d = x.shape
    qk_dim = GDN_QK_HEADS * GDN_DK
    v_dim = GDN_V_HEADS * GDN_DV
    n_main = 2 * qk_dim + 2 * v_dim
    nh = GDN_V_HEADS
    w_b = w_in.astype(BF16)
    w_ba = jnp.pad(w_in[:, n_main:], ((0, 0), (0, LANES - 2 * nh))).astype(BF16)
    qkvz, bg, gt = gdn_proj(x, gain, w_b, conv_w, w_ba, a_log, dt_bias)
    o = gdn_delta(qkvz, bg, gt, o_norm)
    return resident_matmul(o, w_o.astype(BF16), d, res=x, name="gdn_out")


def _s5_disc_kernel(lre_ref, lim_ref, ldt_ref, lre_e_ref, lim_e_ref, bre_ref, bim_ref,
                    lbre_ref, lbim_ref, bbre_ref, bbim_ref):
    dt = jnp.exp(ldt_ref[...])

    def zoh(lre, lim):
        mag = jnp.exp(lre * dt)
        ang = lim * dt
        lb_re = mag * jnp.cos(ang)
        lb_im = mag * jnp.sin(ang)
        den = lre * lre + lim * lim
        nr = lb_re - 1.0
        f_re = (nr * lre + lb_im * lim) / den
        f_im = (lb_im * lre - nr * lim) / den
        return lb_re, lb_im, f_re, f_im

    lb_re, lb_im, _, _ = zoh(lre_ref[...], lim_ref[...])
    lbre_ref[...] = lb_re
    lbim_ref[...] = lb_im
    _, _, f_re, f_im = zoh(lre_e_ref[...], lim_e_ref[...])
    bbre_ref[...] = f_re * bre_ref[...] - f_im * bim_ref[...]
    bbim_ref[...] = f_re * bim_ref[...] + f_im * bre_ref[...]


def _s5_scan_kernel(u_ref, bblk_ref, cblk_ref, are_ref, aim_ref, d_ref, o_ref, st_ref, bu_ref, x_ref,
                    *, lt, n_sq):
    p = pl.program_id(1)
    tb = pl.program_id(2)
    last = pl.num_programs(2) - 1
    nseg = S5_SEGMENTS
    half = bu_ref.shape[1] // 2

    @pl.when((p == 0) & (tb == 0))
    def _():
        st_ref[...] = jnp.zeros(st_ref.shape, F32)

    u = u_ref[...]
    bu_ref[...] = jnp.dot(u.astype(BF16), bblk_ref[0], preferred_element_type=F32)
    ar = jnp.broadcast_to(are_ref[0], (nseg, half))
    ai = jnp.broadcast_to(aim_ref[0], (nseg, half))

    def body(tau, carry):
        xr, xi = carry
        r0 = pl.multiple_of(tau * nseg, nseg)
        b = bu_ref[pl.ds(r0, nseg), :]
        nxr = ar * xr - ai * xi + b[:, :half]
        nxi = ar * xi + ai * xr + b[:, half:]
        x_ref[pl.ds(r0, nseg), :half] = nxr
        x_ref[pl.ds(r0, nseg), half:] = nxi
        return nxr, nxi

    st = st_ref[...]
    xr, xi = lax.fori_loop(0, lt, body, (st[:, :half], st[:, half:]), unroll=8)
    st_ref[:, :half] = xr
    st_ref[:, half:] = xi

    @pl.when((p == 0) & (tb == last))
    def _():
        pr, pi = are_ref[0], aim_ref[0]
        for _ in range(n_sq):
            pr, pi = pr * pr - pi * pi, 2.0 * pr * pi
        sr = jnp.zeros((1, half), F32)
        si = jnp.zeros((1, half), F32)
        rows_r, rows_i = [sr], [si]
        for r in range(nseg - 1):
            sr, si = pr * sr - pi * si + xr[r:r + 1, :], pr * si + pi * sr + xi[r:r + 1, :]
            rows_r.append(sr)
            rows_i.append(si)
        st_ref[:, :half] = jnp.concatenate(rows_r, axis=0)
        st_ref[:, half:] = jnp.concatenate(rows_i, axis=0)

    @pl.when(p == 1)
    def _():
        y = jnp.dot(x_ref[...].astype(BF16), cblk_ref[0], preferred_element_type=F32) + d_ref[...] * u
        o_ref[...] = jax.nn.gelu(y).astype(o_ref.dtype)


def s5_layer(x, gain, w_in, lam_re, lam_im, log_dt, b_re, b_im, c_re, c_im, d_skip, w_out):
    s, d = x.shape
    width = w_in.shape[1]
    g, p_st, gc = b_re.shape
    nseg = S5_SEGMENTS
    sseg = s // nseg
    assert sseg & (sseg - 1) == 0
    tm = min(512, sseg)
    nb = sseg // tm
    ncb = width // LANES
    gpb = LANES // gc
    half = gpb * p_st

    u = resident_matmul(x, w_in.astype(BF16), width, gain=gain, tm=tm,
                        out_map=lambda i: (i % nb, i // nb), out_shape=(sseg, nseg * width), name="s5_in")
    u2 = u.reshape(s, width)

    rep = lambda a: jnp.repeat(a, gc, axis=1)
    vm = pl.BlockSpec(memory_space=pltpu.VMEM)
    lb_re, lb_im, bb_re, bb_im = pl.pallas_call(
        _s5_disc_kernel,
        out_shape=(jax.ShapeDtypeStruct((g, p_st), F32), jax.ShapeDtypeStruct((g, p_st), F32),
                   jax.ShapeDtypeStruct((g, p_st * gc), F32), jax.ShapeDtypeStruct((g, p_st * gc), F32)),
        in_specs=[vm] * 7, out_specs=(vm, vm, vm, vm), name="s5_disc",
    )(lam_re, lam_im, log_dt.reshape(g, 1), rep(lam_re), rep(lam_im),
      b_re.reshape(g, p_st * gc), b_im.reshape(g, p_st * gc))

    eye = jnp.eye(gpb, dtype=F32)

    def in_blocks(bb):
        t = bb.reshape(ncb, gpb, p_st, gc)
        return jnp.einsum("bgpc,gh->bgchp", t, eye).reshape(ncb, gpb * gc, gpb * p_st)

    def out_blocks(cc):
        t = cc.reshape(ncb, gpb, gc, p_st)
        return jnp.einsum("bgcp,gh->bhpgc", t, eye).reshape(ncb, gpb * p_st, gpb * gc)

    bblk = jnp.concatenate([in_blocks(bb_re), in_blocks(bb_im)], axis=2).astype(BF16)
    cblk = jnp.concatenate([out_blocks(c_re), -out_blocks(c_im)], axis=1).astype(BF16)
    a_re = lb_re.reshape(ncb, 1, half)
    a_im = lb_im.reshape(ncb, 1, half)

    lt = min(128, sseg)
    rows = lt * nseg
    yg = pl.pallas_call(
        functools.partial(_s5_scan_kernel, lt=lt, n_sq=sseg.bit_length() - 1),
        out_shape=jax.ShapeDtypeStruct((s, width), BF16),
        grid=(ncb, 2, sseg // lt),
        in_specs=[
            pl.BlockSpec((rows, LANES), lambda cb, p, t: (t, cb)),
            pl.BlockSpec((1, LANES, 2 * half), lambda cb, p, t: (cb, 0, 0)),
            pl.BlockSpec((1, 2 * half, LANES), lambda cb, p, t: (cb, 0, 0)),
            pl.BlockSpec((1, 1, half), lambda cb, p, t: (cb, 0, 0)),
            pl.BlockSpec((1, 1, half), lambda cb, p, t: (cb, 0, 0)),
            pl.BlockSpec((1, LANES), lambda cb, p, t: (0, cb)),
        ],
        out_specs=pl.BlockSpec((rows, LANES), lambda cb, p, t: (t * p, cb)),
        scratch_shapes=[pltpu.VMEM((nseg, 2 * half), F32), pltpu.VMEM((rows, 2 * half), F32),
                        pltpu.VMEM((rows, 2 * half), F32)],
        compiler_params=_params("arbitrary", "arbitrary", "arbitrary"),
        name="s5_scan",
    )(u2, bblk, cblk, a_re, a_im, d_skip.reshape(1, width))

    return resident_matmul(yg.reshape(sseg, nseg * width), w_out.astype(BF16), d, w_offs=(0, d), act="glu",
                           res=x, tm=tm, a_block=(tm, width), a_map=lambda i: (i % nb, i // nb), name="s5_out")


def _conv_kernel(xg_ref, w_ref, b_ref, lg_ref, lb_ref, wo_ref, res_ref, o_ref, stage_ref, y_ref, a_ref,
                 *, tm, halo, tn):
    i = pl.program_id(0)
    ns = stage_ref.shape[0]
    d = ns * LANES
    for c in range(ns):
        prev = stage_ref[c, tm:tm + halo, :]
        stage_ref[c, 0:halo, :] = jnp.where(i == 0, 0.0, prev)
        stage_ref[c, halo:, :] = xg_ref[:, c * LANES:(c + 1) * LANES]

    def strip(c, carry):
        w = w_ref[c]
        acc = stage_ref[c, pl.ds(halo - (CONV_K - 1), tm), :] * w[0:1, :]
        for k in range(1, CONV_K):
            acc = acc + stage_ref[c, pl.ds(halo - (CONV_K - 1) + k, tm), :] * w[k:k + 1, :]
        y_ref[c] = acc + b_ref[c]
        return carry

    lax.fori_loop(0, ns, strip, 0)

    tot = y_ref[0]
    for c in range(1, ns):
        tot = tot + y_ref[c]
    mu = jnp.sum(tot, axis=-1, keepdims=True) * (1.0 / d)
    sq = jnp.square(y_ref[0] - mu)
    for c in range(1, ns):
        sq = sq + jnp.square(y_ref[c] - mu)
    rstd = lax.rsqrt(jnp.sum(sq, axis=-1, keepdims=True) * (1.0 / d) + EPS)
    for c in range(ns):
        yn = (y_ref[c] - mu) * rstd * lg_ref[c] + lb_ref[c]
        a_ref[:, c * LANES:(c + 1) * LANES] = _silu(yn).astype(BF16)
    a = a_ref[...]
    for jn in range(d // tn):
        sl = slice(jn * tn, (jn + 1) * tn)
        o_ref[:, sl] = res_ref[:, sl] + jnp.dot(a, wo_ref[:, sl], preferred_element_type=F32)


def conv_layer(x, gain, w_in, dw_w, dw_b, ln_g, ln_b, w_out, *, tm=256):
    s, d = x.shape
    ch = dw_w.shape[1]
    ns = ch // LANES
    halo = 32
    xg = resident_matmul(x, w_in.astype(BF16), ch, w_offs=(0, ch), gain=gain, act="glu", name="conv_in")
    strips = lambda a: a.reshape(-1, ns, LANES).transpose(1, 0, 2)
    w_s = strips(jnp.pad(dw_w, ((0, halo - CONV_K), (0, 0))))
    const2 = lambda i: (0, 0)
    const3 = lambda i: (0, 0, 0)
    row = lambda i: (i, 0)
    return pl.pallas_call(
        functools.partial(_conv_kernel, tm=tm, halo=halo, tn=512),
        out_shape=jax.ShapeDtypeStruct((s, d), F32),
        grid=(s // tm,),
        in_specs=[
            pl.BlockSpec((tm, ch), row),
            pl.BlockSpec((ns, halo, LANES), const3),
            pl.BlockSpec((ns, 1, LANES), const3),
            pl.BlockSpec((ns, 1, LANES), const3),
            pl.BlockSpec((ns, 1, LANES), const3),
            pl.BlockSpec((ch, d), const2, pipeline_mode=pl.Buffered(1)),
            pl.BlockSpec((tm, d), row),
        ],
        out_specs=pl.BlockSpec((tm, d), row),
        scratch_shapes=[pltpu.VMEM((ns, tm + halo, LANES), F32), pltpu.VMEM((ns, tm, LANES), F32),
                        pltpu.VMEM((tm, ch), BF16)],
        compiler_params=_params("arbitrary"),
        name="conv_mod",
    )(xg, w_s, strips(dw_b.reshape(1, ch)), strips(ln_g.reshape(1, ch)), strips(ln_b.reshape(1, ch)),
      w_out.astype(BF16), x)


def kernel(x, positions, norm_mix, norm_ffn, final_norm, mla_w_in, mla_q_norm, mla_w_uq, mla_kv_norm,
           mla_w_ukv, mla_w_o, gdn_w_in, gdn_conv_w, gdn_a_log, gdn_dt_bias, gdn_o_norm, gdn_w_o, s5_w_in,
           s5_lam_re, s5_lam_im, s5_log_dt, s5_b_re, s5_b_im, s5_c_re, s5_c_im, s5_d, s5_w_out, cv_w_in,
           cv_dw_w, cv_dw_b, cv_ln_g, cv_ln_b, cv_w_out, ffn_w_gate, ffn_w_up, ffn_w_down):
    bsz, s, d = x.shape
    depth = norm_mix.shape[0]
    wg_b, wu_b, wd_b = ffn_w_gate, ffn_w_up, ffn_w_down
    outs = []
    for b in range(bsz):
        h = x[b]
        pos = positions[b]
        for i in range(depth):
            m, j = i % 4, i // 4
            if m == 0:
                h = mla_layer(h, pos, norm_mix[i], mla_w_in[j], mla_q_norm[j], mla_w_uq[j], mla_kv_norm[j],
                              mla_w_ukv[j], mla_w_o[j])
            elif m == 1:
                h = gdn_layer(h, norm_mix[i], gdn_w_in[j], gdn_conv_w[j], gdn_a_log[j], gdn_dt_bias[j],
                              gdn_o_norm[j], gdn_w_o[j])
            elif m == 2:
                h = s5_layer(h, norm_mix[i], s5_w_in[j], s5_lam_re[j], s5_lam_im[j], s5_log_dt[j], s5_b_re[j],
                             s5_b_im[j], s5_c_re[j], s5_c_im[j], s5_d[j], s5_w_out[j])
            else:
                h = conv_layer(h, norm_mix[i], cv_w_in[j], cv_dw_w[j], cv_dw_b[j], cv_ln_g[j], cv_ln_b[j],
                               cv_w_out[j])
            fg = final_norm if i == depth - 1 else None
            h = ffn(h, norm_ffn[i], wg_b, wu_b, wd_b, fg, layer=i)
        outs.append(h)
    return outs[0][None] if bsz == 1 else jnp.stack(outs)
```

```python
import functools
import math

import jax
import jax.numpy as jnp
from jax import lax
from jax.experimental import pallas as pl
from jax.experimental.pallas import tpu as pltpu

F32 = jnp.float32
BF16 = jnp.bfloat16
EPS = 1e-6
NEG_INF = -1e30

VMEM_LIMIT_BYTES = 56 * 1024 * 1024
LANES = 128

MLA_HEADS = 16
MLA_Q_RANK = 512
MLA_KV_RANK = 512
MLA_NOPE = 128
MLA_ROPE = 64
MLA_V = 128
MLA_HEAD_PAD = 256
ROPE_THETA = 10000.0
MLA_QSCALE = (MLA_NOPE + MLA_ROPE) ** -0.5 * math.log2(math.e)

GDN_QK_HEADS = 16
GDN_V_HEADS = 32
GDN_DK = 128
GDN_DV = 128
GDN_CONV = 4
GDN_CHUNK = 64
GDN_SUPER = 256

S5_GROUP = 16
S5_STATE = 64
S5_SEGMENTS = 8

CONV_K = 31


def _params(*sem):
    return pltpu.CompilerParams(dimension_semantics=sem, vmem_limit_bytes=VMEM_LIMIT_BYTES)


def _rms(x, g):
    return x * lax.rsqrt(jnp.mean(x * x, axis=-1, keepdims=True) + EPS) * g


def _silu(x):
    return x * jax.nn.sigmoid(x)


def _mm_kernel(*refs, has_norm, n_w, act, has_res):
    it = iter(refs)
    a_ref = next(it)
    g_ref = next(it) if has_norm else None
    w_refs = [next(it) for _ in range(n_w)]
    res_ref = next(it) if has_res else None
    o_ref = next(it)
    xn_ref = next(it) if has_norm else None

    if has_norm:
        @pl.when(pl.program_id(1) == 0)
        def _():
            xn_ref[...] = _rms(a_ref[...], g_ref[...]).astype(BF16)

        a = xn_ref[...]
    else:
        a = a_ref[...]
    accs = [jnp.dot(a, w[...], preferred_element_type=F32) for w in w_refs]
    if act == "glu":
        y = accs[0] * jax.nn.sigmoid(accs[1])
    else:
        y = accs[0]
    if has_res:
        y = res_ref[...] + y
    o_ref[...] = y.astype(o_ref.dtype)


def fused_matmul(a, ws, *, gain=None, act="none", res=None, out_dtype=F32, tm=512, tn=512,
                 w_col_offsets=None, out_map=None, out_shape=None, a_block=None, a_map=None, name="mm"):
    k = ws[0][0].shape[0]
    m = a.size // k
    n = ws[0][1]
    n_w = len(ws)
    tm = min(tm, m)
    tn = min(tn, n)
    assert m % tm == 0 and n % tn == 0
    has_norm = gain is not None
    has_res = res is not None
    if w_col_offsets is None:
        w_col_offsets = [0] * n_w
    if a_block is None:
        a_block, a_map = (tm, k), (lambda i, j: (i, 0))
    in_specs = [pl.BlockSpec(a_block, a_map)]
    args = [a]
    if has_norm:
        in_specs.append(pl.BlockSpec((1, k), lambda i, j: (0, 0)))
        args.append(gain.reshape(1, k))
    for (w, _), off in zip(ws, w_col_offsets):
        assert off % tn == 0
        ob = off // tn
        in_specs.append(pl.BlockSpec((k, tn), lambda i, j, ob=ob: (0, j + ob)))
        args.append(w)
    if has_res:
        in_specs.append(pl.BlockSpec((tm, tn), lambda i, j: (i, j)))
        args.append(res)
    if out_map is None:
        out_map = lambda i, j: (i, j)
    if out_shape is None:
        out_shape = (m, n)
    scratch = [pltpu.VMEM((tm, k), BF16)] if has_norm else []
    return pl.pallas_call(
        functools.partial(_mm_kernel, has_norm=has_norm, n_w=n_w, act=act, has_res=has_res),
        out_shape=jax.ShapeDtypeStruct(out_shape, out_dtype),
        grid=(m // tm, n // tn),
        in_specs=in_specs,
        out_specs=pl.BlockSpec((tm, tn), out_map),
        scratch_shapes=scratch,
        compiler_params=_params("arbitrary", "arbitrary"),
        name=name,
    )(*args)


def _mm_resident_kernel(*refs, has_norm, w_offs, act, has_res, n, tn):
    it = iter(refs)
    a_ref = next(it)
    g_ref = next(it) if has_norm else None
    w_ref = next(it)
    res_ref = next(it) if has_res else None
    o_ref = next(it)
    a = _rms(a_ref[...], g_ref[...]).astype(BF16) if has_norm else a_ref[...]
    for jn in range(n // tn):
        sl = slice(jn * tn, (jn + 1) * tn)
        accs = [jnp.dot(a, w_ref[:, off + jn * tn:off + (jn + 1) * tn], preferred_element_type=F32)
                for off in w_offs]
        y = accs[0] * jax.nn.sigmoid(accs[1]) if act == "glu" else accs[0]
        if has_res:
            y = res_ref[:, sl] + y
        o_ref[:, sl] = y.astype(o_ref.dtype)


def resident_matmul(a, w, n, *, w_offs=(0,), gain=None, act="none", res=None, out_dtype=F32, tm=512, tn=512,
                    a_block=None, a_map=None, out_map=None, out_shape=None, name="mm"):
    k = w.shape[0]
    m = a.size // k
    tm = min(tm, m)
    tn = min(tn, n)
    assert m % tm == 0 and n % tn == 0
    has_norm = gain is not None
    has_res = res is not None
    row = lambda i: (i, 0)
    const = lambda i: (0, 0)
    in_specs = [pl.BlockSpec(a_block or (tm, k), a_map or row)]
    args = [a]
    if has_norm:
        in_specs.append(pl.BlockSpec((1, k), const))
        args.append(gain.reshape(1, k))
    in_specs.append(pl.BlockSpec(w.shape, const, pipeline_mode=pl.Buffered(1)))
    args.append(w)
    if has_res:
        in_specs.append(pl.BlockSpec((tm, n), row))
        args.append(res)
    return pl.pallas_call(
        functools.partial(_mm_resident_kernel, has_norm=has_norm, w_offs=tuple(w_offs), act=act,
                          has_res=has_res, n=n, tn=tn),
        out_shape=jax.ShapeDtypeStruct(out_shape or (m, n), out_dtype),
        grid=(m // tm,),
        in_specs=in_specs,
        out_specs=pl.BlockSpec((tm, n), out_map or row),
        compiler_params=_params("arbitrary"),
        name=name,
    )(*args)


def _ffn_kernel(*refs, final_norm):
    if final_norm:
        x_ref, g_ref, wg_ref, wu_ref, wd_ref, fg_ref, o_ref, xn_ref = refs
    else:
        x_ref, g_ref, wg_ref, wu_ref, wd_ref, o_ref, xn_ref = refs
        fg_ref = None
    j = pl.program_id(1)

    @pl.when(j == 0)
    def _():
        x = x_ref[...]
        xn_ref[...] = _rms(x, g_ref[...]).astype(BF16)
        o_ref[...] = x

    xn = xn_ref[...]
    hg = jnp.dot(xn, wg_ref[...].astype(BF16), preferred_element_type=F32)
    hu = jnp.dot(xn, wu_ref[...].astype(BF16), preferred_element_type=F32)
    h = (_silu(hg) * hu).astype(BF16)
    o_ref[...] += jnp.dot(h, wd_ref[...].astype(BF16), preferred_element_type=F32)

    if final_norm:
        @pl.when(j == pl.num_programs(1) - 1)
        def _():
            o_ref[...] = _rms(o_ref[...], fg_ref[...])


def ffn(x, gain, wg, wu, wd, final_gain=None, *, layer=0, tm=1024, th=256):
    m, d = x.shape
    hdim = wg.shape[2]
    assert m % tm == 0 and hdim % th == 0
    final_norm = final_gain is not None
    in_specs = [
        pl.BlockSpec((tm, d), lambda i, j: (i, 0)),
        pl.BlockSpec((1, d), lambda i, j: (0, 0)),
        pl.BlockSpec((None, d, th), lambda i, j: (layer, 0, j)),
        pl.BlockSpec((None, d, th), lambda i, j: (layer, 0, j)),
        pl.BlockSpec((None, th, d), lambda i, j: (layer, j, 0)),
    ]
    args = [x, gain.reshape(1, d), wg, wu, wd]
    if final_norm:
        in_specs.append(pl.BlockSpec((1, d), lambda i, j: (0, 0)))
        args.append(final_gain.reshape(1, d))
    return pl.pallas_call(
        functools.partial(_ffn_kernel, final_norm=final_norm),
        out_shape=jax.ShapeDtypeStruct((m, d), F32),
        grid=(m // tm, hdim // th),
        in_specs=in_specs,
        out_specs=pl.BlockSpec((tm, d), lambda i, j: (i, 0)),
        scratch_shapes=[pltpu.VMEM((tm, d), BF16)],
        compiler_params=_params("arbitrary", "arbitrary"),
        name="ffn",
    )(*args)


def _rope_block(xb, c, s1, s2):
    return xb * c + pltpu.roll(xb, 96, 1) * s1 + pltpu.roll(xb, 32, 1) * s2


def _mla_proj_kernel(x_ref, g_ref, pos_ref, invf_ref, win_ref, qn_ref, wuq_ref, kvn_ref, wukv_ref,
                     q_ref, k_ref, v_ref):
    xn = _rms(x_ref[...], g_ref[...]).astype(BF16)
    proj = jnp.dot(xn, win_ref[...], preferred_element_type=F32)
    qn = _rms(proj[:, :MLA_Q_RANK], qn_ref[...]).astype(BF16)
    kvn = _rms(proj[:, MLA_Q_RANK:MLA_Q_RANK + MLA_KV_RANK], kvn_ref[...]).astype(BF16)
    kr = proj[:, MLA_Q_RANK + MLA_KV_RANK:]

    ang = pos_ref[...].astype(F32) * invf_ref[...]
    cs = jnp.cos(ang)
    sn = jnp.sin(ang)
    lane = lax.broadcasted_iota(jnp.int32, ang.shape, 1)
    half = MLA_ROPE // 2
    c = jnp.where(lane < MLA_ROPE, cs, 0.0)
    s1 = jnp.where(lane < half, -sn, 0.0)
    s2 = jnp.where((lane >= half) & (lane < MLA_ROPE), sn, 0.0)

    k_rope = _rope_block(kr, c, s1, s2).astype(BF16)
    hp = MLA_HEAD_PAD
    for h in range(MLA_HEADS):
        qh = jnp.dot(qn, wuq_ref[:, h * hp:(h + 1) * hp], preferred_element_type=F32) * MLA_QSCALE
        q_ref[:, h * hp:h * hp + LANES] = qh[:, :LANES].astype(BF16)
        q_ref[:, h * hp + LANES:(h + 1) * hp] = _rope_block(qh[:, LANES:], c, s1, s2).astype(BF16)
        kvh = jnp.dot(kvn, wukv_ref[:, h * hp:(h + 1) * hp], preferred_element_type=F32)
        k_ref[:, h * hp:h * hp + LANES] = kvh[:, :LANES].astype(BF16)
        k_ref[:, h * hp + LANES:(h + 1) * hp] = k_rope
        v_ref[:, h * MLA_V:(h + 1) * MLA_V] = kvh[:, LANES:].astype(BF16)


def mla_proj(x, gain, pos, invf, w_in_p, q_norm, w_uq_p, kv_norm, w_ukv, *, tm=512):
    s, d = x.shape
    nq = MLA_HEADS * MLA_HEAD_PAD
    const = lambda i: (0, 0)
    return pl.pallas_call(
        _mla_proj_kernel,
        out_shape=(jax.ShapeDtypeStruct((s, nq), BF16), jax.ShapeDtypeStruct((s, nq), BF16),
                   jax.ShapeDtypeStruct((s, MLA_HEADS * MLA_V), BF16)),
        grid=(s // tm,),
        in_specs=[
            pl.BlockSpec((tm, d), lambda i: (i, 0)),
            pl.BlockSpec((1, d), const),
            pl.BlockSpec((tm, 1), lambda i: (i, 0)),
            pl.BlockSpec((1, LANES), const),
            pl.BlockSpec(w_in_p.shape, const, pipeline_mode=pl.Buffered(1)),
            pl.BlockSpec((1, MLA_Q_RANK), const),
            pl.BlockSpec(w_uq_p.shape, const, pipeline_mode=pl.Buffered(1)),
            pl.BlockSpec((1, MLA_KV_RANK), const),
            pl.BlockSpec(w_ukv.shape, const, pipeline_mode=pl.Buffered(1)),
        ],
        out_specs=(pl.BlockSpec((tm, nq), lambda i: (i, 0)), pl.BlockSpec((tm, nq), lambda i: (i, 0)),
                   pl.BlockSpec((tm, MLA_HEADS * MLA_V), lambda i: (i, 0))),
        compiler_params=_params("arbitrary"),
        name="mla_proj",
    )(x, gain.reshape(1, d), pos, invf, w_in_p, q_norm.reshape(1, -1), w_uq_p, kv_norm.reshape(1, -1), w_ukv)


def _flash_kernel(q_ref, k_ref, v_ref, o_ref, m_ref, l_ref, acc_ref, *, tq, tk):
    qi = pl.program_id(1)
    m_ref[...] = jnp.full(m_ref.shape, NEG_INF, F32)
    l_ref[...] = jnp.zeros(l_ref.shape, F32)
    acc_ref[...] = jnp.zeros(acc_ref.shape, F32)

    def step(start, masked, r0=0):
        k = k_ref[pl.ds(start, tk), :]
        v = v_ref[pl.ds(start, tk), :]
        s = lax.dot_general(q_ref[r0:, :], k, (((1,), (1,)), ((), ())), preferred_element_type=F32)
        if masked:
            row = lax.broadcasted_iota(jnp.int32, (tk, tk), 0)
            col = lax.broadcasted_iota(jnp.int32, (tk, tk), 1)
            tri = jnp.where(col <= row, s[:tk], NEG_INF)
            s = tri if s.shape[0] == tk else jnp.concatenate([tri, s[tk:]], axis=0)
        m_prev = m_ref[r0:, :]
        m_new = jnp.maximum(m_prev, jnp.max(s, axis=-1, keepdims=True))
        alpha = jnp.exp2(m_prev - m_new)
        l_new = alpha * l_ref[r0:, :]
        ps = []
        for c in range(tk // LANES):
            pc = jnp.exp2(s[:, c * LANES:(c + 1) * LANES] - m_new)
            l_new = l_new + pc
            ps.append(pc.astype(BF16))
        p = jnp.concatenate(ps, axis=1)
        acc_ref[r0:, :] = alpha * acc_ref[r0:, :] + jnp.dot(p, v, preferred_element_type=F32)
        l_ref[r0:, :] = l_new
        m_ref[r0:, :] = m_new

    nd = tq // tk

    def body(kp, carry):
        for u in range(nd):
            step(pl.multiple_of((kp * nd + u) * tk, tk), False)
        return carry

    lax.fori_loop(0, qi, body, 0)
    for dblk in range(nd):
        step(pl.multiple_of((qi * nd + dblk) * tk, tk), True, r0=dblk * tk)
    o_ref[...] = (acc_ref[...] / jnp.sum(l_ref[...], axis=-1, keepdims=True)).astype(o_ref.dtype)


def flash_attention(q, k, v, *, tq=2048, tk=512):
    s = q.shape[0]
    tq = min(tq, s)
    tk = min(tk, tq)
    hp, dv = MLA_HEAD_PAD, MLA_V
    return pl.pallas_call(
        functools.partial(_flash_kernel, tq=tq, tk=tk),
        out_shape=jax.ShapeDtypeStruct((s, MLA_HEADS * dv), BF16),
        grid=(MLA_HEADS, s // tq),
        in_specs=[
            pl.BlockSpec((tq, hp), lambda h, i: (i, h)),
            pl.BlockSpec((s, hp), lambda h, i: (0, h)),
            pl.BlockSpec((s, dv), lambda h, i: (0, h)),
        ],
        out_specs=pl.BlockSpec((tq, dv), lambda h, i: (i, h)),
        scratch_shapes=[pltpu.VMEM((tq, LANES), F32), pltpu.VMEM((tq, LANES), F32), pltpu.VMEM((tq, dv), F32)],
        compiler_params=_params("arbitrary", "arbitrary"),
        name="mla_flash",
    )(q, k, v)


def mla_layer(x, positions, gain, w_in, q_norm, w_uq, kv_norm, w_ukv, w_o):
    s, d = x.shape
    half = MLA_ROPE // 2
    inv = ROPE_THETA ** (-jnp.arange(half, dtype=F32) / half)
    invf = jnp.concatenate([inv, inv, jnp.zeros((LANES - MLA_ROPE,), F32)]).reshape(1, LANES)
    w_in_p = jnp.pad(w_in, ((0, 0), (0, LANES - MLA_ROPE))).astype(BF16)
    dq = MLA_NOPE + MLA_ROPE
    w_uq_p = jnp.pad(w_uq.reshape(MLA_Q_RANK, MLA_HEADS, dq), ((0, 0), (0, 0), (0, MLA_HEAD_PAD - dq)))
    w_uq_p = w_uq_p.reshape(MLA_Q_RANK, MLA_HEADS * MLA_HEAD_PAD).astype(BF16)
    q, k, v = mla_proj(x, gain, positions.reshape(s, 1), invf, w_in_p, q_norm, w_uq_p, kv_norm,
                       w_ukv.astype(BF16))
    o = flash_attention(q, k, v)
    return resident_matmul(o, w_o.astype(BF16), d, res=x, name="mla_out")


def _softplus(x):
    return jnp.maximum(x, 0.0) + jnp.log(1.0 + jnp.exp(-jnp.abs(x)))


def _gdn_proj_kernel(x_ref, g_ref, w_ref, cw_ref, wba_ref, alr_ref, dtr_ref,
                     o_ref, bg_ref, gt_ref, xn_ref, stage_ref, carry_ref, *, tm, tn, sub, n_qk, n_conv, qscale):
    i = pl.program_id(0)
    j = pl.program_id(1)
    nh = GDN_V_HEADS

    @pl.when(j == 0)
    def _():
        xn = _rms(x_ref[...], g_ref[...]).astype(BF16)
        xn_ref[...] = xn
        ba = jnp.dot(xn, wba_ref[...], preferred_element_type=F32)
        lane = lax.broadcasted_iota(jnp.int32, ba.shape, 1)
        gate = -jnp.exp(alr_ref[...]) * _softplus(ba + dtr_ref[...])
        bg_ref[...] = jnp.where(lane < nh, jax.nn.sigmoid(ba), gate)
        gt_ref[...] = gate.T[nh:2 * nh, :]

    nsub = tn // sub

    def proj(c):
        return jnp.dot(xn_ref[...], w_ref[:, c * sub:(c + 1) * sub].astype(BF16),
                       preferred_element_type=F32)

    def conv_silu(c):
        acc = proj(c)
        outs = []
        for hh in range(sub // LANES):
            k = c * (sub // LANES) + hh
            ks = slice(k * LANES, (k + 1) * LANES)
            a = acc[:, hh * LANES:(hh + 1) * LANES]
            stage_ref[k, 0:8, :] = jnp.where(i == 0, 0.0, carry_ref[j, :, ks])
            stage_ref[k, 8:, :] = a
            carry_ref[j, :, ks] = a[tm - 8:, :]
            cw = cw_ref[:, ks]
            y = stage_ref[k, pl.ds(8 - (GDN_CONV - 1), tm), :] * cw[0:1, :]
            for t in range(1, GDN_CONV):
                y = y + stage_ref[k, pl.ds(8 - (GDN_CONV - 1) + t, tm), :] * cw[t:t + 1, :]
            outs.append((ks, _silu(y)))
        return outs

    @pl.when(j < n_qk)
    def _():
        sc = jnp.where(j < n_qk // 2, qscale, 1.0)
        for c in range(nsub):
            for ks, yb in conv_silu(c):
                nrm = lax.rsqrt(jnp.sum(yb * yb, axis=-1, keepdims=True) + EPS) * sc
                o_ref[:, ks] = (yb * nrm).astype(o_ref.dtype)

    @pl.when((j >= n_qk) & (j < n_conv))
    def _():
        for c in range(nsub):
            for ks, yb in conv_silu(c):
                o_ref[:, ks] = yb.astype(o_ref.dtype)

    @pl.when(j >= n_conv)
    def _():
        for c in range(nsub):
            o_ref[:, c * sub:(c + 1) * sub] = proj(c).astype(o_ref.dtype)


def gdn_proj(x, gain, w_qkvz, conv_w_p, w_ba, a_log, dt_bias, *, tm=1024, tn=1024, sub=256):
    s, d = x.shape
    qk_dim = GDN_QK_HEADS * GDN_DK
    v_dim = GDN_V_HEADS * GDN_DV
    n = 2 * qk_dim + 2 * v_dim
    assert n % tn == 0 and (2 * qk_dim) % tn == 0 and v_dim % tn == 0 and tn % sub == 0
    n_qk = 2 * qk_dim // tn
    n_conv = (2 * qk_dim + v_dim) // tn
    nh = GDN_V_HEADS
    pad = jnp.zeros((nh,), F32)
    alr = jnp.concatenate([pad, a_log, pad, pad]).reshape(1, LANES)
    dtr = jnp.concatenate([pad, dt_bias, pad, pad]).reshape(1, LANES)
    const = lambda i, j: (0, 0)
    kern = functools.partial(_gdn_proj_kernel, tm=tm, tn=tn, sub=sub, n_qk=n_qk, n_conv=n_conv,
                             qscale=GDN_DK ** -0.5)
    return pl.pallas_call(
        kern,
        out_shape=(jax.ShapeDtypeStruct((s, n), BF16), jax.ShapeDtypeStruct((s, LANES), F32),
                   jax.ShapeDtypeStruct((nh, s), F32)),
        grid=(s // tm, n // tn),
        in_specs=[
            pl.BlockSpec((tm, d), lambda i, j: (i, 0)),
            pl.BlockSpec((1, d), const),
            pl.BlockSpec((d, tn), lambda i, j: (0, j)),
            pl.BlockSpec((GDN_CONV, tn), lambda i, j: (0, jnp.minimum(j, n_conv - 1))),
            pl.BlockSpec((d, LANES), const),
            pl.BlockSpec((1, LANES), const),
            pl.BlockSpec((1, LANES), const),
        ],
        out_specs=(pl.BlockSpec((tm, tn), lambda i, j: (i, j)),
                   pl.BlockSpec((tm, LANES), lambda i, j: (i, 0)),
                   pl.BlockSpec((nh, tm), lambda i, j: (0, i))),
        scratch_shapes=[pltpu.VMEM((tm, d), BF16), pltpu.VMEM((tn // LANES, tm + 8, LANES), F32),
                        pltpu.VMEM((n_conv, 8, tn), F32)],
        compiler_params=_params("arbitrary", "arbitrary"),
        name="gdn_proj",
    )(x, gain.reshape(1, d), w_qkvz, conv_w_p, w_ba, alr, dtr)


def _gdn_delta_kernel(q_ref, k_ref, v_ref, z_ref, bg_ref, gt_ref, on_ref, o_ref, state_ref, mask_ref, tri_ref,
                      gtc_ref, *, hq, c):
    hb = pl.program_id(0)
    t = pl.program_id(1)
    nlev = c.bit_length() - 1
    row = lax.broadcasted_iota(jnp.int32, (c, c), 0)
    col = lax.broadcasted_iota(jnp.int32, (c, c), 1)

    @pl.when((hb == 0) & (t == 0))
    def _():
        for l in range(nlev):
            m = ((row >> (l + 1)) == (col >> (l + 1))) & (((row >> l) & 1) == 1) & (((col >> l) & 1) == 0)
            mask_ref[l] = jnp.where(m, 1.0, 0.0).astype(BF16)
        tri_ref[...] = jnp.where(col <= row, 1.0, 0.0).astype(BF16)

    @pl.when(t == 0)
    def _():
        state_ref[...] = jnp.zeros(state_ref.shape, F32)

    tri = tri_ref[...]
    incl = col <= row
    strict = col < row
    bg = bg_ref[...]
    lane = lax.broadcasted_iota(jnp.int32, bg.shape, 1)
    nt = (((1,), (1,)), ((), ()))

    def split3(x):
        x1 = x.astype(BF16)
        r1 = x - x1.astype(F32)
        x2 = r1.astype(BF16)
        return x1, x2, (r1 - x2.astype(F32)).astype(BF16)

    gall = sum(jnp.dot(tri, p, preferred_element_type=F32) for p in split3(bg))
    gtc_ref[...] = sum(lax.dot_general(p, tri, nt, preferred_element_type=F32) for p in split3(gt_ref[...]))

    nb = 2 * hq
    qs, ks, kfs, amats, attns, betas, gcs, glasts = [], [], [], [], [], [], [], []
    for a in range(hq):
        q = q_ref[:, a * GDN_DK:(a + 1) * GDN_DK]
        k = k_ref[:, a * GDN_DK:(a + 1) * GDN_DK]
        kk = lax.dot_general(k, k, nt, preferred_element_type=F32)
        qk = lax.dot_general(q, k, nt, preferred_element_type=F32)
        for b in range(2):
            vh = (hb * hq + a) * 2 + b
            beta = jnp.sum(jnp.where(lane == vh, bg, 0.0), axis=-1, keepdims=True)
            gc_col = jnp.sum(jnp.where(lane == vh + GDN_V_HEADS, gall, 0.0), axis=-1, keepdims=True)
            gc_row = gtc_ref[pl.ds(vh, 1), :]
            dec = jnp.exp(jnp.where(incl, gc_col - gc_row, NEG_INF))
            amats.append(jnp.where(strict, kk * dec, 0.0) * beta)
            attns.append((qk * dec).astype(BF16))
            qs.append(q)
            ks.append(k)
            betas.append(beta)
            gcs.append(gc_col)
            glasts.append(gc_col[c - 1:c, :])
    eye = jnp.where(row == col, 1.0, 0.0).astype(BF16)
    abs_ = [amats[i].astype(BF16) for i in range(nb)]
    ts = [eye - abs_[i] * mask_ref[0] for i in range(nb)]
    for l in range(1, nlev):
        s = 1 << l
        if s < 16:
            xs = [jnp.dot(abs_[i] * mask_ref[l], ts[i], preferred_element_type=F32).astype(BF16)
                  for i in range(nb)]
            ts = [ts[i] - jnp.dot(ts[i], xs[i], preferred_element_type=F32).astype(BF16) for i in range(nb)]
            continue
        odd = [slice((2 * k + 1) * s, (2 * k + 2) * s) for k in range(c // (2 * s))]
        even = [slice(2 * k * s, (2 * k + 1) * s) for k in range(c // (2 * s))]
        pick = lambda a, sls: jnp.concatenate([a[sl] for sl in sls], axis=0) if len(sls) > 1 else a[sls[0]]
        zero = jnp.zeros((s, c), BF16)
        new_ts = []
        m_odd = jnp.concatenate([mask_ref[l, sl, :] for sl in odd], axis=0) if len(odd) > 1 else mask_ref[l, odd[0], :]
        e_odd = [pick(abs_[i], odd) * m_odd for i in range(nb)]
        x_odd = [jnp.dot(e_odd[i], ts[i], preferred_element_type=F32).astype(BF16) for i in range(nb)]
        for i in range(nb):
            x_full = jnp.concatenate([p for k in range(len(odd)) for p in (zero, x_odd[i][k * s:(k + 1) * s])],
                                     axis=0)
            t_odd = pick(ts[i], odd)
            t_odd = t_odd - jnp.dot(t_odd, x_full, preferred_element_type=F32).astype(BF16)
            new_ts.append(jnp.concatenate(
                [p for k in range(len(odd)) for p in (ts[i][even[k]], t_odd[k * s:(k + 1) * s])], axis=0))
        ts = new_ts
    rs = [ts[i] - eye for i in range(nb)]
    egs = [jnp.exp(gcs[i]) for i in range(nb)]
    kfs = [ks[i].astype(F32) for i in range(nb)]
    rhss = [jnp.concatenate([v_ref[:, i * GDN_DV:(i + 1) * GDN_DV].astype(F32) * betas[i],
                             kfs[i] * (betas[i] * egs[i])], axis=1) for i in range(nb)]
    uws = [rhss[i] + jnp.dot(rs[i], rhss[i].astype(BF16), preferred_element_type=F32)
           for i in range(nb)]
    sts = [state_ref[i] for i in range(nb)]
    wss = [jnp.dot(jnp.concatenate([uws[i][:, GDN_DV:].astype(BF16),
                                    (qs[i].astype(F32) * egs[i]).astype(BF16)], axis=0),
                   sts[i].astype(BF16), preferred_element_type=F32) for i in range(nb)]
    vnbs = [(uws[i][:, :GDN_DV] - wss[i][:c]).astype(BF16) for i in range(nb)]
    os_ = [wss[i][c:] + jnp.dot(attns[i], vnbs[i], preferred_element_type=F32) for i in range(nb)]
    for i in range(nb):
        kdec = (kfs[i] * jnp.exp(glasts[i] - gcs[i])).astype(BF16)
        state_ref[i] = sts[i] * jnp.exp(glasts[i]) + lax.dot_general(
            kdec, vnbs[i], (((0,), (0,)), ((), ())), preferred_element_type=F32)
        z = z_ref[:, i * GDN_DV:(i + 1) * GDN_DV].astype(F32)
        o_ref[:, i * GDN_DV:(i + 1) * GDN_DV] = (_rms(os_[i], on_ref[...]) * _silu(z)).astype(o_ref.dtype)


def gdn_delta(qkvz, bg, gt, o_norm, *, hq=8, c=GDN_SUPER):
    s = qkvz.shape[0]
    qk_dim = GDN_QK_HEADS * GDN_DK
    v_dim = GDN_V_HEADS * GDN_DV
    wq = hq * GDN_DK
    wv = 2 * hq * GDN_DV
    nlev = c.bit_length() - 1
    return pl.pallas_call(
        functools.partial(_gdn_delta_kernel, hq=hq, c=c),
        out_shape=jax.ShapeDtypeStruct((s, v_dim), BF16),
        grid=(GDN_QK_HEADS // hq, s // c),
        in_specs=[
            pl.BlockSpec((c, wq), lambda h, t: (t, h)),
            pl.BlockSpec((c, wq), lambda h, t: (t, qk_dim // wq + h)),
            pl.BlockSpec((c, wv), lambda h, t: (t, 2 * qk_dim // wv + h)),
            pl.BlockSpec((c, wv), lambda h, t: (t, (2 * qk_dim + v_dim) // wv + h)),
            pl.BlockSpec((c, LANES), lambda h, t: (t, 0)),
            pl.BlockSpec((GDN_V_HEADS, c), lambda h, t: (0, t)),
            pl.BlockSpec((1, GDN_DV), lambda h, t: (0, 0)),
        ],
        out_specs=pl.BlockSpec((c, wv), lambda h, t: (t, h)),
        scratch_shapes=[pltpu.VMEM((2 * hq, GDN_DK, GDN_DV), F32), pltpu.VMEM((nlev, c, c), BF16),
                        pltpu.VMEM((c, c), BF16), pltpu.VMEM((GDN_V_HEADS, c), F32)],
        compiler_params=_params("arbitrary", "arbitrary"),
        name="gdn_delta",
    )(qkvz, qkvz, qkvz, qkvz, bg, gt, o_norm.reshape(1, GDN_DV))


def gdn_layer(x, gain, w_in, conv_w, a_log, dt_bias, o_norm, w_o):
    s, d = x.shape
    qk_dim = GDN_QK_HEADS * GDN_DK
    v_dim = GDN_V_HEADS * GDN_DV
    n_main = 2 * qk_dim + 2 * v_dim
    nh = GDN_V_HEADS
    w_ba = jnp.pad(w_in[:, n_main:], ((0, 0), (0, LANES - 2 * nh))).astype(BF16)
    qkvz, bg, gt = gdn_proj(x, gain, w_in, conv_w, w_ba, a_log, dt_bias)
    o = gdn_delta(qkvz, bg, gt, o_norm)
    return resident_matmul(o, w_o.astype(BF16), d, res=x, name="gdn_out")


def _s5_disc_kernel(lre_ref, lim_ref, ldt_ref, lre_e_ref, lim_e_ref, bre_ref, bim_ref,
                    lbre_ref, lbim_ref, bbre_ref, bbim_ref):
    dt = jnp.exp(ldt_ref[...])

    def zoh(lre, lim):
        mag = jnp.exp(lre * dt)
        ang = lim * dt
        lb_re = mag * jnp.cos(ang)
        lb_im = mag * jnp.sin(ang)
        den = lre * lre + lim * lim
        nr = lb_re - 1.0
        f_re = (nr * lre + lb_im * lim) / den
        f_im = (lb_im * lre - nr * lim) / den
        return lb_re, lb_im, f_re, f_im

    lb_re, lb_im, _, _ = zoh(lre_ref[...], lim_ref[...])
    lbre_ref[...] = lb_re
    lbim_ref[...] = lb_im
    _, _, f_re, f_im = zoh(lre_e_ref[...], lim_e_ref[...])
    bbre_ref[...] = f_re * bre_ref[...] - f_im * bim_ref[...]
    bbim_ref[...] = f_re * bim_ref[...] + f_im * bre_ref[...]


def _s5_scan_kernel(u_ref, bblk_ref, cblk_ref, are_ref, aim_ref, d_ref, o_ref, st_ref, bu_ref, x_ref,
                    *, lt, n_sq):
    p = pl.program_id(1)
    tb = pl.program_id(2)
    last = pl.num_programs(2) - 1
    nseg = S5_SEGMENTS
    half = bu_ref.shape[1] // 2

    @pl.when((p == 0) & (tb == 0))
    def _():
        st_ref[...] = jnp.zeros(st_ref.shape, F32)

    u = jnp.swapaxes(u_ref[...], 0, 1).reshape(lt * nseg, u_ref.shape[2])
    bu_ref[...] = jnp.dot(u.astype(BF16), bblk_ref[0], preferred_element_type=F32)
    ar = jnp.broadcast_to(are_ref[0], (nseg, half))
    ai = jnp.broadcast_to(aim_ref[0], (nseg, half))

    def body(tau, carry):
        xr, xi = carry
        r0 = pl.multiple_of(tau * nseg, nseg)
        b = bu_ref[pl.ds(r0, nseg), :]
        nxr = ar * xr - ai * xi + b[:, :half]
        nxi = ar * xi + ai * xr + b[:, half:]
        x_ref[pl.ds(r0, nseg), :half] = nxr
        x_ref[pl.ds(r0, nseg), half:] = nxi
        return nxr, nxi

    st = st_ref[...]
    xr, xi = lax.fori_loop(0, lt, body, (st[:, :half], st[:, half:]), unroll=8)
    st_ref[:, :half] = xr
    st_ref[:, half:] = xi

    @pl.when((p == 0) & (tb == last))
    def _():
        pr, pi = are_ref[0], aim_ref[0]
        for _ in range(n_sq):
            pr, pi = pr * pr - pi * pi, 2.0 * pr * pi
        sr = jnp.zeros((1, half), F32)
        si = jnp.zeros((1, half), F32)
        rows_r, rows_i = [sr], [si]
        for r in range(nseg - 1):
            sr, si = pr * sr - pi * si + xr[r:r + 1, :], pr * si + pi * sr + xi[r:r + 1, :]
            rows_r.append(sr)
            rows_i.append(si)
        st_ref[:, :half] = jnp.concatenate(rows_r, axis=0)
        st_ref[:, half:] = jnp.concatenate(rows_i, axis=0)

    @pl.when(p == 1)
    def _():
        y = jnp.dot(x_ref[...].astype(BF16), cblk_ref[0], preferred_element_type=F32) + d_ref[...] * u
        yg = jax.nn.gelu(y).reshape(lt, nseg, y.shape[1])
        o_ref[...] = jnp.swapaxes(yg, 0, 1).astype(o_ref.dtype)


def s5_layer(x, gain, w_in, lam_re, lam_im, log_dt, b_re, b_im, c_re, c_im, d_skip, w_out):
    s, d = x.shape
    width = w_in.shape[1]
    g, p_st, gc = b_re.shape
    nseg = S5_SEGMENTS
    sseg = s // nseg
    assert sseg & (sseg - 1) == 0
    tm = min(512, sseg)
    nb = sseg // tm
    ncb = width // LANES
    gpb = LANES // gc
    half = gpb * p_st

    u = resident_matmul(x, w_in.astype(BF16), width, gain=gain, tm=tm, name="s5_in")
    u3 = u.reshape(nseg, sseg, width)

    rep = lambda a: jnp.repeat(a, gc, axis=1)
    vm = pl.BlockSpec(memory_space=pltpu.VMEM)
    lb_re, lb_im, bb_re, bb_im = pl.pallas_call(
        _s5_disc_kernel,
        out_shape=(jax.ShapeDtypeStruct((g, p_st), F32), jax.ShapeDtypeStruct((g, p_st), F32),
                   jax.ShapeDtypeStruct((g, p_st * gc), F32), jax.ShapeDtypeStruct((g, p_st * gc), F32)),
        in_specs=[vm] * 7, out_specs=(vm, vm, vm, vm), name="s5_disc",
    )(lam_re, lam_im, log_dt.reshape(g, 1), rep(lam_re), rep(lam_im),
      b_re.reshape(g, p_st * gc), b_im.reshape(g, p_st * gc))

    eye = jnp.eye(gpb, dtype=F32)

    def in_blocks(bb):
        t = bb.reshape(ncb, gpb, p_st, gc)
        return jnp.einsum("bgpc,gh->bgchp", t, eye).reshape(ncb, gpb * gc, gpb * p_st)

    def out_blocks(cc):
        t = cc.reshape(ncb, gpb, gc, p_st)
        return jnp.einsum("bgcp,gh->bhpgc", t, eye).reshape(ncb, gpb * p_st, gpb * gc)

    bblk = jnp.concatenate([in_blocks(bb_re), in_blocks(bb_im)], axis=2).astype(BF16)
    cblk = jnp.concatenate([out_blocks(c_re), -out_blocks(c_im)], axis=1).astype(BF16)
    a_re = lb_re.reshape(ncb, 1, half)
    a_im = lb_im.reshape(ncb, 1, half)

    lt = min(128, sseg)
    rows = lt * nseg
    yg = pl.pallas_call(
        functools.partial(_s5_scan_kernel, lt=lt, n_sq=sseg.bit_length() - 1),
        out_shape=jax.ShapeDtypeStruct((nseg, sseg, width), BF16),
        grid=(ncb, 2, sseg // lt),
        in_specs=[
            pl.BlockSpec((nseg, lt, LANES), lambda cb, p, t: (0, t, cb)),
            pl.BlockSpec((1, LANES, 2 * half), lambda cb, p, t: (cb, 0, 0)),
            pl.BlockSpec((1, 2 * half, LANES), lambda cb, p, t: (cb, 0, 0)),
            pl.BlockSpec((1, 1, half), lambda cb, p, t: (cb, 0, 0)),
            pl.BlockSpec((1, 1, half), lambda cb, p, t: (cb, 0, 0)),
            pl.BlockSpec((1, LANES), lambda cb, p, t: (0, cb)),
        ],
        out_specs=pl.BlockSpec((nseg, lt, LANES), lambda cb, p, t: (0, t * p, cb)),
        scratch_shapes=[pltpu.VMEM((nseg, 2 * half), F32), pltpu.VMEM((rows, 2 * half), F32),
                        pltpu.VMEM((rows, 2 * half), F32)],
        compiler_params=_params("arbitrary", "arbitrary", "arbitrary"),
        name="s5_scan",
    )(u3, bblk, cblk, a_re, a_im, d_skip.reshape(1, width))

    return resident_matmul(yg.reshape(s, width), w_out.astype(BF16), d, w_offs=(0, d), act="glu", res=x, tm=tm,
                           name="s5_out")


def _conv_kernel(xg_ref, w_ref, b_ref, lg_ref, lb_ref, wo_ref, res_ref, o_ref, stage_ref, y_ref, a_ref,
                 *, tm, halo, tn):
    i = pl.program_id(0)
    ns = stage_ref.shape[0]
    d = ns * LANES
    for c in range(ns):
        prev = stage_ref[c, tm:tm + halo, :]
        stage_ref[c, 0:halo, :] = jnp.where(i == 0, 0.0, prev)
        stage_ref[c, halo:, :] = xg_ref[:, c * LANES:(c + 1) * LANES]

    def strip(c, carry):
        w = w_ref[c]
        acc = stage_ref[c, pl.ds(halo - (CONV_K - 1), tm), :] * w[0:1, :]
        for k in range(1, CONV_K):
            acc = acc + stage_ref[c, pl.ds(halo - (CONV_K - 1) + k, tm), :] * w[k:k + 1, :]
        y_ref[c] = acc + b_ref[c]
        return carry

    lax.fori_loop(0, ns, strip, 0)

    tot = y_ref[0]
    for c in range(1, ns):
        tot = tot + y_ref[c]
    mu = jnp.sum(tot, axis=-1, keepdims=True) * (1.0 / d)
    sq = jnp.square(y_ref[0] - mu)
    for c in range(1, ns):
        sq = sq + jnp.square(y_ref[c] - mu)
    rstd = lax.rsqrt(jnp.sum(sq, axis=-1, keepdims=True) * (1.0 / d) + EPS)
    for c in range(ns):
        yn = (y_ref[c] - mu) * rstd * lg_ref[c] + lb_ref[c]
        a_ref[:, c * LANES:(c + 1) * LANES] = _silu(yn).astype(BF16)
    a = a_ref[...]
    for jn in range(d // tn):
        sl = slice(jn * tn, (jn + 1) * tn)
        o_ref[:, sl] = res_ref[:, sl] + jnp.dot(a, wo_ref[:, sl], preferred_element_type=F32)


def conv_layer(x, gain, w_in, dw_w, dw_b, ln_g, ln_b, w_out, *, tm=256):
    s, d = x.shape
    ch = dw_w.shape[1]
    ns = ch // LANES
    halo = 32
    xg = resident_matmul(x, w_in.astype(BF16), ch, w_offs=(0, ch), gain=gain, act="glu", name="conv_in")
    strips = lambda a: a.reshape(-1, ns, LANES).transpose(1, 0, 2)
    w_s = strips(jnp.pad(dw_w, ((0, halo - CONV_K), (0, 0))))
    const2 = lambda i: (0, 0)
    const3 = lambda i: (0, 0, 0)
    row = lambda i: (i, 0)
    return pl.pallas_call(
        functools.partial(_conv_kernel, tm=tm, halo=halo, tn=512),
        out_shape=jax.ShapeDtypeStruct((s, d), F32),
        grid=(s // tm,),
        in_specs=[
            pl.BlockSpec((tm, ch), row),
            pl.BlockSpec((ns, halo, LANES), const3),
            pl.BlockSpec((ns, 1, LANES), const3),
            pl.BlockSpec((ns, 1, LANES), const3),
            pl.BlockSpec((ns, 1, LANES), const3),
            pl.BlockSpec((ch, d), const2, pipeline_mode=pl.Buffered(1)),
            pl.BlockSpec((tm, d), row),
        ],
        out_specs=pl.BlockSpec((tm, d), row),
        scratch_shapes=[pltpu.VMEM((ns, tm + halo, LANES), F32), pltpu.VMEM((ns, tm, LANES), F32),
                        pltpu.VMEM((tm, ch), BF16)],
        compiler_params=_params("arbitrary"),
        name="conv_mod",
    )(xg, w_s, strips(dw_b.reshape(1, ch)), strips(ln_g.reshape(1, ch)), strips(ln_b.reshape(1, ch)),
      w_out.astype(BF16), x)


def kernel(x, positions, norm_mix, norm_ffn, final_norm, mla_w_in, mla_q_norm, mla_w_uq, mla_kv_norm,
           mla_w_ukv, mla_w_o, gdn_w_in, gdn_conv_w, gdn_a_log, gdn_dt_bias, gdn_o_norm, gdn_w_o, s5_w_in,
           s5_lam_re, s5_lam_im, s5_log_dt, s5_b_re, s5_b_im, s5_c_re, s5_c_im, s5_d, s5_w_out, cv_w_in,
           cv_dw_w, cv_dw_b, cv_ln_g, cv_ln_b, cv_w_out, ffn_w_gate, ffn_w_up, ffn_w_down):
    bsz, s, d = x.shape
    depth = norm_mix.shape[0]
    wg_b, wu_b, wd_b = ffn_w_gate, ffn_w_up, ffn_w_down
    outs = []
    for b in range(bsz):
        h = x[b]
        pos = positions[b]
        for i in range(depth):
            m, j = i % 4, i // 4
            if m == 0:
                h = mla_layer(h, pos, norm_mix[i], mla_w_in[j], mla_q_norm[j], mla_w_uq[j], mla_kv_norm[j],
                              mla_w_ukv[j], mla_w_o[j])
            elif m == 1:
                h = gdn_layer(h, norm_mix[i], gdn_w_in[j], gdn_conv_w[j], gdn_a_log[j], gdn_dt_bias[j],
                              gdn_o_norm[j], gdn_w_o[j])
            elif m == 2:
                h = s5_layer(h, norm_mix[i], s5_w_in[j], s5_lam_re[j], s5_lam_im[j], s5_log_dt[j], s5_b_re[j],
                             s5_b_im[j], s5_c_re[j], s5_c_im[j], s5_d[j], s5_w_out[j])
            else:
                h = conv_layer(h, norm_mix[i], cv_w_in[j], cv_dw_w[j], cv_dw_b[j], cv_ln_g[j], cv_ln_b[j],
                               cv_w_out[j])
            fg = final_norm if i == depth - 1 else None
            h = ffn(h, norm_ffn[i], wg_b, wu_b, wd_b, fg, layer=i)
        outs.append(h)
    return outs[0][None] if bsz == 1 else jnp.stack(outs)
```

```python
import functools
import math

import jax
import jax.numpy as jnp
from jax import lax
from jax.experimental import pallas as pl
from jax.experimental.pallas import tpu as pltpu

F32 = jnp.float32
BF16 = jnp.bfloat16
EPS = 1e-6
NEG_INF = -1e30

VMEM_LIMIT_BYTES = 56 * 1024 * 1024
LANES = 128

MLA_HEADS = 16
MLA_Q_RANK = 512
MLA_KV_RANK = 512
MLA_NOPE = 128
MLA_ROPE = 64
MLA_V = 128
MLA_HEAD_PAD = 256
ROPE_THETA = 10000.0
MLA_QSCALE = (MLA_NOPE + MLA_ROPE) ** -0.5 * math.log2(math.e)

GDN_QK_HEADS = 16
GDN_V_HEADS = 32
GDN_DK = 128
GDN_DV = 128
GDN_CONV = 4
GDN_CHUNK = 64
GDN_SUPER = 256

S5_GROUP = 16
S5_STATE = 64
S5_SEGMENTS = 8

CONV_K = 31


def _params(*sem):
    return pltpu.CompilerParams(dimension_semantics=sem, vmem_limit_bytes=VMEM_LIMIT_BYTES)


def _rms(x, g):
    return x * lax.rsqrt(jnp.mean(x * x, axis=-1, keepdims=True) + EPS) * g


def _silu(x):
    return x * jax.nn.sigmoid(x)


def _mm_kernel(*refs, has_norm, n_w, act, has_res):
    it = iter(refs)
    a_ref = next(it)
    g_ref = next(it) if has_norm else None
    w_refs = [next(it) for _ in range(n_w)]
    res_ref = next(it) if has_res else None
    o_ref = next(it)
    xn_ref = next(it) if has_norm else None

    if has_norm:
        @pl.when(pl.program_id(1) == 0)
        def _():
            xn_ref[...] = _rms(a_ref[...], g_ref[...]).astype(BF16)

        a = xn_ref[...]
    else:
        a = a_ref[...]
    accs = [jnp.dot(a, w[...], preferred_element_type=F32) for w in w_refs]
    if act == "glu":
        y = accs[0] * jax.nn.sigmoid(accs[1])
    else:
        y = accs[0]
    if has_res:
        y = res_ref[...] + y
    o_ref[...] = y.astype(o_ref.dtype)


def fused_matmul(a, ws, *, gain=None, act="none", res=None, out_dtype=F32, tm=512, tn=512,
                 w_col_offsets=None, out_map=None, out_shape=None, a_block=None, a_map=None, name="mm"):
    k = ws[0][0].shape[0]
    m = a.size // k
    n = ws[0][1]
    n_w = len(ws)
    tm = min(tm, m)
    tn = min(tn, n)
    assert m % tm == 0 and n % tn == 0
    has_norm = gain is not None
    has_res = res is not None
    if w_col_offsets is None:
        w_col_offsets = [0] * n_w
    if a_block is None:
        a_block, a_map = (tm, k), (lambda i, j: (i, 0))
    in_specs = [pl.BlockSpec(a_block, a_map)]
    args = [a]
    if has_norm:
        in_specs.append(pl.BlockSpec((1, k), lambda i, j: (0, 0)))
        args.append(gain.reshape(1, k))
    for (w, _), off in zip(ws, w_col_offsets):
        assert off % tn == 0
        ob = off // tn
        in_specs.append(pl.BlockSpec((k, tn), lambda i, j, ob=ob: (0, j + ob)))
        args.append(w)
    if has_res:
        in_specs.append(pl.BlockSpec((tm, tn), lambda i, j: (i, j)))
        args.append(res)
    if out_map is None:
        out_map = lambda i, j: (i, j)
    if out_shape is None:
        out_shape = (m, n)
    scratch = [pltpu.VMEM((tm, k), BF16)] if has_norm else []
    return pl.pallas_call(
        functools.partial(_mm_kernel, has_norm=has_norm, n_w=n_w, act=act, has_res=has_res),
        out_shape=jax.ShapeDtypeStruct(out_shape, out_dtype),
        grid=(m // tm, n // tn),
        in_specs=in_specs,
        out_specs=pl.BlockSpec((tm, tn), out_map),
        scratch_shapes=scratch,
        compiler_params=_params("arbitrary", "arbitrary"),
        name=name,
    )(*args)


def _mm_resident_kernel(*refs, has_norm, w_offs, act, has_res, n, tn):
    it = iter(refs)
    a_ref = next(it)
    g_ref = next(it) if has_norm else None
    w_ref = next(it)
    res_ref = next(it) if has_res else None
    o_ref = next(it)
    a = _rms(a_ref[...], g_ref[...]).astype(BF16) if has_norm else a_ref[...]
    for jn in range(n // tn):
        sl = slice(jn * tn, (jn + 1) * tn)
        accs = [jnp.dot(a, w_ref[:, off + jn * tn:off + (jn + 1) * tn], preferred_element_type=F32)
                for off in w_offs]
        y = accs[0] * jax.nn.sigmoid(accs[1]) if act == "glu" else accs[0]
        if has_res:
            y = res_ref[:, sl] + y
        o_ref[:, sl] = y.astype(o_ref.dtype)


def resident_matmul(a, w, n, *, w_offs=(0,), gain=None, act="none", res=None, out_dtype=F32, tm=512, tn=512,
                    a_block=None, a_map=None, out_map=None, out_shape=None, name="mm"):
    k = w.shape[0]
    m = a.size // k
    tm = min(tm, m)
    tn = min(tn, n)
    assert m % tm == 0 and n % tn == 0
    has_norm = gain is not None
    has_res = res is not None
    row = lambda i: (i, 0)
    const = lambda i: (0, 0)
    in_specs = [pl.BlockSpec(a_block or (tm, k), a_map or row)]
    args = [a]
    if has_norm:
        in_specs.append(pl.BlockSpec((1, k), const))
        args.append(gain.reshape(1, k))
    in_specs.append(pl.BlockSpec(w.shape, const, pipeline_mode=pl.Buffered(1)))
    args.append(w)
    if has_res:
        in_specs.append(pl.BlockSpec((tm, n), row))
        args.append(res)
    return pl.pallas_call(
        functools.partial(_mm_resident_kernel, has_norm=has_norm, w_offs=tuple(w_offs), act=act,
                          has_res=has_res, n=n, tn=tn),
        out_shape=jax.ShapeDtypeStruct(out_shape or (m, n), out_dtype),
        grid=(m // tm,),
        in_specs=in_specs,
        out_specs=pl.BlockSpec((tm, n), out_map or row),
        compiler_params=_params("arbitrary"),
        name=name,
    )(*args)


def _ffn_kernel(*refs, final_norm):
    if final_norm:
        x_ref, g_ref, wg_ref, wu_ref, wd_ref, fg_ref, o_ref, xn_ref = refs
    else:
        x_ref, g_ref, wg_ref, wu_ref, wd_ref, o_ref, xn_ref = refs
        fg_ref = None
    j = pl.program_id(1)

    @pl.when(j == 0)
    def _():
        x = x_ref[...]
        xn_ref[...] = _rms(x, g_ref[...]).astype(BF16)
        o_ref[...] = x

    xn = xn_ref[...]
    hg = jnp.dot(xn, wg_ref[...].astype(BF16), preferred_element_type=F32)
    hu = jnp.dot(xn, wu_ref[...].astype(BF16), preferred_element_type=F32)
    h = (_silu(hg) * hu).astype(BF16)
    o_ref[...] += jnp.dot(h, wd_ref[...].astype(BF16), preferred_element_type=F32)

    if final_norm:
        @pl.when(j == pl.num_programs(1) - 1)
        def _():
            o_ref[...] = _rms(o_ref[...], fg_ref[...])


def ffn(x, gain, wg, wu, wd, final_gain=None, *, layer=0, tm=1024, th=256):
    m, d = x.shape
    hdim = wg.shape[2]
    assert m % tm == 0 and hdim % th == 0
    final_norm = final_gain is not None
    in_specs = [
        pl.BlockSpec((tm, d), lambda i, j: (i, 0)),
        pl.BlockSpec((1, d), lambda i, j: (0, 0)),
        pl.BlockSpec((None, d, th), lambda i, j: (layer, 0, j)),
        pl.BlockSpec((None, d, th), lambda i, j: (layer, 0, j)),
        pl.BlockSpec((None, th, d), lambda i, j: (layer, j, 0)),
    ]
    args = [x, gain.reshape(1, d), wg, wu, wd]
    if final_norm:
        in_specs.append(pl.BlockSpec((1, d), lambda i, j: (0, 0)))
        args.append(final_gain.reshape(1, d))
    return pl.pallas_call(
        functools.partial(_ffn_kernel, final_norm=final_norm),
        out_shape=jax.ShapeDtypeStruct((m, d), F32),
        grid=(m // tm, hdim // th),
        in_specs=in_specs,
        out_specs=pl.BlockSpec((tm, d), lambda i, j: (i, 0)),
        scratch_shapes=[pltpu.VMEM((tm, d), BF16)],
        compiler_params=_params("arbitrary", "arbitrary"),
        name="ffn",
    )(*args)


def _rope_block(xb, c, s1, s2):
    return xb * c + pltpu.roll(xb, 96, 1) * s1 + pltpu.roll(xb, 32, 1) * s2


def _mla_proj_kernel(x_ref, g_ref, pos_ref, invf_ref, win_ref, qn_ref, wuq_ref, kvn_ref, wukv_ref,
                     q_ref, k_ref, v_ref):
    xn = _rms(x_ref[...], g_ref[...]).astype(BF16)
    proj = jnp.dot(xn, win_ref[...], preferred_element_type=F32)
    qn = _rms(proj[:, :MLA_Q_RANK], qn_ref[...]).astype(BF16)
    kvn = _rms(proj[:, MLA_Q_RANK:MLA_Q_RANK + MLA_KV_RANK], kvn_ref[...]).astype(BF16)
    kr = proj[:, MLA_Q_RANK + MLA_KV_RANK:]

    ang = pos_ref[...].astype(F32) * invf_ref[...]
    cs = jnp.cos(ang)
    sn = jnp.sin(ang)
    lane = lax.broadcasted_iota(jnp.int32, ang.shape, 1)
    half = MLA_ROPE // 2
    c = jnp.where(lane < MLA_ROPE, cs, 0.0)
    s1 = jnp.where(lane < half, -sn, 0.0)
    s2 = jnp.where((lane >= half) & (lane < MLA_ROPE), sn, 0.0)

    k_rope = _rope_block(kr, c, s1, s2).astype(BF16)
    hp = MLA_HEAD_PAD
    for h in range(MLA_HEADS):
        qh = jnp.dot(qn, wuq_ref[:, h * hp:(h + 1) * hp], preferred_element_type=F32) * MLA_QSCALE
        q_ref[:, h * hp:h * hp + LANES] = qh[:, :LANES].astype(BF16)
        q_ref[:, h * hp + LANES:(h + 1) * hp] = _rope_block(qh[:, LANES:], c, s1, s2).astype(BF16)
        kvh = jnp.dot(kvn, wukv_ref[:, h * hp:(h + 1) * hp], preferred_element_type=F32)
        k_ref[:, h * hp:h * hp + LANES] = kvh[:, :LANES].astype(BF16)
        k_ref[:, h * hp + LANES:(h + 1) * hp] = k_rope
        v_ref[:, h * MLA_V:(h + 1) * MLA_V] = kvh[:, LANES:].astype(BF16)


def mla_proj(x, gain, pos, invf, w_in_p, q_norm, w_uq_p, kv_norm, w_ukv, *, tm=512):
    s, d = x.shape
    nq = MLA_HEADS * MLA_HEAD_PAD
    const = lambda i: (0, 0)
    return pl.pallas_call(
        _mla_proj_kernel,
        out_shape=(jax.ShapeDtypeStruct((s, nq), BF16), jax.ShapeDtypeStruct((s, nq), BF16),
                   jax.ShapeDtypeStruct((s, MLA_HEADS * MLA_V), BF16)),
        grid=(s // tm,),
        in_specs=[
            pl.BlockSpec((tm, d), lambda i: (i, 0)),
            pl.BlockSpec((1, d), const),
            pl.BlockSpec((tm, 1), lambda i: (i, 0)),
            pl.BlockSpec((1, LANES), const),
            pl.BlockSpec(w_in_p.shape, const, pipeline_mode=pl.Buffered(1)),
            pl.BlockSpec((1, MLA_Q_RANK), const),
            pl.BlockSpec(w_uq_p.shape, const, pipeline_mode=pl.Buffered(1)),
            pl.BlockSpec((1, MLA_KV_RANK), const),
            pl.BlockSpec(w_ukv.shape, const, pipeline_mode=pl.Buffered(1)),
        ],
        out_specs=(pl.BlockSpec((tm, nq), lambda i: (i, 0)), pl.BlockSpec((tm, nq), lambda i: (i, 0)),
                   pl.BlockSpec((tm, MLA_HEADS * MLA_V), lambda i: (i, 0))),
        compiler_params=_params("arbitrary"),
        name="mla_proj",
    )(x, gain.reshape(1, d), pos, invf, w_in_p, q_norm.reshape(1, -1), w_uq_p, kv_norm.reshape(1, -1), w_ukv)


def _flash_kernel(q_ref, k_ref, v_ref, o_ref, m_ref, l_ref, acc_ref, *, tq, tk):
    qi = pl.program_id(1)
    m_ref[...] = jnp.full(m_ref.shape, NEG_INF, F32)
    l_ref[...] = jnp.zeros(l_ref.shape, F32)
    acc_ref[...] = jnp.zeros(acc_ref.shape, F32)

    def step(start, masked, r0=0):
        k = k_ref[pl.ds(start, tk), :]
        v = v_ref[pl.ds(start, tk), :]
        s = lax.dot_general(q_ref[r0:, :], k, (((1,), (1,)), ((), ())), preferred_element_type=F32)
        if masked:
            row = lax.broadcasted_iota(jnp.int32, (tk, tk), 0)
            col = lax.broadcasted_iota(jnp.int32, (tk, tk), 1)
            tri = jnp.where(col <= row, s[:tk], NEG_INF)
            s = tri if s.shape[0] == tk else jnp.concatenate([tri, s[tk:]], axis=0)
        m_prev = m_ref[r0:, :]
        m_new = jnp.maximum(m_prev, jnp.max(s, axis=-1, keepdims=True))
        alpha = jnp.exp2(m_prev - m_new)
        l_new = alpha * l_ref[r0:, :]
        ps = []
        for c in range(tk // LANES):
            pc = jnp.exp2(s[:, c * LANES:(c + 1) * LANES] - m_new)
            l_new = l_new + pc
            ps.append(pc.astype(BF16))
        p = jnp.concatenate(ps, axis=1)
        acc_ref[r0:, :] = alpha * acc_ref[r0:, :] + jnp.dot(p, v, preferred_element_type=F32)
        l_ref[r0:, :] = l_new
        m_ref[r0:, :] = m_new

    nd = tq // tk

    def body(kp, carry):
        for u in range(nd):
            step(pl.multiple_of((kp * nd + u) * tk, tk), False)
        return carry

    lax.fori_loop(0, qi, body, 0)
    for dblk in range(nd):
        step(pl.multiple_of((qi * nd + dblk) * tk, tk), True, r0=dblk * tk)
    o_ref[...] = (acc_ref[...] / jnp.sum(l_ref[...], axis=-1, keepdims=True)).astype(o_ref.dtype)


def flash_attention(q, k, v, *, tq=2048, tk=512):
    s = q.shape[0]
    tq = min(tq, s)
    tk = min(tk, tq)
    hp, dv = MLA_HEAD_PAD, MLA_V
    return pl.pallas_call(
        functools.partial(_flash_kernel, tq=tq, tk=tk),
        out_shape=jax.ShapeDtypeStruct((s, MLA_HEADS * dv), BF16),
        grid=(MLA_HEADS, s // tq),
        in_specs=[
            pl.BlockSpec((tq, hp), lambda h, i: (i, h)),
            pl.BlockSpec((s, hp), lambda h, i: (0, h)),
            pl.BlockSpec((s, dv), lambda h, i: (0, h)),
        ],
        out_specs=pl.BlockSpec((tq, dv), lambda h, i: (i, h)),
        scratch_shapes=[pltpu.VMEM((tq, LANES), F32), pltpu.VMEM((tq, LANES), F32), pltpu.VMEM((tq, dv), F32)],
        compiler_params=_params("arbitrary", "arbitrary"),
        name="mla_flash",
    )(q, k, v)


def mla_layer(x, positions, gain, w_in, q_norm, w_uq, kv_norm, w_ukv, w_o):
    s, d = x.shape
    half = MLA_ROPE // 2
    inv = ROPE_THETA ** (-jnp.arange(half, dtype=F32) / half)
    invf = jnp.concatenate([inv, inv, jnp.zeros((LANES - MLA_ROPE,), F32)]).reshape(1, LANES)
    w_in_p = jnp.pad(w_in, ((0, 0), (0, LANES - MLA_ROPE))).astype(BF16)
    dq = MLA_NOPE + MLA_ROPE
    w_uq_p = jnp.pad(w_uq.reshape(MLA_Q_RANK, MLA_HEADS, dq), ((0, 0), (0, 0), (0, MLA_HEAD_PAD - dq)))
    w_uq_p = w_uq_p.reshape(MLA_Q_RANK, MLA_HEADS * MLA_HEAD_PAD).astype(BF16)
    q, k, v = mla_proj(x, gain, positions.reshape(s, 1), invf, w_in_p, q_norm, w_uq_p, kv_norm,
                       w_ukv.astype(BF16))
    o = flash_attention(q, k, v)
    return resident_matmul(o, w_o.astype(BF16), d, res=x, name="mla_out")


def _softplus(x):
    return jnp.maximum(x, 0.0) + jnp.log(1.0 + jnp.exp(-jnp.abs(x)))


def _gdn_proj_kernel(x_ref, g_ref, w_ref, cw_ref, wba_ref, alr_ref, dtr_ref,
                     o_ref, bg_ref, gt_ref, xn_ref, stage_ref, carry_ref, *, tm, tn, sub, n_qk, n_conv, qscale):
    i = pl.program_id(0)
    j = pl.program_id(1)
    nh = GDN_V_HEADS

    @pl.when(j == 0)
    def _():
        xn = _rms(x_ref[...], g_ref[...]).astype(BF16)
        xn_ref[...] = xn
        ba = lax.dot_general(xn, wba_ref[...], (((1,), (1,)), ((), ())),
                             preferred_element_type=F32)
        lane = lax.broadcasted_iota(jnp.int32, ba.shape, 1)
        gate = -jnp.exp(alr_ref[...]) * _softplus(ba + dtr_ref[...])
        bg_ref[...] = jnp.where(lane < nh, jax.nn.sigmoid(ba), gate)
        gt_ref[...] = gate.T[nh:2 * nh, :]

    nsub = tn // sub

    def proj(c):
        return lax.dot_general(xn_ref[...], w_ref[c * sub:(c + 1) * sub, :].astype(BF16),
                               (((1,), (1,)), ((), ())), preferred_element_type=F32)

    def conv_silu(c):
        acc = proj(c)
        outs = []
        for hh in range(sub // LANES):
            k = c * (sub // LANES) + hh
            ks = slice(k * LANES, (k + 1) * LANES)
            a = acc[:, hh * LANES:(hh + 1) * LANES]
            stage_ref[k, 0:8, :] = jnp.where(i == 0, 0.0, carry_ref[j, :, ks])
            stage_ref[k, 8:, :] = a
            carry_ref[j, :, ks] = a[tm - 8:, :]
            cw = cw_ref[:, ks]
            y = stage_ref[k, pl.ds(8 - (GDN_CONV - 1), tm), :] * cw[0:1, :]
            for t in range(1, GDN_CONV):
                y = y + stage_ref[k, pl.ds(8 - (GDN_CONV - 1) + t, tm), :] * cw[t:t + 1, :]
            outs.append((ks, _silu(y)))
        return outs

    @pl.when(j < n_qk)
    def _():
        sc = jnp.where(j < n_qk // 2, qscale, 1.0)
        for c in range(nsub):
            for ks, yb in conv_silu(c):
                nrm = lax.rsqrt(jnp.sum(yb * yb, axis=-1, keepdims=True) + EPS) * sc
                o_ref[:, ks] = (yb * nrm).astype(o_ref.dtype)

    @pl.when((j >= n_qk) & (j < n_conv))
    def _():
        for c in range(nsub):
            for ks, yb in conv_silu(c):
                o_ref[:, ks] = yb.astype(o_ref.dtype)

    @pl.when(j >= n_conv)
    def _():
        for c in range(nsub):
            o_ref[:, c * sub:(c + 1) * sub] = proj(c).astype(o_ref.dtype)


def gdn_proj(x, gain, w_qkvz, conv_w_p, w_ba, a_log, dt_bias, *, tm=1024, tn=1024, sub=256):
    s, d = x.shape
    qk_dim = GDN_QK_HEADS * GDN_DK
    v_dim = GDN_V_HEADS * GDN_DV
    n = 2 * qk_dim + 2 * v_dim
    assert n % tn == 0 and (2 * qk_dim) % tn == 0 and v_dim % tn == 0 and tn % sub == 0
    n_qk = 2 * qk_dim // tn
    n_conv = (2 * qk_dim + v_dim) // tn
    nh = GDN_V_HEADS
    pad = jnp.zeros((nh,), F32)
    alr = jnp.concatenate([pad, a_log, pad, pad]).reshape(1, LANES)
    dtr = jnp.concatenate([pad, dt_bias, pad, pad]).reshape(1, LANES)
    const = lambda i, j: (0, 0)
    kern = functools.partial(_gdn_proj_kernel, tm=tm, tn=tn, sub=sub, n_qk=n_qk, n_conv=n_conv,
                             qscale=GDN_DK ** -0.5)
    return pl.pallas_call(
        kern,
        out_shape=(jax.ShapeDtypeStruct((s, n), BF16), jax.ShapeDtypeStruct((s, LANES), F32),
                   jax.ShapeDtypeStruct((nh, s), F32)),
        grid=(s // tm, n // tn),
        in_specs=[
            pl.BlockSpec((tm, d), lambda i, j: (i, 0)),
            pl.BlockSpec((1, d), const),
            pl.BlockSpec((tn, d), lambda i, j: (j, 0)),
            pl.BlockSpec((GDN_CONV, tn), lambda i, j: (0, jnp.minimum(j, n_conv - 1))),
            pl.BlockSpec((LANES, d), const),
            pl.BlockSpec((1, LANES), const),
            pl.BlockSpec((1, LANES), const),
        ],
        out_specs=(pl.BlockSpec((tm, tn), lambda i, j: (i, j)),
                   pl.BlockSpec((tm, LANES), lambda i, j: (i, 0)),
                   pl.BlockSpec((nh, tm), lambda i, j: (0, i))),
        scratch_shapes=[pltpu.VMEM((tm, d), BF16), pltpu.VMEM((tn // LANES, tm + 8, LANES), F32),
                        pltpu.VMEM((n_conv, 8, tn), F32)],
        compiler_params=_params("arbitrary", "arbitrary"),
        name="gdn_proj",
    )(x, gain.reshape(1, d), w_qkvz, conv_w_p, w_ba, alr, dtr)


def _gdn_delta_kernel(q_ref, k_ref, v_ref, z_ref, bg_ref, gt_ref, on_ref, o_ref, state_ref, mask_ref, tri_ref,
                      gtc_ref, *, hq, c):
    hb = pl.program_id(0)
    t = pl.program_id(1)
    nlev = c.bit_length() - 1
    row = lax.broadcasted_iota(jnp.int32, (c, c), 0)
    col = lax.broadcasted_iota(jnp.int32, (c, c), 1)

    @pl.when((hb == 0) & (t == 0))
    def _():
        for l in range(nlev):
            m = ((row >> (l + 1)) == (col >> (l + 1))) & (((row >> l) & 1) == 1) & (((col >> l) & 1) == 0)
            mask_ref[l] = jnp.where(m, 1.0, 0.0).astype(BF16)
        tri_ref[...] = jnp.where(col <= row, 1.0, 0.0).astype(BF16)

    @pl.when(t == 0)
    def _():
        state_ref[...] = jnp.zeros(state_ref.shape, F32)

    tri = tri_ref[...]
    incl = col <= row
    strict = col < row
    bg = bg_ref[...]
    lane = lax.broadcasted_iota(jnp.int32, bg.shape, 1)
    nt = (((1,), (1,)), ((), ()))

    def split3(x):
        x1 = x.astype(BF16)
        r1 = x - x1.astype(F32)
        x2 = r1.astype(BF16)
        return x1, x2, (r1 - x2.astype(F32)).astype(BF16)

    gall = sum(jnp.dot(tri, p, preferred_element_type=F32) for p in split3(bg))
    gtc_ref[...] = sum(lax.dot_general(p, tri, nt, preferred_element_type=F32) for p in split3(gt_ref[...]))

    nb = 2 * hq
    qs, ks, kfs, amats, attns, betas, gcs, glasts = [], [], [], [], [], [], [], []
    for a in range(hq):
        q = q_ref[:, a * GDN_DK:(a + 1) * GDN_DK]
        k = k_ref[:, a * GDN_DK:(a + 1) * GDN_DK]
        kk = lax.dot_general(k, k, nt, preferred_element_type=F32)
        qk = lax.dot_general(q, k, nt, preferred_element_type=F32)
        for b in range(2):
            vh = (hb * hq + a) * 2 + b
            beta = jnp.sum(jnp.where(lane == vh, bg, 0.0), axis=-1, keepdims=True)
            gc_col = jnp.sum(jnp.where(lane == vh + GDN_V_HEADS, gall, 0.0), axis=-1, keepdims=True)
            gc_row = gtc_ref[pl.ds(vh, 1), :]
            dec = jnp.exp(jnp.where(incl, gc_col - gc_row, NEG_INF))
            amats.append(jnp.where(strict, kk * dec, 0.0) * beta)
            attns.append((qk * dec).astype(BF16))
            qs.append(q)
            ks.append(k)
            betas.append(beta)
            gcs.append(gc_col)
            glasts.append(gc_col[c - 1:c, :])
    eye = jnp.where(row == col, 1.0, 0.0).astype(BF16)
    abs_ = [amats[i].astype(BF16) for i in range(nb)]
    ts = [eye - abs_[i] * mask_ref[0] for i in range(nb)]
    for l in range(1, nlev):
        s = 1 << l
        if s < 16:
            xs = [jnp.dot(abs_[i] * mask_ref[l], ts[i], preferred_element_type=F32).astype(BF16)
                  for i in range(nb)]
            ts = [ts[i] - jnp.dot(ts[i], xs[i], preferred_element_type=F32).astype(BF16) for i in range(nb)]
            continue
        odd = [slice((2 * k + 1) * s, (2 * k + 2) * s) for k in range(c // (2 * s))]
        even = [slice(2 * k * s, (2 * k + 1) * s) for k in range(c // (2 * s))]
        pick = lambda a, sls: jnp.concatenate([a[sl] for sl in sls], axis=0) if len(sls) > 1 else a[sls[0]]
        zero = jnp.zeros((s, c), BF16)
        new_ts = []
        m_odd = jnp.concatenate([mask_ref[l, sl, :] for sl in odd], axis=0) if len(odd) > 1 else mask_ref[l, odd[0], :]
        e_odd = [pick(abs_[i], odd) * m_odd for i in range(nb)]
        x_odd = [jnp.dot(e_odd[i], ts[i], preferred_element_type=F32).astype(BF16) for i in range(nb)]
        for i in range(nb):
            x_full = jnp.concatenate([p for k in range(len(odd)) for p in (zero, x_odd[i][k * s:(k + 1) * s])],
                                     axis=0)
            t_odd = pick(ts[i], odd)
            t_odd = t_odd - jnp.dot(t_odd, x_full, preferred_element_type=F32).astype(BF16)
            new_ts.append(jnp.concatenate(
                [p for k in range(len(odd)) for p in (ts[i][even[k]], t_odd[k * s:(k + 1) * s])], axis=0))
        ts = new_ts
    rs = [ts[i] - eye for i in range(nb)]
    egs = [jnp.exp(gcs[i]) for i in range(nb)]
    kfs = [ks[i].astype(F32) for i in range(nb)]
    rhss = [jnp.concatenate([v_ref[:, i * GDN_DV:(i + 1) * GDN_DV].astype(F32) * betas[i],
                             kfs[i] * (betas[i] * egs[i])], axis=1) for i in range(nb)]
    uws = [rhss[i] + jnp.dot(rs[i], rhss[i].astype(BF16), preferred_element_type=F32)
           for i in range(nb)]
    sts = [state_ref[i] for i in range(nb)]
    wss = [jnp.dot(jnp.concatenate([uws[i][:, GDN_DV:].astype(BF16),
                                    (qs[i].astype(F32) * egs[i]).astype(BF16)], axis=0),
                   sts[i].astype(BF16), preferred_element_type=F32) for i in range(nb)]
    vnbs = [(uws[i][:, :GDN_DV] - wss[i][:c]).astype(BF16) for i in range(nb)]
    os_ = [wss[i][c:] + jnp.dot(attns[i], vnbs[i], preferred_element_type=F32) for i in range(nb)]
    for i in range(nb):
        kdec = (kfs[i] * jnp.exp(glasts[i] - gcs[i])).astype(BF16)
        state_ref[i] = sts[i] * jnp.exp(glasts[i]) + lax.dot_general(
            kdec, vnbs[i], (((0,), (0,)), ((), ())), preferred_element_type=F32)
        z = z_ref[:, i * GDN_DV:(i + 1) * GDN_DV].astype(F32)
        o_ref[:, i * GDN_DV:(i + 1) * GDN_DV] = (_rms(os_[i], on_ref[...]) * _silu(z)).astype(o_ref.dtype)


def gdn_delta(qkvz, bg, gt, o_norm, *, hq=8, c=GDN_SUPER):
    s = qkvz.shape[0]
    qk_dim = GDN_QK_HEADS * GDN_DK
    v_dim = GDN_V_HEADS * GDN_DV
    wq = hq * GDN_DK
    wv = 2 * hq * GDN_DV
    nlev = c.bit_length() - 1
    return pl.pallas_call(
        functools.partial(_gdn_delta_kernel, hq=hq, c=c),
        out_shape=jax.ShapeDtypeStruct((s, v_dim), BF16),
        grid=(GDN_QK_HEADS // hq, s // c),
        in_specs=[
            pl.BlockSpec((c, wq), lambda h, t: (t, h)),
            pl.BlockSpec((c, wq), lambda h, t: (t, qk_dim // wq + h)),
            pl.BlockSpec((c, wv), lambda h, t: (t, 2 * qk_dim // wv + h)),
            pl.BlockSpec((c, wv), lambda h, t: (t, (2 * qk_dim + v_dim) // wv + h)),
            pl.BlockSpec((c, LANES), lambda h, t: (t, 0)),
            pl.BlockSpec((GDN_V_HEADS, c), lambda h, t: (0, t)),
            pl.BlockSpec((1, GDN_DV), lambda h, t: (0, 0)),
        ],
        out_specs=pl.BlockSpec((c, wv), lambda h, t: (t, h)),
        scratch_shapes=[pltpu.VMEM((2 * hq, GDN_DK, GDN_DV), F32), pltpu.VMEM((nlev, c, c), BF16),
                        pltpu.VMEM((c, c), BF16), pltpu.VMEM((GDN_V_HEADS, c), F32)],
        compiler_params=_params("arbitrary", "arbitrary"),
        name="gdn_delta",
    )(qkvz, qkvz, qkvz, qkvz, bg, gt, o_norm.reshape(1, GDN_DV))


def gdn_layer(x, gain, w_in, conv_w, a_log, dt_bias, o_norm, w_o):
    s, d = x.shape
    qk_dim = GDN_QK_HEADS * GDN_DK
    v_dim = GDN_V_HEADS * GDN_DV
    n_main = 2 * qk_dim + 2 * v_dim
    nh = GDN_V_HEADS
    w_t = w_in.T
    w_ba = jnp.pad(w_t[n_main:], ((0, LANES - 2 * nh), (0, 0))).astype(BF16)
    qkvz, bg, gt = gdn_proj(x, gain, w_t, conv_w, w_ba, a_log, dt_bias)
    o = gdn_delta(qkvz, bg, gt, o_norm)
    return resident_matmul(o, w_o.astype(BF16), d, res=x, name="gdn_out")


def _s5_disc_kernel(lre_ref, lim_ref, ldt_ref, lre_e_ref, lim_e_ref, bre_ref, bim_ref,
                    lbre_ref, lbim_ref, bbre_ref, bbim_ref):
    dt = jnp.exp(ldt_ref[...])

    def zoh(lre, lim):
        mag = jnp.exp(lre * dt)
        ang = lim * dt
        lb_re = mag * jnp.cos(ang)
        lb_im = mag * jnp.sin(ang)
        den = lre * lre + lim * lim
        nr = lb_re - 1.0
        f_re = (nr * lre + lb_im * lim) / den
        f_im = (lb_im * lre - nr * lim) / den
        return lb_re, lb_im, f_re, f_im

    lb_re, lb_im, _, _ = zoh(lre_ref[...], lim_ref[...])
    lbre_ref[...] = lb_re
    lbim_ref[...] = lb_im
    _, _, f_re, f_im = zoh(lre_e_ref[...], lim_e_ref[...])
    bbre_ref[...] = f_re * bre_ref[...] - f_im * bim_ref[...]
    bbim_ref[...] = f_re * bim_ref[...] + f_im * bre_ref[...]


def _s5_scan_kernel(u_ref, bblk_ref, cblk_ref, are_ref, aim_ref, d_ref, o_ref, st_ref, bu_ref, x_ref,
                    *, lt, n_sq):
    p = pl.program_id(1)
    tb = pl.program_id(2)
    last = pl.num_programs(2) - 1
    nseg = S5_SEGMENTS
    half = bu_ref.shape[1] // 2

    @pl.when((p == 0) & (tb == 0))
    def _():
        st_ref[...] = jnp.zeros(st_ref.shape, F32)

    u = jnp.swapaxes(u_ref[...], 0, 1).reshape(lt * nseg, u_ref.shape[2])
    bu_ref[...] = jnp.dot(u.astype(BF16), bblk_ref[0], preferred_element_type=F32)
    ar = jnp.broadcast_to(are_ref[0], (nseg, half))
    ai = jnp.broadcast_to(aim_ref[0], (nseg, half))

    def body(tau, carry):
        xr, xi = carry
        r0 = pl.multiple_of(tau * nseg, nseg)
        b = bu_ref[pl.ds(r0, nseg), :]
        nxr = ar * xr - ai * xi + b[:, :half]
        nxi = ar * xi + ai * xr + b[:, half:]
        x_ref[pl.ds(r0, nseg), :half] = nxr
        x_ref[pl.ds(r0, nseg), half:] = nxi
        return nxr, nxi

    st = st_ref[...]
    xr, xi = lax.fori_loop(0, lt, body, (st[:, :half], st[:, half:]), unroll=8)
    st_ref[:, :half] = xr
    st_ref[:, half:] = xi

    @pl.when((p == 0) & (tb == last))
    def _():
        pr, pi = are_ref[0], aim_ref[0]
        for _ in range(n_sq):
            pr, pi = pr * pr - pi * pi, 2.0 * pr * pi
        sr = jnp.zeros((1, half), F32)
        si = jnp.zeros((1, half), F32)
        rows_r, rows_i = [sr], [si]
        for r in range(nseg - 1):
            sr, si = pr * sr - pi * si + xr[r:r + 1, :], pr * si + pi * sr + xi[r:r + 1, :]
            rows_r.append(sr)
            rows_i.append(si)
        st_ref[:, :half] = jnp.concatenate(rows_r, axis=0)
        st_ref[:, half:] = jnp.concatenate(rows_i, axis=0)

    @pl.when(p == 1)
    def _():
        y = jnp.dot(x_ref[...].astype(BF16), cblk_ref[0], preferred_element_type=F32) + d_ref[...] * u
        yg = jax.nn.gelu(y).reshape(lt, nseg, y.shape[1])
        o_ref[...] = jnp.swapaxes(yg, 0, 1).astype(o_ref.dtype)


def s5_layer(x, gain, w_in, lam_re, lam_im, log_dt, b_re, b_im, c_re, c_im, d_skip, w_out):
    s, d = x.shape
    width = w_in.shape[1]
    g, p_st, gc = b_re.shape
    nseg = S5_SEGMENTS
    sseg = s // nseg
    assert sseg & (sseg - 1) == 0
    tm = min(512, sseg)
    nb = sseg // tm
    ncb = width // LANES
    gpb = LANES // gc
    half = gpb * p_st

    u = resident_matmul(x, w_in.astype(BF16), width, gain=gain, tm=tm, name="s5_in")
    u3 = u.reshape(nseg, sseg, width)

    rep = lambda a: jnp.repeat(a, gc, axis=1)
    vm = pl.BlockSpec(memory_space=pltpu.VMEM)
    lb_re, lb_im, bb_re, bb_im = pl.pallas_call(
        _s5_disc_kernel,
        out_shape=(jax.ShapeDtypeStruct((g, p_st), F32), jax.ShapeDtypeStruct((g, p_st), F32),
                   jax.ShapeDtypeStruct((g, p_st * gc), F32), jax.ShapeDtypeStruct((g, p_st * gc), F32)),
        in_specs=[vm] * 7, out_specs=(vm, vm, vm, vm), name="s5_disc",
    )(lam_re, lam_im, log_dt.reshape(g, 1), rep(lam_re), rep(lam_im),
      b_re.reshape(g, p_st * gc), b_im.reshape(g, p_st * gc))

    eye = jnp.eye(gpb, dtype=F32)

    def in_blocks(bb):
        t = bb.reshape(ncb, gpb, p_st, gc)
        return jnp.einsum("bgpc,gh->bgchp", t, eye).reshape(ncb, gpb * gc, gpb * p_st)

    def out_blocks(cc):
        t = cc.reshape(ncb, gpb, gc, p_st)
        return jnp.einsum("bgcp,gh->bhpgc", t, eye).reshape(ncb, gpb * p_st, gpb * gc)

    bblk = jnp.concatenate([in_blocks(bb_re), in_blocks(bb_im)], axis=2).astype(BF16)
    cblk = jnp.concatenate([out_blocks(c_re), -out_blocks(c_im)], axis=1).astype(BF16)
    a_re = lb_re.reshape(ncb, 1, half)
    a_im = lb_im.reshape(ncb, 1, half)

    lt = min(128, sseg)
    rows = lt * nseg
    yg = pl.pallas_call(
        functools.partial(_s5_scan_kernel, lt=lt, n_sq=sseg.bit_length() - 1),
        out_shape=jax.ShapeDtypeStruct((nseg, sseg, width), BF16),
        grid=(ncb, 2, sseg // lt),
        in_specs=[
            pl.BlockSpec((nseg, lt, LANES), lambda cb, p, t: (0, t, cb)),
            pl.BlockSpec((1, LANES, 2 * half), lambda cb, p, t: (cb, 0, 0)),
            pl.BlockSpec((1, 2 * half, LANES), lambda cb, p, t: (cb, 0, 0)),
            pl.BlockSpec((1, 1, half), lambda cb, p, t: (cb, 0, 0)),
            pl.BlockSpec((1, 1, half), lambda cb, p, t: (cb, 0, 0)),
            pl.BlockSpec((1, LANES), lambda cb, p, t: (0, cb)),
        ],
        out_specs=pl.BlockSpec((nseg, lt, LANES), lambda cb, p, t: (0, t * p, cb)),
        scratch_shapes=[pltpu.VMEM((nseg, 2 * half), F32), pltpu.VMEM((rows, 2 * half), F32),
                        pltpu.VMEM((rows, 2 * half), F32)],
        compiler_params=_params("arbitrary", "arbitrary", "arbitrary"),
        name="s5_scan",
    )(u3, bblk, cblk, a_re, a_im, d_skip.reshape(1, width))

    return resident_matmul(yg.reshape(s, width), w_out.astype(BF16), d, w_offs=(0, d), act="glu", res=x, tm=tm,
                           name="s5_out")


def _conv_kernel(xg_ref, w_ref, b_ref, lg_ref, lb_ref, wo_ref, res_ref, o_ref, stage_ref, y_ref, a_ref,
                 *, tm, halo, tn):
    i = pl.program_id(0)
    ns = stage_ref.shape[0]
    d = ns * LANES
    for c in range(ns):
        prev = stage_ref[c, tm:tm + halo, :]
        stage_ref[c, 0:halo, :] = jnp.where(i == 0, 0.0, prev)
        stage_ref[c, halo:, :] = xg_ref[:, c * LANES:(c + 1) * LANES]

    def strip(c, carry):
        w = w_ref[c]
        acc = stage_ref[c, pl.ds(halo - (CONV_K - 1), tm), :] * w[0:1, :]
        for k in range(1, CONV_K):
            acc = acc + stage_ref[c, pl.ds(halo - (CONV_K - 1) + k, tm), :] * w[k:k + 1, :]
        y_ref[c] = acc + b_ref[c]
        return carry

    lax.fori_loop(0, ns, strip, 0)

    tot = y_ref[0]
    for c in range(1, ns):
        tot = tot + y_ref[c]
    mu = jnp.sum(tot, axis=-1, keepdims=True) * (1.0 / d)
    sq = jnp.square(y_ref[0] - mu)
    for c in range(1, ns):
        sq = sq + jnp.square(y_ref[c] - mu)
    rstd = lax.rsqrt(jnp.sum(sq, axis=-1, keepdims=True) * (1.0 / d) + EPS)
    for c in range(ns):
        yn = (y_ref[c] - mu) * rstd * lg_ref[c] + lb_ref[c]
        a_ref[:, c * LANES:(c + 1) * LANES] = _silu(yn).astype(BF16)
    a = a_ref[...]
    for jn in range(d // tn):
        sl = slice(jn * tn, (jn + 1) * tn)
        o_ref[:, sl] = res_ref[:, sl] + jnp.dot(a, wo_ref[:, sl], preferred_element_type=F32)


def conv_layer(x, gain, w_in, dw_w, dw_b, ln_g, ln_b, w_out, *, tm=256):
    s, d = x.shape
    ch = dw_w.shape[1]
    ns = ch // LANES
    halo = 32
    xg = resident_matmul(x, w_in.astype(BF16), ch, w_offs=(0, ch), gain=gain, act="glu", name="conv_in")
    strips = lambda a: a.reshape(-1, ns, LANES).transpose(1, 0, 2)
    w_s = strips(jnp.pad(dw_w, ((0, halo - CONV_K), (0, 0))))
    const2 = lambda i: (0, 0)
    const3 = lambda i: (0, 0, 0)
    row = lambda i: (i, 0)
    return pl.pallas_call(
        functools.partial(_conv_kernel, tm=tm, halo=halo, tn=512),
        out_shape=jax.ShapeDtypeStruct((s, d), F32),
        grid=(s // tm,),
        in_specs=[
            pl.BlockSpec((tm, ch), row),
            pl.BlockSpec((ns, halo, LANES), const3),
            pl.BlockSpec((ns, 1, LANES), const3),
            pl.BlockSpec((ns, 1, LANES), const3),
            pl.BlockSpec((ns, 1, LANES), const3),
            pl.BlockSpec((ch, d), const2, pipeline_mode=pl.Buffered(1)),
            pl.BlockSpec((tm, d), row),
        ],
        out_specs=pl.BlockSpec((tm, d), row),
        scratch_shapes=[pltpu.VMEM((ns, tm + halo, LANES), F32), pltpu.VMEM((ns, tm, LANES), F32),
                        pltpu.VMEM((tm, ch), BF16)],
        compiler_params=_params("arbitrary"),
        name="conv_mod",
    )(xg, w_s, strips(dw_b.reshape(1, ch)), strips(ln_g.reshape(1, ch)), strips(ln_b.reshape(1, ch)),
      w_out.astype(BF16), x)


def kernel(x, positions, norm_mix, norm_ffn, final_norm, mla_w_in, mla_q_norm, mla_w_uq, mla_kv_norm,
           mla_w_ukv, mla_w_o, gdn_w_in, gdn_conv_w, gdn_a_log, gdn_dt_bias, gdn_o_norm, gdn_w_o, s5_w_in,
           s5_lam_re, s5_lam_im, s5_log_dt, s5_b_re, s5_b_im, s5_c_re, s5_c_im, s5_d, s5_w_out, cv_w_in,
           cv_dw_w, cv_dw_b, cv_ln_g, cv_ln_b, cv_w_out, ffn_w_gate, ffn_w_up, ffn_w_down):
    bsz, s, d = x.shape
    depth = norm_mix.shape[0]
    wg_b, wu_b, wd_b = ffn_w_gate, ffn_w_up, ffn_w_down
    outs = []
    for b in range(bsz):
        h = x[b]
        pos = positions[b]
        for i in range(depth):
            m, j = i % 4, i // 4
            if m == 0:
                h = mla_layer(h, pos, norm_mix[i], mla_w_in[j], mla_q_norm[j], mla_w_uq[j], mla_kv_norm[j],
                              mla_w_ukv[j], mla_w_o[j])
            elif m == 1:
                h = gdn_layer(h, norm_mix[i], gdn_w_in[j], gdn_conv_w[j], gdn_a_log[j], gdn_dt_bias[j],
                              gdn_o_norm[j], gdn_w_o[j])
            elif m == 2:
                h = s5_layer(h, norm_mix[i], s5_w_in[j], s5_lam_re[j], s5_lam_im[j], s5_log_dt[j], s5_b_re[j],
                             s5_b_im[j], s5_c_re[j], s5_c_im[j], s5_d[j], s5_w_out[j])
            else:
                h = conv_layer(h, norm_mix[i], cv_w_in[j], cv_dw_w[j], cv_dw_b[j], cv_ln_g[j], cv_ln_b[j],
                               cv_w_out[j])
            fg = final_norm if i == depth - 1 else None
            h = ffn(h, norm_ffn[i], wg_b, wu_b, wd_b, fg, layer=i)
        outs.append(h)
    return outs[0][None] if bsz == 1 else jnp.stack(outs)
```

```python
import functools
import math

import jax
import jax.numpy as jnp
from jax import lax
from jax.experimental import pallas as pl
from jax.experimental.pallas import tpu as pltpu

F32 = jnp.float32
BF16 = jnp.bfloat16
EPS = 1e-6
NEG_INF = -1e30

VMEM_LIMIT_BYTES = 56 * 1024 * 1024
LANES = 128

MLA_HEADS = 16
MLA_Q_RANK = 512
MLA_KV_RANK = 512
MLA_NOPE = 128
MLA_ROPE = 64
MLA_V = 128
MLA_HEAD_PAD = 256
ROPE_THETA = 10000.0
MLA_QSCALE = (MLA_NOPE + MLA_ROPE) ** -0.5 * math.log2(math.e)

GDN_QK_HEADS = 16
GDN_V_HEADS = 32
GDN_DK = 128
GDN_DV = 128
GDN_CONV = 4
GDN_CHUNK = 64
GDN_SUPER = 256

S5_GROUP = 16
S5_STATE = 64
S5_SEGMENTS = 8

CONV_K = 31


def _params(*sem):
    return pltpu.CompilerParams(dimension_semantics=sem, vmem_limit_bytes=VMEM_LIMIT_BYTES)


def _rms(x, g):
    return x * lax.rsqrt(jnp.mean(x * x, axis=-1, keepdims=True) + EPS) * g


def _silu(x):
    return x * jax.nn.sigmoid(x)


def _mm_kernel(*refs, has_norm, n_w, act, has_res):
    it = iter(refs)
    a_ref = next(it)
    g_ref = next(it) if has_norm else None
    w_refs = [next(it) for _ in range(n_w)]
    res_ref = next(it) if has_res else None
    o_ref = next(it)
    xn_ref = next(it) if has_norm else None

    if has_norm:
        @pl.when(pl.program_id(1) == 0)
        def _():
            xn_ref[...] = _rms(a_ref[...], g_ref[...]).astype(BF16)

        a = xn_ref[...]
    else:
        a = a_ref[...]
    accs = [jnp.dot(a, w[...], preferred_element_type=F32) for w in w_refs]
    if act == "glu":
        y = accs[0] * jax.nn.sigmoid(accs[1])
    else:
        y = accs[0]
    if has_res:
        y = res_ref[...] + y
    o_ref[...] = y.astype(o_ref.dtype)


def fused_matmul(a, ws, *, gain=None, act="none", res=None, out_dtype=F32, tm=512, tn=512,
                 w_col_offsets=None, out_map=None, out_shape=None, a_block=None, a_map=None, name="mm"):
    k = ws[0][0].shape[0]
    m = a.size // k
    n = ws[0][1]
    n_w = len(ws)
    tm = min(tm, m)
    tn = min(tn, n)
    assert m % tm == 0 and n % tn == 0
    has_norm = gain is not None
    has_res = res is not None
    if w_col_offsets is None:
        w_col_offsets = [0] * n_w
    if a_block is None:
        a_block, a_map = (tm, k), (lambda i, j: (i, 0))
    in_specs = [pl.BlockSpec(a_block, a_map)]
    args = [a]
    if has_norm:
        in_specs.append(pl.BlockSpec((1, k), lambda i, j: (0, 0)))
        args.append(gain.reshape(1, k))
    for (w, _), off in zip(ws, w_col_offsets):
        assert off % tn == 0
        ob = off // tn
        in_specs.append(pl.BlockSpec((k, tn), lambda i, j, ob=ob: (0, j + ob)))
        args.append(w)
    if has_res:
        in_specs.append(pl.BlockSpec((tm, tn), lambda i, j: (i, j)))
        args.append(res)
    if out_map is None:
        out_map = lambda i, j: (i, j)
    if out_shape is None:
        out_shape = (m, n)
    scratch = [pltpu.VMEM((tm, k), BF16)] if has_norm else []
    return pl.pallas_call(
        functools.partial(_mm_kernel, has_norm=has_norm, n_w=n_w, act=act, has_res=has_res),
        out_shape=jax.ShapeDtypeStruct(out_shape, out_dtype),
        grid=(m // tm, n // tn),
        in_specs=in_specs,
        out_specs=pl.BlockSpec((tm, tn), out_map),
        scratch_shapes=scratch,
        compiler_params=_params("arbitrary", "arbitrary"),
        name=name,
    )(*args)


def _mm_resident_kernel(*refs, has_norm, w_offs, act, has_res, n, tn):
    it = iter(refs)
    a_ref = next(it)
    g_ref = next(it) if has_norm else None
    w_ref = next(it)
    res_ref = next(it) if has_res else None
    o_ref = next(it)
    a = _rms(a_ref[...], g_ref[...]).astype(BF16) if has_norm else a_ref[...]
    for jn in range(n // tn):
        sl = slice(jn * tn, (jn + 1) * tn)
        accs = [jnp.dot(a, w_ref[:, off + jn * tn:off + (jn + 1) * tn], preferred_element_type=F32)
                for off in w_offs]
        y = accs[0] * jax.nn.sigmoid(accs[1]) if act == "glu" else accs[0]
        if has_res:
            y = res_ref[:, sl] + y
        o_ref[:, sl] = y.astype(o_ref.dtype)


def resident_matmul(a, w, n, *, w_offs=(0,), gain=None, act="none", res=None, out_dtype=F32, tm=512, tn=512,
                    a_block=None, a_map=None, out_map=None, out_shape=None, name="mm"):
    k = w.shape[0]
    m = a.size // k
    tm = min(tm, m)
    tn = min(tn, n)
    assert m % tm == 0 and n % tn == 0
    has_norm = gain is not None
    has_res = res is not None
    row = lambda i: (i, 0)
    const = lambda i: (0, 0)
    in_specs = [pl.BlockSpec(a_block or (tm, k), a_map or row)]
    args = [a]
    if has_norm:
        in_specs.append(pl.BlockSpec((1, k), const))
        args.append(gain.reshape(1, k))
    in_specs.append(pl.BlockSpec(w.shape, const, pipeline_mode=pl.Buffered(1)))
    args.append(w)
    if has_res:
        in_specs.append(pl.BlockSpec((tm, n), row))
        args.append(res)
    return pl.pallas_call(
        functools.partial(_mm_resident_kernel, has_norm=has_norm, w_offs=tuple(w_offs), act=act,
                          has_res=has_res, n=n, tn=tn),
        out_shape=jax.ShapeDtypeStruct(out_shape or (m, n), out_dtype),
        grid=(m // tm,),
        in_specs=in_specs,
        out_specs=pl.BlockSpec((tm, n), out_map or row),
        compiler_params=_params("arbitrary"),
        name=name,
    )(*args)


def _ffn_kernel(*refs, final_norm):
    if final_norm:
        x_ref, g_ref, wg_ref, wu_ref, wd_ref, fg_ref, o_ref, xn_ref = refs
    else:
        x_ref, g_ref, wg_ref, wu_ref, wd_ref, o_ref, xn_ref = refs
        fg_ref = None
    j = pl.program_id(1)

    @pl.when(j == 0)
    def _():
        x = x_ref[...]
        xn_ref[...] = _rms(x, g_ref[...]).astype(BF16)
        o_ref[...] = x

    xn = xn_ref[...]
    hg = jnp.dot(xn, wg_ref[...].astype(BF16), preferred_element_type=F32)
    hu = jnp.dot(xn, wu_ref[...].astype(BF16), preferred_element_type=F32)
    h = (_silu(hg) * hu).astype(BF16)
    o_ref[...] += jnp.dot(h, wd_ref[...].astype(BF16), preferred_element_type=F32)

    if final_norm:
        @pl.when(j == pl.num_programs(1) - 1)
        def _():
            o_ref[...] = _rms(o_ref[...], fg_ref[...])


def ffn(x, gain, wg, wu, wd, final_gain=None, *, layer=0, tm=1024, th=256):
    m, d = x.shape
    hdim = wg.shape[2]
    assert m % tm == 0 and hdim % th == 0
    final_norm = final_gain is not None
    in_specs = [
        pl.BlockSpec((tm, d), lambda i, j: (i, 0)),
        pl.BlockSpec((1, d), lambda i, j: (0, 0)),
        pl.BlockSpec((None, d, th), lambda i, j: (layer, 0, j)),
        pl.BlockSpec((None, d, th), lambda i, j: (layer, 0, j)),
        pl.BlockSpec((None, th, d), lambda i, j: (layer, j, 0)),
    ]
    args = [x, gain.reshape(1, d), wg, wu, wd]
    if final_norm:
        in_specs.append(pl.BlockSpec((1, d), lambda i, j: (0, 0)))
        args.append(final_gain.reshape(1, d))
    return pl.pallas_call(
        functools.partial(_ffn_kernel, final_norm=final_norm),
        out_shape=jax.ShapeDtypeStruct((m, d), F32),
        grid=(m // tm, hdim // th),
        in_specs=in_specs,
        out_specs=pl.BlockSpec((tm, d), lambda i, j: (i, 0)),
        scratch_shapes=[pltpu.VMEM((tm, d), BF16)],
        compiler_params=_params("arbitrary", "arbitrary"),
        name="ffn",
    )(*args)


def _rope_block(xb, c, s1, s2):
    return xb * c + pltpu.roll(xb, 96, 1) * s1 + pltpu.roll(xb, 32, 1) * s2


def _mla_proj_kernel(x_ref, g_ref, pos_ref, invf_ref, win_ref, qn_ref, wuq_ref, kvn_ref, wukv_ref,
                     q_ref, k_ref, v_ref):
    xn = _rms(x_ref[...], g_ref[...]).astype(BF16)
    proj = lax.dot_general(xn, win_ref[...], (((1,), (1,)), ((), ())),
                           preferred_element_type=F32)
    qn = _rms(proj[:, :MLA_Q_RANK], qn_ref[...]).astype(BF16)
    kvn = _rms(proj[:, MLA_Q_RANK:MLA_Q_RANK + MLA_KV_RANK], kvn_ref[...]).astype(BF16)
    kr = proj[:, MLA_Q_RANK + MLA_KV_RANK:]

    ang = pos_ref[...].astype(F32) * invf_ref[...]
    cs = jnp.cos(ang)
    sn = jnp.sin(ang)
    lane = lax.broadcasted_iota(jnp.int32, ang.shape, 1)
    half = MLA_ROPE // 2
    c = jnp.where(lane < MLA_ROPE, cs, 0.0)
    s1 = jnp.where(lane < half, -sn, 0.0)
    s2 = jnp.where((lane >= half) & (lane < MLA_ROPE), sn, 0.0)

    k_rope = _rope_block(kr, c, s1, s2).astype(BF16)
    hp = MLA_HEAD_PAD
    for h in range(MLA_HEADS):
        qh = jnp.dot(qn, wuq_ref[:, h * hp:(h + 1) * hp], preferred_element_type=F32) * MLA_QSCALE
        q_ref[:, h * hp:h * hp + LANES] = qh[:, :LANES].astype(BF16)
        q_ref[:, h * hp + LANES:(h + 1) * hp] = _rope_block(qh[:, LANES:], c, s1, s2).astype(BF16)
        kvh = jnp.dot(kvn, wukv_ref[:, h * hp:(h + 1) * hp], preferred_element_type=F32)
        k_ref[:, h * hp:h * hp + LANES] = kvh[:, :LANES].astype(BF16)
        k_ref[:, h * hp + LANES:(h + 1) * hp] = k_rope
        v_ref[:, h * MLA_V:(h + 1) * MLA_V] = kvh[:, LANES:].astype(BF16)


def mla_proj(x, gain, pos, invf, w_in_p, q_norm, w_uq_p, kv_norm, w_ukv, *, tm=512):
    s, d = x.shape
    nq = MLA_HEADS * MLA_HEAD_PAD
    const = lambda i: (0, 0)
    return pl.pallas_call(
        _mla_proj_kernel,
        out_shape=(jax.ShapeDtypeStruct((s, nq), BF16), jax.ShapeDtypeStruct((s, nq), BF16),
                   jax.ShapeDtypeStruct((s, MLA_HEADS * MLA_V), BF16)),
        grid=(s // tm,),
        in_specs=[
            pl.BlockSpec((tm, d), lambda i: (i, 0)),
            pl.BlockSpec((1, d), const),
            pl.BlockSpec((tm, 1), lambda i: (i, 0)),
            pl.BlockSpec((1, LANES), const),
            pl.BlockSpec(w_in_p.shape, const, pipeline_mode=pl.Buffered(1)),
            pl.BlockSpec((1, MLA_Q_RANK), const),
            pl.BlockSpec(w_uq_p.shape, const, pipeline_mode=pl.Buffered(1)),
            pl.BlockSpec((1, MLA_KV_RANK), const),
            pl.BlockSpec(w_ukv.shape, const, pipeline_mode=pl.Buffered(1)),
        ],
        out_specs=(pl.BlockSpec((tm, nq), lambda i: (i, 0)), pl.BlockSpec((tm, nq), lambda i: (i, 0)),
                   pl.BlockSpec((tm, MLA_HEADS * MLA_V), lambda i: (i, 0))),
        compiler_params=_params("arbitrary"),
        name="mla_proj",
    )(x, gain.reshape(1, d), pos, invf, w_in_p, q_norm.reshape(1, -1), w_uq_p, kv_norm.reshape(1, -1), w_ukv)


def _flash_kernel(q_ref, k_ref, v_ref, o_ref, m_ref, l_ref, acc_ref, *, tq, tk):
    qi = pl.program_id(1)
    m_ref[...] = jnp.full(m_ref.shape, NEG_INF, F32)
    l_ref[...] = jnp.zeros(l_ref.shape, F32)
    acc_ref[...] = jnp.zeros(acc_ref.shape, F32)

    def step(start, masked, r0=0):
        k = k_ref[pl.ds(start, tk), :]
        v = v_ref[pl.ds(start, tk), :]
        s = lax.dot_general(q_ref[r0:, :], k, (((1,), (1,)), ((), ())), preferred_element_type=F32)
        if masked:
            row = lax.broadcasted_iota(jnp.int32, (tk, tk), 0)
            col = lax.broadcasted_iota(jnp.int32, (tk, tk), 1)
            tri = jnp.where(col <= row, s[:tk], NEG_INF)
            s = tri if s.shape[0] == tk else jnp.concatenate([tri, s[tk:]], axis=0)
        m_prev = m_ref[r0:, :]
        m_new = jnp.maximum(m_prev, jnp.max(s, axis=-1, keepdims=True))
        alpha = jnp.exp2(m_prev - m_new)
        l_new = alpha * l_ref[r0:, :]
        ps = []
        for c in range(tk // LANES):
            pc = jnp.exp2(s[:, c * LANES:(c + 1) * LANES] - m_new)
            l_new = l_new + pc
            ps.append(pc.astype(BF16))
        p = jnp.concatenate(ps, axis=1)
        acc_ref[r0:, :] = alpha * acc_ref[r0:, :] + jnp.dot(p, v, preferred_element_type=F32)
        l_ref[r0:, :] = l_new
        m_ref[r0:, :] = m_new

    nd = tq // tk

    def body(kp, carry):
        for u in range(nd):
            step(pl.multiple_of((kp * nd + u) * tk, tk), False)
        return carry

    lax.fori_loop(0, qi, body, 0)
    for dblk in range(nd):
        step(pl.multiple_of((qi * nd + dblk) * tk, tk), True, r0=dblk * tk)
    o_ref[...] = (acc_ref[...] / jnp.sum(l_ref[...], axis=-1, keepdims=True)).astype(o_ref.dtype)


def flash_attention(q, k, v, *, tq=2048, tk=512):
    s = q.shape[0]
    tq = min(tq, s)
    tk = min(tk, tq)
    hp, dv = MLA_HEAD_PAD, MLA_V
    return pl.pallas_call(
        functools.partial(_flash_kernel, tq=tq, tk=tk),
        out_shape=jax.ShapeDtypeStruct((s, MLA_HEADS * dv), BF16),
        grid=(MLA_HEADS, s // tq),
        in_specs=[
            pl.BlockSpec((tq, hp), lambda h, i: (i, h)),
            pl.BlockSpec((s, hp), lambda h, i: (0, h)),
            pl.BlockSpec((s, dv), lambda h, i: (0, h)),
        ],
        out_specs=pl.BlockSpec((tq, dv), lambda h, i: (i, h)),
        scratch_shapes=[pltpu.VMEM((tq, LANES), F32), pltpu.VMEM((tq, LANES), F32), pltpu.VMEM((tq, dv), F32)],
        compiler_params=_params("arbitrary", "arbitrary"),
        name="mla_flash",
    )(q, k, v)


def mla_layer(x, positions, gain, w_in, q_norm, w_uq, kv_norm, w_ukv, w_o):
    s, d = x.shape
    half = MLA_ROPE // 2
    inv = ROPE_THETA ** (-jnp.arange(half, dtype=F32) / half)
    invf = jnp.concatenate([inv, inv, jnp.zeros((LANES - MLA_ROPE,), F32)]).reshape(1, LANES)
    w_in_p = jnp.pad(w_in.T, ((0, LANES - MLA_ROPE), (0, 0))).astype(BF16)
    dq = MLA_NOPE + MLA_ROPE
    w_uq_p = jnp.pad(w_uq.reshape(MLA_Q_RANK, MLA_HEADS, dq), ((0, 0), (0, 0), (0, MLA_HEAD_PAD - dq)))
    w_uq_p = w_uq_p.reshape(MLA_Q_RANK, MLA_HEADS * MLA_HEAD_PAD).astype(BF16)
    q, k, v = mla_proj(x, gain, positions.reshape(s, 1), invf, w_in_p, q_norm, w_uq_p, kv_norm,
                       w_ukv.astype(BF16))
    o = flash_attention(q, k, v)
    return resident_matmul(o, w_o.astype(BF16), d, res=x, name="mla_out")


def _softplus(x):
    return jnp.maximum(x, 0.0) + jnp.log(1.0 + jnp.exp(-jnp.abs(x)))


def _gdn_proj_kernel(x_ref, g_ref, w_ref, cw_ref, wba_ref, alr_ref, dtr_ref,
                     o_ref, bg_ref, gt_ref, xn_ref, stage_ref, carry_ref, *, tm, tn, sub, n_qk, n_conv, qscale):
    i = pl.program_id(0)
    j = pl.program_id(1)
    nh = GDN_V_HEADS

    @pl.when(j == 0)
    def _():
        xn = _rms(x_ref[...], g_ref[...]).astype(BF16)
        xn_ref[...] = xn
        ba = lax.dot_general(xn, wba_ref[...], (((1,), (1,)), ((), ())),
                             preferred_element_type=F32)
        lane = lax.broadcasted_iota(jnp.int32, ba.shape, 1)
        gate = -jnp.exp(alr_ref[...]) * _softplus(ba + dtr_ref[...])
        bg_ref[...] = jnp.where(lane < nh, jax.nn.sigmoid(ba), gate)
        gt_ref[...] = gate.T[nh:2 * nh, :]

    nsub = tn // sub

    def proj(c):
        return lax.dot_general(xn_ref[...], w_ref[c * sub:(c + 1) * sub, :].astype(BF16),
                               (((1,), (1,)), ((), ())), preferred_element_type=F32)

    def conv_silu(c):
        acc = proj(c)
        outs = []
        for hh in range(sub // LANES):
            k = c * (sub // LANES) + hh
            ks = slice(k * LANES, (k + 1) * LANES)
            a = acc[:, hh * LANES:(hh + 1) * LANES]
            stage_ref[k, 0:8, :] = jnp.where(i == 0, 0.0, carry_ref[j, :, ks])
            stage_ref[k, 8:, :] = a
            carry_ref[j, :, ks] = a[tm - 8:, :]
            cw = cw_ref[:, ks]
            y = stage_ref[k, pl.ds(8 - (GDN_CONV - 1), tm), :] * cw[0:1, :]
            for t in range(1, GDN_CONV):
                y = y + stage_ref[k, pl.ds(8 - (GDN_CONV - 1) + t, tm), :] * cw[t:t + 1, :]
            outs.append((ks, _silu(y)))
        return outs

    @pl.when(j < n_qk)
    def _():
        sc = jnp.where(j < n_qk // 2, qscale, 1.0)
        for c in range(nsub):
            for ks, yb in conv_silu(c):
                nrm = lax.rsqrt(jnp.sum(yb * yb, axis=-1, keepdims=True) + EPS) * sc
                o_ref[:, ks] = (yb * nrm).astype(o_ref.dtype)

    @pl.when((j >= n_qk) & (j < n_conv))
    def _():
        for c in range(nsub):
            for ks, yb in conv_silu(c):
                o_ref[:, ks] = yb.astype(o_ref.dtype)

    @pl.when(j >= n_conv)
    def _():
        for c in range(nsub):
            o_ref[:, c * sub:(c + 1) * sub] = proj(c).astype(o_ref.dtype)


def gdn_proj(x, gain, w_qkvz, conv_w_p, w_ba, a_log, dt_bias, *, tm=1024, tn=1024, sub=256):
    s, d = x.shape
    qk_dim = GDN_QK_HEADS * GDN_DK
    v_dim = GDN_V_HEADS * GDN_DV
    n = 2 * qk_dim + 2 * v_dim
    assert n % tn == 0 and (2 * qk_dim) % tn == 0 and v_dim % tn == 0 and tn % sub == 0
    n_qk = 2 * qk_dim // tn
    n_conv = (2 * qk_dim + v_dim) // tn
    nh = GDN_V_HEADS
    pad = jnp.zeros((nh,), F32)
    alr = jnp.concatenate([pad, a_log, pad, pad]).reshape(1, LANES)
    dtr = jnp.concatenate([pad, dt_bias, pad, pad]).reshape(1, LANES)
    const = lambda i, j: (0, 0)
    kern = functools.partial(_gdn_proj_kernel, tm=tm, tn=tn, sub=sub, n_qk=n_qk, n_conv=n_conv,
                             qscale=GDN_DK ** -0.5)
    return pl.pallas_call(
        kern,
        out_shape=(jax.ShapeDtypeStruct((s, n), BF16), jax.ShapeDtypeStruct((s, LANES), F32),
                   jax.ShapeDtypeStruct((nh, s), F32)),
        grid=(s // tm, n // tn),
        in_specs=[
            pl.BlockSpec((tm, d), lambda i, j: (i, 0)),
            pl.BlockSpec((1, d), const),
            pl.BlockSpec((tn, d), lambda i, j: (j, 0)),
            pl.BlockSpec((GDN_CONV, tn), lambda i, j: (0, jnp.minimum(j, n_conv - 1))),
            pl.BlockSpec((LANES, d), const),
            pl.BlockSpec((1, LANES), const),
            pl.BlockSpec((1, LANES), const),
        ],
        out_specs=(pl.BlockSpec((tm, tn), lambda i, j: (i, j)),
                   pl.BlockSpec((tm, LANES), lambda i, j: (i, 0)),
                   pl.BlockSpec((nh, tm), lambda i, j: (0, i))),
        scratch_shapes=[pltpu.VMEM((tm, d), BF16), pltpu.VMEM((tn // LANES, tm + 8, LANES), F32),
                        pltpu.VMEM((n_conv, 8, tn), F32)],
        compiler_params=_params("arbitrary", "arbitrary"),
        name="gdn_proj",
    )(x, gain.reshape(1, d), w_qkvz, conv_w_p, w_ba, alr, dtr)


def _gdn_delta_kernel(q_ref, k_ref, v_ref, z_ref, bg_ref, gt_ref, on_ref, o_ref, state_ref, mask_ref, tri_ref,
                      gtc_ref, *, hq, c):
    hb = pl.program_id(0)
    t = pl.program_id(1)
    nlev = c.bit_length() - 1
    row = lax.broadcasted_iota(jnp.int32, (c, c), 0)
    col = lax.broadcasted_iota(jnp.int32, (c, c), 1)

    @pl.when((hb == 0) & (t == 0))
    def _():
        for l in range(nlev):
            m = ((row >> (l + 1)) == (col >> (l + 1))) & (((row >> l) & 1) == 1) & (((col >> l) & 1) == 0)
            mask_ref[l] = jnp.where(m, 1.0, 0.0).astype(BF16)
        tri_ref[...] = jnp.where(col <= row, 1.0, 0.0).astype(BF16)

    @pl.when(t == 0)
    def _():
        state_ref[...] = jnp.zeros(state_ref.shape, F32)

    tri = tri_ref[...]
    incl = col <= row
    strict = col < row
    bg = bg_ref[...]
    lane = lax.broadcasted_iota(jnp.int32, bg.shape, 1)
    nt = (((1,), (1,)), ((), ()))

    def split3(x):
        x1 = x.astype(BF16)
        r1 = x - x1.astype(F32)
        x2 = r1.astype(BF16)
        return x1, x2, (r1 - x2.astype(F32)).astype(BF16)

    gall = sum(jnp.dot(tri, p, preferred_element_type=F32) for p in split3(bg))
    gtc_ref[...] = sum(lax.dot_general(p, tri, nt, preferred_element_type=F32) for p in split3(gt_ref[...]))

    nb = 2 * hq
    qs, ks, kfs, amats, attns, betas, gcs, glasts = [], [], [], [], [], [], [], []
    for a in range(hq):
        q = q_ref[:, a * GDN_DK:(a + 1) * GDN_DK]
        k = k_ref[:, a * GDN_DK:(a + 1) * GDN_DK]
        kk = lax.dot_general(k, k, nt, preferred_element_type=F32)
        qk = lax.dot_general(q, k, nt, preferred_element_type=F32)
        for b in range(2):
            vh = (hb * hq + a) * 2 + b
            beta = jnp.sum(jnp.where(lane == vh, bg, 0.0), axis=-1, keepdims=True)
            gc_col = jnp.sum(jnp.where(lane == vh + GDN_V_HEADS, gall, 0.0), axis=-1, keepdims=True)
            gc_row = gtc_ref[pl.ds(vh, 1), :]
            dec = jnp.exp(jnp.where(incl, gc_col - gc_row, NEG_INF))
            amats.append(jnp.where(strict, kk * dec, 0.0) * beta)
            attns.append((qk * dec).astype(BF16))
            qs.append(q)
            ks.append(k)
            betas.append(beta)
            gcs.append(gc_col)
            glasts.append(gc_col[c - 1:c, :])
    eye = jnp.where(row == col, 1.0, 0.0).astype(BF16)
    abs_ = [amats[i].astype(BF16) for i in range(nb)]
    ts = [eye - abs_[i] * mask_ref[0] for i in range(nb)]
    for l in range(1, nlev):
        s = 1 << l
        if s < 16:
            xs = [jnp.dot(abs_[i] * mask_ref[l], ts[i], preferred_element_type=F32).astype(BF16)
                  for i in range(nb)]
            ts = [ts[i] - jnp.dot(ts[i], xs[i], preferred_element_type=F32).astype(BF16) for i in range(nb)]
            continue
        odd = [slice((2 * k + 1) * s, (2 * k + 2) * s) for k in range(c // (2 * s))]
        even = [slice(2 * k * s, (2 * k + 1) * s) for k in range(c // (2 * s))]
        pick = lambda a, sls: jnp.concatenate([a[sl] for sl in sls], axis=0) if len(sls) > 1 else a[sls[0]]
        zero = jnp.zeros((s, c), BF16)
        new_ts = []
        m_odd = jnp.concatenate([mask_ref[l, sl, :] for sl in odd], axis=0) if len(odd) > 1 else mask_ref[l, odd[0], :]
        e_odd = [pick(abs_[i], odd) * m_odd for i in range(nb)]
        x_odd = [jnp.dot(e_odd[i], ts[i], preferred_element_type=F32).astype(BF16) for i in range(nb)]
        for i in range(nb):
            x_full = jnp.concatenate([p for k in range(len(odd)) for p in (zero, x_odd[i][k * s:(k + 1) * s])],
                                     axis=0)
            t_odd = pick(ts[i], odd)
            t_odd = t_odd - jnp.dot(t_odd, x_full, preferred_element_type=F32).astype(BF16)
            new_ts.append(jnp.concatenate(
                [p for k in range(len(odd)) for p in (ts[i][even[k]], t_odd[k * s:(k + 1) * s])], axis=0))
        ts = new_ts
    rs = [ts[i] - eye for i in range(nb)]
    egs = [jnp.exp(gcs[i]) for i in range(nb)]
    kfs = [ks[i].astype(F32) for i in range(nb)]
    rhss = [jnp.concatenate([v_ref[:, i * GDN_DV:(i + 1) * GDN_DV].astype(F32) * betas[i],
                             kfs[i] * (betas[i] * egs[i])], axis=1) for i in range(nb)]
    uws = [rhss[i] + jnp.dot(rs[i], rhss[i].astype(BF16), preferred_element_type=F32)
           for i in range(nb)]
    sts = [state_ref[i] for i in range(nb)]
    wss = [jnp.dot(jnp.concatenate([uws[i][:, GDN_DV:].astype(BF16),
                                    (qs[i].astype(F32) * egs[i]).astype(BF16)], axis=0),
                   sts[i].astype(BF16), preferred_element_type=F32) for i in range(nb)]
    vnbs = [(uws[i][:, :GDN_DV] - wss[i][:c]).astype(BF16) for i in range(nb)]
    os_ = [wss[i][c:] + jnp.dot(attns[i], vnbs[i], preferred_element_type=F32) for i in range(nb)]
    for i in range(nb):
        kdec = (kfs[i] * jnp.exp(glasts[i] - gcs[i])).astype(BF16)
        state_ref[i] = sts[i] * jnp.exp(glasts[i]) + lax.dot_general(
            kdec, vnbs[i], (((0,), (0,)), ((), ())), preferred_element_type=F32)
        z = z_ref[:, i * GDN_DV:(i + 1) * GDN_DV].astype(F32)
        o_ref[:, i * GDN_DV:(i + 1) * GDN_DV] = (_rms(os_[i], on_ref[...]) * _silu(z)).astype(o_ref.dtype)


def gdn_delta(qkvz, bg, gt, o_norm, *, hq=8, c=GDN_SUPER):
    s = qkvz.shape[0]
    qk_dim = GDN_QK_HEADS * GDN_DK
    v_dim = GDN_V_HEADS * GDN_DV
    wq = hq * GDN_DK
    wv = 2 * hq * GDN_DV
    nlev = c.bit_length() - 1
    return pl.pallas_call(
        functools.partial(_gdn_delta_kernel, hq=hq, c=c),
        out_shape=jax.ShapeDtypeStruct((s, v_dim), BF16),
        grid=(GDN_QK_HEADS // hq, s // c),
        in_specs=[
            pl.BlockSpec((c, wq), lambda h, t: (t, h)),
            pl.BlockSpec((c, wq), lambda h, t: (t, qk_dim // wq + h)),
            pl.BlockSpec((c, wv), lambda h, t: (t, 2 * qk_dim // wv + h)),
            pl.BlockSpec((c, wv), lambda h, t: (t, (2 * qk_dim + v_dim) // wv + h)),
            pl.BlockSpec((c, LANES), lambda h, t: (t, 0)),
            pl.BlockSpec((GDN_V_HEADS, c), lambda h, t: (0, t)),
            pl.BlockSpec((1, GDN_DV), lambda h, t: (0, 0)),
        ],
        out_specs=pl.BlockSpec((c, wv), lambda h, t: (t, h)),
        scratch_shapes=[pltpu.VMEM((2 * hq, GDN_DK, GDN_DV), F32), pltpu.VMEM((nlev, c, c), BF16),
                        pltpu.VMEM((c, c), BF16), pltpu.VMEM((GDN_V_HEADS, c), F32)],
        compiler_params=_params("arbitrary", "arbitrary"),
        name="gdn_delta",
    )(qkvz, qkvz, qkvz, qkvz, bg, gt, o_norm.reshape(1, GDN_DV))


def gdn_layer(x, gain, w_in, conv_w, a_log, dt_bias, o_norm, w_o):
    s, d = x.shape
    qk_dim = GDN_QK_HEADS * GDN_DK
    v_dim = GDN_V_HEADS * GDN_DV
    n_main = 2 * qk_dim + 2 * v_dim
    nh = GDN_V_HEADS
    w_t = w_in.T
    w_ba = jnp.pad(w_t[n_main:], ((0, LANES - 2 * nh), (0, 0))).astype(BF16)
    qkvz, bg, gt = gdn_proj(x, gain, w_t, conv_w, w_ba, a_log, dt_bias)
    o = gdn_delta(qkvz, bg, gt, o_norm)
    return resident_matmul(o, w_o.astype(BF16), d, res=x, name="gdn_out")


def _s5_disc_kernel(lre_ref, lim_ref, ldt_ref, lre_e_ref, lim_e_ref, bre_ref, bim_ref,
                    lbre_ref, lbim_ref, bbre_ref, bbim_ref):
    dt = jnp.exp(ldt_ref[...])

    def zoh(lre, lim):
        mag = jnp.exp(lre * dt)
        ang = lim * dt
        lb_re = mag * jnp.cos(ang)
        lb_im = mag * jnp.sin(ang)
        den = lre * lre + lim * lim
        nr = lb_re - 1.0
        f_re = (nr * lre + lb_im * lim) / den
        f_im = (lb_im * lre - nr * lim) / den
        return lb_re, lb_im, f_re, f_im

    lb_re, lb_im, _, _ = zoh(lre_ref[...], lim_ref[...])
    lbre_ref[...] = lb_re
    lbim_ref[...] = lb_im
    _, _, f_re, f_im = zoh(lre_e_ref[...], lim_e_ref[...])
    bbre_ref[...] = f_re * bre_ref[...] - f_im * bim_ref[...]
    bbim_ref[...] = f_re * bim_ref[...] + f_im * bre_ref[...]


def _s5_scan_kernel(u_ref, bblk_ref, cblk_ref, are_ref, aim_ref, d_ref, o_ref, st_ref, bu_ref, x_ref,
                    *, lt, n_sq):
    p = pl.program_id(1)
    tb = pl.program_id(2)
    last = pl.num_programs(2) - 1
    nseg = S5_SEGMENTS
    half = bu_ref.shape[1] // 2

    @pl.when((p == 0) & (tb == 0))
    def _():
        st_ref[...] = jnp.zeros(st_ref.shape, F32)

    u = jnp.swapaxes(u_ref[...], 0, 1).reshape(lt * nseg, u_ref.shape[2])
    bu_ref[...] = jnp.dot(u.astype(BF16), bblk_ref[0], preferred_element_type=F32)
    ar = jnp.broadcast_to(are_ref[0], (nseg, half))
    ai = jnp.broadcast_to(aim_ref[0], (nseg, half))

    def body(tau, carry):
        xr, xi = carry
        r0 = pl.multiple_of(tau * nseg, nseg)
        b = bu_ref[pl.ds(r0, nseg), :]
        nxr = ar * xr - ai * xi + b[:, :half]
        nxi = ar * xi + ai * xr + b[:, half:]
        x_ref[pl.ds(r0, nseg), :half] = nxr
        x_ref[pl.ds(r0, nseg), half:] = nxi
        return nxr, nxi

    st = st_ref[...]
    xr, xi = lax.fori_loop(0, lt, body, (st[:, :half], st[:, half:]), unroll=8)
    st_ref[:, :half] = xr
    st_ref[:, half:] = xi

    @pl.when((p == 0) & (tb == last))
    def _():
        pr, pi = are_ref[0], aim_ref[0]
        for _ in range(n_sq):
            pr, pi = pr * pr - pi * pi, 2.0 * pr * pi
        sr = jnp.zeros((1, half), F32)
        si = jnp.zeros((1, half), F32)
        rows_r, rows_i = [sr], [si]
        for r in range(nseg - 1):
            sr, si = pr * sr - pi * si + xr[r:r + 1, :], pr * si + pi * sr + xi[r:r + 1, :]
            rows_r.append(sr)
            rows_i.append(si)
        st_ref[:, :half] = jnp.concatenate(rows_r, axis=0)
        st_ref[:, half:] = jnp.concatenate(rows_i, axis=0)

    @pl.when(p == 1)
    def _():
        y = jnp.dot(x_ref[...].astype(BF16), cblk_ref[0], preferred_element_type=F32) + d_ref[...] * u
        yg = jax.nn.gelu(y).reshape(lt, nseg, y.shape[1])
        o_ref[...] = jnp.swapaxes(yg, 0, 1).astype(o_ref.dtype)


def s5_layer(x, gain, w_in, lam_re, lam_im, log_dt, b_re, b_im, c_re, c_im, d_skip, w_out):
    s, d = x.shape
    width = w_in.shape[1]
    g, p_st, gc = b_re.shape
    nseg = S5_SEGMENTS
    sseg = s // nseg
    assert sseg & (sseg - 1) == 0
    tm = min(512, sseg)
    nb = sseg // tm
    ncb = width // LANES
    gpb = LANES // gc
    half = gpb * p_st

    u = resident_matmul(x, w_in.astype(BF16), width, gain=gain, tm=tm, name="s5_in")
    u3 = u.reshape(nseg, sseg, width)

    rep = lambda a: jnp.repeat(a, gc, axis=1)
    vm = pl.BlockSpec(memory_space=pltpu.VMEM)
    lb_re, lb_im, bb_re, bb_im = pl.pallas_call(
        _s5_disc_kernel,
        out_shape=(jax.ShapeDtypeStruct((g, p_st), F32), jax.ShapeDtypeStruct((g, p_st), F32),
                   jax.ShapeDtypeStruct((g, p_st * gc), F32), jax.ShapeDtypeStruct((g, p_st * gc), F32)),
        in_specs=[vm] * 7, out_specs=(vm, vm, vm, vm), name="s5_disc",
    )(lam_re, lam_im, log_dt.reshape(g, 1), rep(lam_re), rep(lam_im),
      b_re.reshape(g, p_st * gc), b_im.reshape(g, p_st * gc))

    eye = jnp.eye(gpb, dtype=F32)

    def in_blocks(bb):
        t = bb.reshape(ncb, gpb, p_st, gc)
        return jnp.einsum("bgpc,gh->bgchp", t, eye).reshape(ncb, gpb * gc, gpb * p_st)

    def out_blocks(cc):
        t = cc.reshape(ncb, gpb, gc, p_st)
        return jnp.einsum("bgcp,gh->bhpgc", t, eye).reshape(ncb, gpb * p_st, gpb * gc)

    bblk = jnp.concatenate([in_blocks(bb_re), in_blocks(bb_im)], axis=2).astype(BF16)
    cblk = jnp.concatenate([out_blocks(c_re), -out_blocks(c_im)], axis=1).astype(BF16)
    a_re = lb_re.reshape(ncb, 1, half)
    a_im = lb_im.reshape(ncb, 1, half)

    lt = min(256, sseg)
    rows = lt * nseg
    yg = pl.pallas_call(
        functools.partial(_s5_scan_kernel, lt=lt, n_sq=sseg.bit_length() - 1),
        out_shape=jax.ShapeDtypeStruct((nseg, sseg, width), BF16),
        grid=(ncb, 2, sseg // lt),
        in_specs=[
            pl.BlockSpec((nseg, lt, LANES), lambda cb, p, t: (0, t, cb)),
            pl.BlockSpec((1, LANES, 2 * half), lambda cb, p, t: (cb, 0, 0)),
            pl.BlockSpec((1, 2 * half, LANES), lambda cb, p, t: (cb, 0, 0)),
            pl.BlockSpec((1, 1, half), lambda cb, p, t: (cb, 0, 0)),
            pl.BlockSpec((1, 1, half), lambda cb, p, t: (cb, 0, 0)),
            pl.BlockSpec((1, LANES), lambda cb, p, t: (0, cb)),
        ],
        out_specs=pl.BlockSpec((nseg, lt, LANES), lambda cb, p, t: (0, t * p, cb)),
        scratch_shapes=[pltpu.VMEM((nseg, 2 * half), F32), pltpu.VMEM((rows, 2 * half), F32),
                        pltpu.VMEM((rows, 2 * half), F32)],
        compiler_params=_params("arbitrary", "arbitrary", "arbitrary"),
        name="s5_scan",
    )(u3, bblk, cblk, a_re, a_im, d_skip.reshape(1, width))

    return resident_matmul(yg.reshape(s, width), w_out.astype(BF16), d, w_offs=(0, d), act="glu", res=x, tm=tm,
                           name="s5_out")


def _conv_kernel(xg_ref, w_ref, b_ref, lg_ref, lb_ref, wo_ref, res_ref, o_ref, stage_ref, y_ref, a_ref,
                 *, tm, halo, tn):
    i = pl.program_id(0)
    ns = stage_ref.shape[0]
    d = ns * LANES
    for c in range(ns):
        prev = stage_ref[c, tm:tm + halo, :]
        stage_ref[c, 0:halo, :] = jnp.where(i == 0, 0.0, prev)
        stage_ref[c, halo:, :] = xg_ref[:, c * LANES:(c + 1) * LANES]

    def strip(c, carry):
        w = w_ref[c]
        acc = stage_ref[c, pl.ds(halo - (CONV_K - 1), tm), :] * w[0:1, :]
        for k in range(1, CONV_K):
            acc = acc + stage_ref[c, pl.ds(halo - (CONV_K - 1) + k, tm), :] * w[k:k + 1, :]
        y_ref[c] = acc + b_ref[c]
        return carry

    lax.fori_loop(0, ns, strip, 0)

    tot = y_ref[0]
    for c in range(1, ns):
        tot = tot + y_ref[c]
    mu = jnp.sum(tot, axis=-1, keepdims=True) * (1.0 / d)
    sq = jnp.square(y_ref[0] - mu)
    for c in range(1, ns):
        sq = sq + jnp.square(y_ref[c] - mu)
    rstd = lax.rsqrt(jnp.sum(sq, axis=-1, keepdims=True) * (1.0 / d) + EPS)
    for c in range(ns):
        yn = (y_ref[c] - mu) * rstd * lg_ref[c] + lb_ref[c]
        a_ref[:, c * LANES:(c + 1) * LANES] = _silu(yn).astype(BF16)
    a = a_ref[...]
    for jn in range(d // tn):
        sl = slice(jn * tn, (jn + 1) * tn)
        o_ref[:, sl] = res_ref[:, sl] + jnp.dot(a, wo_ref[:, sl], preferred_element_type=F32)


def conv_layer(x, gain, w_in, dw_w, dw_b, ln_g, ln_b, w_out, *, tm=256):
    s, d = x.shape
    ch = dw_w.shape[1]
    ns = ch // LANES
    halo = 32
    xg = resident_matmul(x, w_in.astype(BF16), ch, w_offs=(0, ch), gain=gain, act="glu", name="conv_in")
    strips = lambda a: a.reshape(-1, ns, LANES).transpose(1, 0, 2)
    w_s = strips(jnp.pad(dw_w, ((0, halo - CONV_K), (0, 0))))
    const2 = lambda i: (0, 0)
    const3 = lambda i: (0, 0, 0)
    row = lambda i: (i, 0)
    return pl.pallas_call(
        functools.partial(_conv_kernel, tm=tm, halo=halo, tn=512),
        out_shape=jax.ShapeDtypeStruct((s, d), F32),
        grid=(s // tm,),
        in_specs=[
            pl.BlockSpec((tm, ch), row),
            pl.BlockSpec((ns, halo, LANES), const3),
            pl.BlockSpec((ns, 1, LANES), const3),
            pl.BlockSpec((ns, 1, LANES), const3),
            pl.BlockSpec((ns, 1, LANES), const3),
            pl.BlockSpec((ch, d), const2, pipeline_mode=pl.Buffered(1)),
            pl.BlockSpec((tm, d), row),
        ],
        out_specs=pl.BlockSpec((tm, d), row),
        scratch_shapes=[pltpu.VMEM((ns, tm + halo, LANES), F32), pltpu.VMEM((ns, tm, LANES), F32),
                        pltpu.VMEM((tm, ch), BF16)],
        compiler_params=_params("arbitrary"),
        name="conv_mod",
    )(xg, w_s, strips(dw_b.reshape(1, ch)), strips(ln_g.reshape(1, ch)), strips(ln_b.reshape(1, ch)),
      w_out.astype(BF16), x)


def kernel(x, positions, norm_mix, norm_ffn, final_norm, mla_w_in, mla_q_norm, mla_w_uq, mla_kv_norm,
           mla_w_ukv, mla_w_o, gdn_w_in, gdn_conv_w, gdn_a_log, gdn_dt_bias, gdn_o_norm, gdn_w_o, s5_w_in,
           s5_lam_re, s5_lam_im, s5_log_dt, s5_b_re, s5_b_im, s5_c_re, s5_c_im, s5_d, s5_w_out, cv_w_in,
           cv_dw_w, cv_dw_b, cv_ln_g, cv_ln_b, cv_w_out, ffn_w_gate, ffn_w_up, ffn_w_down):
    bsz, s, d = x.shape
    depth = norm_mix.shape[0]
    wg_b, wu_b, wd_b = ffn_w_gate, ffn_w_up, ffn_w_down
    outs = []
    for b in range(bsz):
        h = x[b]
        pos = positions[b]
        for i in range(depth):
            m, j = i % 4, i // 4
            if m == 0:
                h = mla_layer(h, pos, norm_mix[i], mla_w_in[j], mla_q_norm[j], mla_w_uq[j], mla_kv_norm[j],
                              mla_w_ukv[j], mla_w_o[j])
            elif m == 1:
                h = gdn_layer(h, norm_mix[i], gdn_w_in[j], gdn_conv_w[j], gdn_a_log[j], gdn_dt_bias[j],
                              gdn_o_norm[j], gdn_w_o[j])
            elif m == 2:
                h = s5_layer(h, norm_mix[i], s5_w_in[j], s5_lam_re[j], s5_lam_im[j], s5_log_dt[j], s5_b_re[j],
                             s5_b_im[j], s5_c_re[j], s5_c_im[j], s5_d[j], s5_w_out[j])
            else:
                h = conv_layer(h, norm_mix[i], cv_w_in[j], cv_dw_w[j], cv_dw_b[j], cv_ln_g[j], cv_ln_b[j],
                               cv_w_out[j])
            fg = final_norm if i == depth - 1 else None
            h = ffn(h, norm_ffn[i], wg_b, wu_b, wd_b, fg, layer=i)
        outs.append(h)
    return outs[0][None] if bsz == 1 else jnp.stack(outs)
```
